```python
import jax, jax.numpy as jnp
from jax import lax
import numpy as np

D_MODEL = 1024
BATCH = 4
SEQ = 4096
DEPTH = 2

N_MIXERS = 2
N_ATTN_LAYERS = (DEPTH + 1) // 2
N_HGRN_LAYERS = DEPTH // 2
ATTN_HEAD_DIM = 64
ATTN_HEADS = D_MODEL // ATTN_HEAD_DIM
DILATED_PATTERNS = ((128, 1), (512, 4), (2048, 16))
N_GROUPS = len(DILATED_PATTERNS)
ROPE_THETA = 10000.0
HGRN_EXPAND = 128
HGRN_HEADS = D_MODEL // HGRN_EXPAND
HGRN_DK = HGRN_EXPAND
HGRN_DV = D_MODEL // HGRN_HEADS
HGRN_CHUNK = 64
D_FF = 4 * D_MODEL
LN_EPS = 1e-5
RMS_EPS = 1e-6
DEEPNORM_ALPHA = (2 * DEPTH) ** 0.25
DEEPNORM_BETA = (8 * DEPTH) ** -0.25

kernel_name = 'hybrid_dilated_attn_hgrn2_deepnorm'

F32 = jnp.float32


def layer_norm(x, g, b):
    xf = x.astype(F32)
    mu = jnp.mean(xf, axis=-1, keepdims=True)
    var = jnp.mean(jnp.square(xf - mu), axis=-1, keepdims=True)
    return ((xf - mu) * lax.rsqrt(var + LN_EPS) * g.astype(F32) + b.astype(F32)).astype(x.dtype)


def rotary(x, pos):
    e = x.shape[-1]
    half = e // 2
    inv = ROPE_THETA ** (-jnp.arange(half, dtype=F32) * (2.0 / e))
    ang = pos.astype(F32)[:, None] * inv[None, :]
    cos = jnp.cos(ang)[None, :, None, :]
    sin = jnp.sin(ang)[None, :, None, :]
    xf = x.astype(F32)
    x1, x2 = xf[..., :half], xf[..., half:]
    return jnp.concatenate([x1 * cos - x2 * sin, x2 * cos + x1 * sin], axis=-1).astype(x.dtype)


def dilated_window_attention(q, k, v, window, dilation):
    B, S, H, E = q.shape
    blk = window // dilation
    span = dilation * blk
    s_pad = -(-S // span) * span
    n = s_pad // dilation
    nb = n // blk
    pad = ((0, 0), (0, s_pad - S), (0, 0), (0, 0))

    def to_blocks(t):
        t = jnp.pad(t, pad).reshape(B, n, dilation, H, E).transpose(0, 2, 1, 3, 4)
        return t.reshape(B * dilation, nb, blk, H, E)

    def with_prev(t):
        prev = jnp.pad(t, ((0, 0), (1, 0), (0, 0), (0, 0), (0, 0)))[:, :-1]
        return jnp.concatenate([prev, t], axis=2)

    qb, kb, vb = to_blocks(q), to_blocks(k), to_blocks(v)
    kk, vv = with_prev(kb), with_prev(vb)
    s = jnp.einsum('znqhe,znkhe->znhqk', qb, kk).astype(F32) * (E ** -0.5)
    qi = jnp.arange(blk)[:, None]
    kj = jnp.arange(2 * blk)[None, :]
    dist = qi + blk - kj
    in_band = (dist >= 0) & (dist <= blk)
    kabs = jnp.arange(nb)[:, None, None] * blk + kj[None] - blk
    valid = in_band[None] & (kabs >= 0)
    s = jnp.where(valid[None, :, None], s, -jnp.inf)
    m = jnp.max(s, axis=-1, keepdims=True)
    p = jnp.exp(s - m)
    l = jnp.sum(p, axis=-1, keepdims=True)
    o = jnp.einsum('znhqk,znkhe->znqhe', (p / l).astype(v.dtype), vv)
    lse = (m + jnp.log(l))[..., 0].transpose(0, 1, 3, 2)

    def from_blocks(t):
        t = t.reshape(B, dilation, n, *t.shape[3:]).swapaxes(1, 2)
        return t.reshape(B, s_pad, *t.shape[3:])[:, :S]

    return from_blocks(o), from_blocks(lse)


def dilated_attention_mixer(x, w_in, w_out):
    B, S, _ = x.shape
    proj = (x @ w_in).reshape(B, S, N_GROUPS, 3, ATTN_HEADS, ATTN_HEAD_DIM)
    pos = jnp.arange(S)
    outs, lses = [], []
    for g, (window, dilation) in enumerate(DILATED_PATTERNS):
        q = rotary(proj[:, :, g, 0], pos)
        k = rotary(proj[:, :, g, 1], pos)
        o, lse = dilated_window_attention(q, k, proj[:, :, g, 2], window, dilation)
        outs.append(o)
        lses.append(lse)
    wts = jax.nn.softmax(jnp.stack(lses, axis=0), axis=0)
    o = jnp.einsum('gbsh,gbshe->bshe', wts, jnp.stack(outs, axis=0).astype(F32))
    return o.reshape(B, S, ATTN_HEADS * ATTN_HEAD_DIM).astype(x.dtype) @ w_out


def forget_lower_bounds(lb_logits):
    c = jnp.cumsum(jax.nn.softmax(lb_logits.astype(F32), axis=0), axis=0)
    return c - c[0]


def hgrn2_mixer(x, w_in, w_out, norm_g, lb):
    B, S, _ = x.shape
    H, K, V, C = HGRN_HEADS, HGRN_DK, HGRN_DV, HGRN_CHUNK
    nc = S // C
    q_raw, f_raw, i_raw = jnp.split(x @ w_in, [H * K, 2 * H * K], axis=-1)
    z = f_raw.astype(F32)
    log_f = jnp.logaddexp(jnp.log(lb), jnp.log1p(-lb) + jax.nn.log_sigmoid(z))
    key = (1.0 - lb) * jax.nn.sigmoid(-z)
    q = jax.nn.silu(q_raw.astype(F32))
    v = i_raw.astype(F32)

    def chunks(t, d):
        return t.reshape(B, nc, C, H, d).transpose(0, 3, 1, 2, 4)

    q, key, log_f, v = chunks(q, K), chunks(key, K), chunks(log_f, K), chunks(v, V)
    b = jnp.cumsum(log_f, axis=3)
    q_dec = q * jnp.exp(b)
    k_dec = key * jnp.exp(-b)
    causal = jnp.tril(jnp.ones((C, C), dtype=bool))
    a = jnp.where(causal, jnp.einsum('bhcid,bhcjd->bhcij', q_dec, k_dec), 0.0)
    o_intra = jnp.einsum('bhcij,bhcje->bhcie', a, v)
    b_last = b[:, :, :, -1:, :]
    kv = jnp.einsum('bhcjd,bhcje->bhcde', key * jnp.exp(b_last - b), v)
    chunk_decay = jnp.exp(b_last[:, :, :, 0, :])

    def step(state, inp):
        dec, kv_c = inp
        return dec[..., None] * state + kv_c, state

    s0 = jnp.zeros((B, H, K, V), F32)
    _, states = lax.scan(step, s0, (jnp.moveaxis(chunk_decay, 2, 0), jnp.moveaxis(kv, 2, 0)))
    o_inter = jnp.einsum('bhcid,cbhde->bhcie', q_dec, states)
    o = (o_intra + o_inter).transpose(0, 2, 3, 1, 4).reshape(B, S, H, V)
    o = o * lax.rsqrt(jnp.mean(o * o, axis=-1, keepdims=True) + RMS_EPS) * norm_g.astype(F32).reshape(H, V)
    return o.reshape(B, S, H * V).astype(x.dtype) @ w_out


def squared_relu_mlp(x, w_up, w_down):
    return jnp.square(jax.nn.relu(x @ w_up)) @ w_down


def setup_inputs(seed: int = 0) -> dict:
    key = jax.random.key(seed)
    ks = jax.random.split(key, 13)
    d_attn_in = N_GROUPS * 3 * ATTN_HEADS * ATTN_HEAD_DIM
    d_attn_out = ATTN_HEADS * ATTN_HEAD_DIM
    d_hgrn_in = 2 * HGRN_HEADS * HGRN_DK + HGRN_HEADS * HGRN_DV
    d_hgrn_out = HGRN_HEADS * HGRN_DV
    nrm = lambda k, shape, scale: jax.random.normal(k, shape, F32) * scale
    return {
        'x': nrm(ks[0], (BATCH, SEQ, D_MODEL), 1.0),
        'attn_w_in': nrm(ks[1], (N_ATTN_LAYERS, D_MODEL, d_attn_in), D_MODEL ** -0.5),
        'attn_w_out': nrm(ks[2], (N_ATTN_LAYERS, d_attn_out, D_MODEL), d_attn_out ** -0.5 * DEEPNORM_BETA),
        'hgrn_w_in': nrm(ks[3], (N_HGRN_LAYERS, D_MODEL, d_hgrn_in), D_MODEL ** -0.5),
        'hgrn_w_out': nrm(ks[4], (N_HGRN_LAYERS, d_hgrn_out, D_MODEL), d_hgrn_out ** -0.5 * DEEPNORM_BETA),
        'hgrn_norm_g': 1.0 + nrm(ks[5], (N_HGRN_LAYERS, d_hgrn_out), 0.02),
        'lb_logits': nrm(ks[6], (DEPTH, HGRN_HEADS * HGRN_DK), 0.1),
        'ln_mix_g': 1.0 + nrm(ks[7], (DEPTH, D_MODEL), 0.02),
        'ln_mix_b': nrm(ks[8], (DEPTH, D_MODEL), 0.02),
        'ln_ffn_g': 1.0 + nrm(ks[9], (DEPTH, D_MODEL), 0.02),
        'ln_ffn_b': nrm(ks[10], (DEPTH, D_MODEL), 0.02),
        'ffn_w_up': nrm(ks[11], (DEPTH, D_MODEL, D_FF), D_MODEL ** -0.5),
        'ffn_w_down': nrm(ks[12], (DEPTH, D_FF, D_MODEL), D_FF ** -0.5 * DEEPNORM_BETA),
    }


def reference(x, attn_w_in, attn_w_out, hgrn_w_in, hgrn_w_out, hgrn_norm_g, lb_logits,
              ln_mix_g, ln_mix_b, ln_ffn_g, ln_ffn_b, ffn_w_up, ffn_w_down):
    lbs = forget_lower_bounds(lb_logits)
    for i in range(DEPTH):
        j = i // N_MIXERS
        if i % N_MIXERS == 0:
            y = dilated_attention_mixer(x, attn_w_in[j], attn_w_out[j])
        else:
            y = hgrn2_mixer(x, hgrn_w_in[j], hgrn_w_out[j], hgrn_norm_g[j], lbs[i])
        x = layer_norm(DEEPNORM_ALPHA * x + y, ln_mix_g[i], ln_mix_b[i])
        y = squared_relu_mlp(x, ffn_w_up[i], ffn_w_down[i])
        x = layer_norm(DEEPNORM_ALPHA * x + y, ln_ffn_g[i], ln_ffn_b[i])
    return x
```

```python
import functools

import jax
import jax.numpy as jnp
from jax import lax
from jax.experimental import pallas as pl
from jax.experimental.pallas import tpu as pltpu

F32 = jnp.float32
BF16 = jnp.bfloat16

D_MODEL = 1024
DEPTH = 2
ATTN_HEAD_DIM = 64
ATTN_HEADS = D_MODEL // ATTN_HEAD_DIM
DILATED_PATTERNS = ((128, 1), (512, 4), (2048, 16))
N_GROUPS = len(DILATED_PATTERNS)
ROPE_THETA = 10000.0
HGRN_HEADS = 8
HGRN_DK = 128
HGRN_DV = 128
HGRN_CHUNK = 64
D_FF = 4 * D_MODEL
LN_EPS = 1e-5
RMS_EPS = 1e-6
DEEPNORM_ALPHA = (2 * DEPTH) ** 0.25

LANES = 128
ATTN_BLK = 128
MASK_VALUE = -1e30
VMEM_LIMIT = 56 * 1024 * 1024

QKV_TM = 1024
OUT_TM = 512
FFN_TM = 512
HGRN_TM = 256


def _layer_norm(y, g, b):
    mu = jnp.mean(y, axis=-1, keepdims=True)
    d = y - mu
    var = jnp.mean(d * d, axis=-1, keepdims=True)
    return d * lax.rsqrt(var + LN_EPS) * g + b


def _resident(shape):
    nd = len(shape)
    return pl.BlockSpec(shape, lambda *_: (0,) * nd, pipeline_mode=pl.Buffered(1))


def _qkv_rope_kernel(x_ref, w_ref, tab_ref, o_ref, xb_ref):
    j = pl.program_id(1)

    @pl.when(j == 0)
    def _():
        xb_ref[...] = x_ref[...].astype(BF16)

    acc = jnp.dot(xb_ref[...], w_ref[...], preferred_element_type=F32)
    kind = j % 3

    @pl.when(kind == 2)
    def _():
        o_ref[...] = acc.astype(BF16)

    @pl.when(kind != 2)
    def _():
        cos = tab_ref[0]
        sin = tab_ref[1]
        lane = lax.broadcasted_iota(jnp.int32, (1, LANES), 1)
        first_half = (lane % ATTN_HEAD_DIM) < (ATTN_HEAD_DIM // 2)
        for c in range(D_MODEL // LANES):
            a = acc[:, c * LANES:(c + 1) * LANES]
            rot = jnp.where(first_half, pltpu.roll(a, LANES - 32, 1), pltpu.roll(a, 32, 1))
            o_ref[:, c * LANES:(c + 1) * LANES] = (a * cos + rot * sin).astype(BF16)


def _rope_tables(seq):
    half = ATTN_HEAD_DIM // 2
    inv = ROPE_THETA ** (-jnp.arange(half, dtype=F32) * (2.0 / ATTN_HEAD_DIM))
    ang = jnp.arange(seq, dtype=F32)[:, None] * inv[None, :]
    cos = jnp.cos(ang)
    sin = jnp.sin(ang)
    cos_t = jnp.tile(jnp.concatenate([cos, cos], axis=-1), (1, LANES // ATTN_HEAD_DIM))
    sin_t = jnp.tile(jnp.concatenate([-sin, sin], axis=-1), (1, LANES // ATTN_HEAD_DIM))
    k_tab = jnp.stack([cos_t, sin_t])
    q_tab = k_tab * (ATTN_HEAD_DIM ** -0.5)
    return jnp.stack([q_tab, k_tab])


def _qkv_rope(x2d, w_in_bf16, seq):
    t = x2d.shape[0]
    tm = QKV_TM
    n_blk = w_in_bf16.shape[1] // D_MODEL
    tabs = _rope_tables(seq)
    tiles_per_seq = seq // tm
    return pl.pallas_call(
        _qkv_rope_kernel,
        out_shape=jax.ShapeDtypeStruct((n_blk, t, D_MODEL), BF16),
        grid=(t // tm, n_blk),
        in_specs=[
            pl.BlockSpec((tm, D_MODEL), lambda i, j: (i, 0)),
            pl.BlockSpec((D_MODEL, D_MODEL), lambda i, j: (0, j)),
            pl.BlockSpec((None, 2, tm, LANES),
                         lambda i, j: (jnp.minimum(j % 3, 1), 0, i % tiles_per_seq, 0)),
        ],
        out_specs=pl.BlockSpec((None, tm, D_MODEL), lambda i, j: (j, i, 0)),
        scratch_shapes=[pltpu.VMEM((tm, D_MODEL), BF16)],
        compiler_params=pltpu.CompilerParams(
            dimension_semantics=("arbitrary", "arbitrary"), vmem_limit_bytes=VMEM_LIMIT),
        name="qkv_rope",
    )(x2d, w_in_bf16, tabs)


def _attn_kernel(q_ref, kp_ref, kc_ref, vp_ref, vc_ref, o_ref, lse_ref):
    i = pl.program_id(2)
    blk = ATTN_BLK
    row = lax.broadcasted_iota(jnp.int32, (blk, 2 * blk), 0)
    col = lax.broadcasted_iota(jnp.int32, (blk, 2 * blk), 1)
    valid = (col >= row) & (col <= row + blk) & ((col >= blk) | (i > 0))
    bias = jnp.where(valid, 0.0, MASK_VALUE).astype(F32)
    bias2 = jnp.concatenate([bias, bias], axis=0)
    lane = lax.broadcasted_iota(jnp.int32, (blk, LANES), 1)
    low = lane < ATTN_HEAD_DIM
    lse_tile = jnp.zeros((blk, LANES), F32)
    for p in range(ATTN_HEADS // 2):
        sl = slice(p * LANES, (p + 1) * LANES)
        q = q_ref[:, sl]
        zero = jnp.zeros_like(q)
        qs = jnp.concatenate([jnp.where(low, q, zero), jnp.where(low, zero, q)], axis=0)
        k = jnp.concatenate([kp_ref[:, sl], kc_ref[:, sl]], axis=0)
        v = jnp.concatenate([vp_ref[:, sl], vc_ref[:, sl]], axis=0)
        s = lax.dot_general(qs, k, (((1,), (1,)), ((), ())), preferred_element_type=F32) + bias2
        m = jnp.max(s, axis=-1, keepdims=True)
        e = jnp.exp(s - m)
        l = jnp.sum(e, axis=-1, keepdims=True)
        acc = jnp.dot(e.astype(BF16), v, preferred_element_type=F32)
        o2 = acc / l
        o_ref[:, sl] = jnp.where(low, o2[:blk], o2[blk:])
        lse = m + jnp.log(l)
        lse_tile = jnp.where(lane == 2 * p, lse[:blk], lse_tile)
        lse_tile = jnp.where(lane == 2 * p + 1, lse[blk:], lse_tile)
    lse_ref[...] = lse_tile


def _attention_group(qkv, g, dilation, batch, seq):
    n = seq // dilation
    nb = n // ATTN_BLK
    view = qkv.reshape(3 * N_GROUPS, batch, n, dilation * D_MODEL)

    def cur(which):
        return pl.BlockSpec((None, None, ATTN_BLK, D_MODEL),
                            lambda b, r, i: (3 * g + which, b, i, r))

    def prev(which):
        return pl.BlockSpec((None, None, ATTN_BLK, D_MODEL),
                            lambda b, r, i: (3 * g + which, b, jnp.maximum(i - 1, 0), r))

    o, lse = pl.pallas_call(
        _attn_kernel,
        out_shape=(jax.ShapeDtypeStruct((batch, n, dilation * D_MODEL), F32),
                   jax.ShapeDtypeStruct((batch, n, dilation * LANES), F32)),
        grid=(batch, dilation, nb),
        in_specs=[cur(0), prev(1), cur(1), prev(2), cur(2)],
        out_specs=(pl.BlockSpec((None, ATTN_BLK, D_MODEL), lambda b, r, i: (b, i, r)),
                   pl.BlockSpec((None, ATTN_BLK, LANES), lambda b, r, i: (b, i, r))),
        compiler_params=pltpu.CompilerParams(
            dimension_semantics=("arbitrary", "arbitrary", "arbitrary"),
            vmem_limit_bytes=VMEM_LIMIT),
        name=f"dilated_attn_g{g}",
    )(view, view, view, view, view)
    t = batch * seq
    return o.reshape(t, D_MODEL), lse.reshape(t, LANES)


def _attn_out_kernel(o0_ref, o1_ref, o2_ref, l0_ref, l1_ref, l2_ref, x_ref, w_ref, g_ref, b_ref,
                     y_ref, mix_ref):
    o_refs = (o0_ref, o1_ref, o2_ref)
    lses = [l0_ref[...], l1_ref[...], l2_ref[...]]
    mx = jnp.maximum(jnp.maximum(lses[0], lses[1]), lses[2])
    es = [jnp.exp(v - mx) for v in lses]
    inv = 1.0 / (es[0] + es[1] + es[2])
    wts = [e * inv for e in es]
    tm = x_ref.shape[0]
    lane = lax.broadcasted_iota(jnp.int32, (tm, LANES), 1)
    low = lane < ATTN_HEAD_DIM
    for p in range(ATTN_HEADS // 2):
        sl = slice(p * LANES, (p + 1) * LANES)
        mixed = jnp.zeros((tm, LANES), F32)
        for g in range(N_GROUPS):
            w_pair = jnp.where(low, wts[g][:, 2 * p:2 * p + 1], wts[g][:, 2 * p + 1:2 * p + 2])
            mixed = mixed + w_pair * o_refs[g][:, sl]
        mix_ref[:, sl] = mixed.astype(BF16)
    y = jnp.dot(mix_ref[...], w_ref[...], preferred_element_type=F32)
    y_ref[...] = _layer_norm(DEEPNORM_ALPHA * x_ref[...] + y, g_ref[...], b_ref[...])


def _attn_out(os_, lses, x2d, w_out_bf16, ln_g, ln_b):
    t = x2d.shape[0]
    tm = OUT_TM
    row = lambda w: pl.BlockSpec((tm, w), lambda i: (i, 0))
    return pl.pallas_call(
        _attn_out_kernel,
        out_shape=jax.ShapeDtypeStruct((t, D_MODEL), F32),
        grid=(t // tm,),
        in_specs=[row(D_MODEL)] * 3 + [row(LANES)] * 3 + [row(D_MODEL),
                  _resident((D_MODEL, D_MODEL)), _resident((1, D_MODEL)), _resident((1, D_MODEL))],
        out_specs=row(D_MODEL),
        scratch_shapes=[pltpu.VMEM((tm, D_MODEL), BF16)],
        compiler_params=pltpu.CompilerParams(
            dimension_semantics=("arbitrary",), vmem_limit_bytes=VMEM_LIMIT),
        name="attn_out_ln",
    )(*os_, *lses, x2d, w_out_bf16, ln_g, ln_b)


def _ffn_kernel(x_ref, wu_ref, wd_ref, g_ref, b_ref, y_ref, h_ref):
    x = x_ref[...]
    xb = x.astype(BF16)
    for c in range(D_FF // D_MODEL):
        sl = slice(c * D_MODEL, (c + 1) * D_MODEL)
        h = jnp.dot(xb, wu_ref[:, sl], preferred_element_type=F32)
        h_ref[:, sl] = jnp.square(jnp.maximum(h, 0.0)).astype(BF16)
    y = jnp.dot(h_ref[...], wd_ref[...], preferred_element_type=F32)
    y_ref[...] = _layer_norm(DEEPNORM_ALPHA * x + y, g_ref[...], b_ref[...])


def _ffn(x2d, w_up_bf16, w_down_bf16, ln_g, ln_b):
    t = x2d.shape[0]
    tm = FFN_TM
    return pl.pallas_call(
        _ffn_kernel,
        out_shape=jax.ShapeDtypeStruct((t, D_MODEL), F32),
        grid=(t // tm,),
        in_specs=[pl.BlockSpec((tm, D_MODEL), lambda i: (i, 0)),
                  _resident((D_MODEL, D_FF)), _resident((D_FF, D_MODEL)),
                  _resident((1, D_MODEL)), _resident((1, D_MODEL))],
        out_specs=pl.BlockSpec((tm, D_MODEL), lambda i: (i, 0)),
        scratch_shapes=[pltpu.VMEM((tm, D_FF), BF16)],
        compiler_params=pltpu.CompilerParams(
            dimension_semantics=("arbitrary",), vmem_limit_bytes=VMEM_LIMIT),
        name="ffn_ln",
    )(x2d, w_up_bf16, w_down_bf16, ln_g, ln_b)


def _hgrn_kernel(layer, x_ref, wi_ref, wo_ref, lbl_ref, ng_ref, g_ref, b_ref, y_ref,
                 state_ref, on_ref):
    tm = x_ref.shape[0]
    hk = HGRN_HEADS * HGRN_DK
    c_len = HGRN_CHUNK

    @pl.when(pl.program_id(1) == 0)
    def _():
        state_ref[...] = jnp.zeros_like(state_ref)

    logits = lbl_ref[...]
    ex = jnp.exp(logits - jnp.max(logits, axis=0, keepdims=True))
    sm = ex / jnp.sum(ex, axis=0, keepdims=True)
    lb = jnp.sum(sm[1:layer + 1], axis=0, keepdims=True)

    x = x_ref[...]
    proj = jnp.dot(x.astype(BF16), wi_ref[...], preferred_element_type=F32)
    q_raw = proj[:, :hk]
    z = proj[:, hk:2 * hk]
    v = proj[:, 2 * hk:]

    key = (1.0 - lb) / (1.0 + jnp.exp(z))
    log_f = jnp.log(1.0 - key)
    q = q_raw / (1.0 + jnp.exp(-q_raw))

    ri = lax.broadcasted_iota(jnp.int32, (tm, tm), 0)
    ci = lax.broadcasted_iota(jnp.int32, (tm, tm), 1)
    tri = ((ri // c_len == ci // c_len) & (ci <= ri)).astype(BF16)
    hi = log_f.astype(BF16)
    r1 = log_f - hi.astype(F32)
    mid = r1.astype(BF16)
    lo = (r1 - mid.astype(F32)).astype(BF16)
    bcum = (jnp.dot(tri, hi, preferred_element_type=F32)
            + jnp.dot(tri, mid, preferred_element_type=F32)
            + jnp.dot(tri, lo, preferred_element_type=F32))

    causal = (lax.broadcasted_iota(jnp.int32, (c_len, c_len), 1)
              <= lax.broadcasted_iota(jnp.int32, (c_len, c_len), 0))
    ng = ng_ref[...]
    for c in range(tm // c_len):
        rows = slice(c * c_len, (c + 1) * c_len)
        b_c = bcum[rows]
        b_last = b_c[c_len - 1:c_len]
        q_dec = (q[rows] * jnp.exp(b_c)).astype(BF16)
        k_dec = (key[rows] * jnp.exp(-b_c)).astype(BF16)
        k_end = (key[rows] * jnp.exp(b_last - b_c)).astype(BF16)
        decay = jnp.exp(b_last)
        v_c = v[rows].astype(BF16)
        for h in range(HGRN_HEADS):
            ls = slice(h * HGRN_DK, (h + 1) * HGRN_DK)
            qd, kd, ke, vv = q_dec[:, ls], k_dec[:, ls], k_end[:, ls], v_c[:, ls]
            a = lax.dot_general(qd, kd, (((1,), (1,)), ((), ())), preferred_element_type=F32)
            a = jnp.where(causal, a, 0.0).astype(BF16)
            o = jnp.dot(a, vv, preferred_element_type=F32)
            st = state_ref[h]
            o = o + lax.dot_general(qd, st.astype(BF16), (((1,), (1,)), ((), ())),
                                    preferred_element_type=F32)
            kv_t = lax.dot_general(vv, ke, (((0,), (0,)), ((), ())), preferred_element_type=F32)
            state_ref[h] = decay[:, ls] * st + kv_t
            o = o * lax.rsqrt(jnp.mean(o * o, axis=-1, keepdims=True) + RMS_EPS) * ng[:, ls]
            on_ref[rows, ls] = o.astype(BF16)

    y = jnp.dot(on_ref[...], wo_ref[...], preferred_element_type=F32)
    y_ref[...] = _layer_norm(DEEPNORM_ALPHA * x + y, g_ref[...], b_ref[...])


def _hgrn_mixer(layer, x2d, w_in_bf16, w_out_bf16, lb_logits, norm_g, ln_g, ln_b, batch, seq):
    t = x2d.shape[0]
    tm = HGRN_TM
    tiles = seq // tm
    d_in = w_in_bf16.shape[1]
    row = pl.BlockSpec((tm, D_MODEL), lambda b, i: (b * tiles + i, 0))
    return pl.pallas_call(
        functools.partial(_hgrn_kernel, layer),
        out_shape=jax.ShapeDtypeStruct((t, D_MODEL), F32),
        grid=(batch, tiles),
        in_specs=[row, _resident((D_MODEL, d_in)), _resident((D_MODEL, D_MODEL)),
                  _resident((DEPTH, D_MODEL)), _resident((1, D_MODEL)),
                  _resident((1, D_MODEL)), _resident((1, D_MODEL))],
        out_specs=row,
        scratch_shapes=[pltpu.VMEM((HGRN_HEADS, HGRN_DV, HGRN_DK), F32),
                        pltpu.VMEM((tm, D_MODEL), BF16)],
        compiler_params=pltpu.CompilerParams(
            dimension_semantics=("arbitrary", "arbitrary"), vmem_limit_bytes=VMEM_LIMIT),
        name="hgrn2_mixer_ln",
    )(x2d, w_in_bf16, w_out_bf16, lb_logits, norm_g, ln_g, ln_b)


def kernel(x, attn_w_in, attn_w_out, hgrn_w_in, hgrn_w_out, hgrn_norm_g, lb_logits,
           ln_mix_g, ln_mix_b, ln_ffn_g, ln_ffn_b, ffn_w_up, ffn_w_down):
    batch, seq, d = x.shape
    assert d == D_MODEL and lb_logits.shape[0] == DEPTH
    for window, dilation in DILATED_PATTERNS:
        assert window // dilation == ATTN_BLK and seq % window == 0
    assert seq % QKV_TM == 0 and seq % HGRN_TM == 0 and HGRN_TM % HGRN_CHUNK == 0
    h = x.reshape(batch * seq, d)
    row = lambda a: a.reshape(1, -1)
    for i in range(DEPTH):
        j = i // 2
        if i % 2 == 0:
            qkv = _qkv_rope(h, attn_w_in[j].astype(BF16), seq)
            outs = [_attention_group(qkv, g, dil, batch, seq)
                    for g, (_, dil) in enumerate(DILATED_PATTERNS)]
            h = _attn_out([o for o, _ in outs], [l for _, l in outs], h,
                          attn_w_out[j].astype(BF16), row(ln_mix_g[i]), row(ln_mix_b[i]))
        else:
            h = _hgrn_mixer(i, h, hgrn_w_in[j].astype(BF16), hgrn_w_out[j].astype(BF16),
                            lb_logits, row(hgrn_norm_g[j]), row(ln_mix_g[i]), row(ln_mix_b[i]),
                            batch, seq)
        h = _ffn(h, ffn_w_up[i].astype(BF16), ffn_w_down[i].astype(BF16),
                 row(ln_ffn_g[i]), row(ln_ffn_b[i]))
    return h.reshape(batch, seq, d)
```

```python
import functools

import jax
import jax.numpy as jnp
from jax import lax
from jax.experimental import pallas as pl
from jax.experimental.pallas import tpu as pltpu

F32 = jnp.float32
BF16 = jnp.bfloat16

D_MODEL = 1024
DEPTH = 2
ATTN_HEAD_DIM = 64
ATTN_HEADS = D_MODEL // ATTN_HEAD_DIM
DILATED_PATTERNS = ((128, 1), (512, 4), (2048, 16))
N_GROUPS = len(DILATED_PATTERNS)
MAX_DILATION = max(d for _, d in DILATED_PATTERNS)
ROPE_THETA = 10000.0
HGRN_HEADS = 8
HGRN_DK = 128
HGRN_DV = 128
HGRN_CHUNK = 64
D_FF = 4 * D_MODEL
LN_EPS = 1e-5
RMS_EPS = 1e-6
DEEPNORM_ALPHA = (2 * DEPTH) ** 0.25

LANES = 128
ATTN_BLK = 128
HALF = ATTN_HEAD_DIM // 2
MASK_VALUE = -1e30
VMEM_LIMIT = 56 * 1024 * 1024

QKV_TM = 1024
QKV_SUB = 256
OUT_TM = 512
FFN_TM = 512
HGRN_TM = 256


def _layer_norm(y, g, b):
    mu = jnp.mean(y, axis=-1, keepdims=True)
    d = y - mu
    var = jnp.mean(d * d, axis=-1, keepdims=True)
    return d * lax.rsqrt(var + LN_EPS) * g + b


def _resident(shape):
    nd = len(shape)
    return pl.BlockSpec(shape, lambda *_: (0,) * nd, pipeline_mode=pl.Buffered(1))


def _qkv_rope_kernel(dilation, x_ref, w_ref, tab_ref, o_ref, xb_ref, xs_ref):
    j = pl.program_id(2)
    tm = x_ref.shape[0]
    n_per = tm // dilation

    @pl.when(j == 0)
    def _():
        if dilation == 1:
            xb_ref[...] = x_ref[...].astype(BF16)
        else:
            for c in range(D_MODEL // LANES):
                xs_ref[c] = x_ref[:, c * LANES:(c + 1) * LANES]
            for r in range(dilation):
                xb_ref[r * n_per:(r + 1) * n_per, :] = jnp.concatenate(
                    [xs_ref[c, pl.ds(r, n_per, stride=dilation), :].astype(BF16)
                     for c in range(D_MODEL // LANES)], axis=1)

    def store(s, val):
        if n_per >= QKV_SUB:
            start = s * QKV_SUB
            o_ref[start // n_per, start % n_per:start % n_per + QKV_SUB, :] = val
        else:
            per = QKV_SUB // n_per
            for c in range(per):
                o_ref[s * per + c] = val[c * n_per:(c + 1) * n_per]

    @pl.when(j == 2)
    def _():
        for s in range(tm // QKV_SUB):
            rows = slice(s * QKV_SUB, (s + 1) * QKV_SUB)
            acc = jnp.dot(xb_ref[rows], w_ref[...], preferred_element_type=F32)
            store(s, acc.astype(BF16))

    @pl.when(j != 2)
    def _():
        for s in range(tm // QKV_SUB):
            rows = slice(s * QKV_SUB, (s + 1) * QKV_SUB)
            acc = jnp.dot(xb_ref[rows], w_ref[...], preferred_element_type=F32)
            cos = tab_ref[0, rows, :]
            sin = tab_ref[1, rows, :]
            pieces = []
            for c in range(D_MODEL // LANES):
                a = acc[:, c * LANES:(c + 1) * LANES]
                pieces.append((a * cos + pltpu.roll(a, LANES // 2, 1) * sin).astype(BF16))
            store(s, jnp.concatenate(pieces, axis=1))


def _rope_tables(seq, dilation, tm):
    inv = ROPE_THETA ** (-jnp.arange(HALF, dtype=F32) * (2.0 / ATTN_HEAD_DIM))
    ang = jnp.arange(seq, dtype=F32)[:, None] * inv[None, :]
    cos = jnp.tile(jnp.cos(ang), (1, LANES // HALF))
    sin = jnp.tile(jnp.sin(ang), (1, LANES // HALF))
    sign = jnp.where(jnp.arange(LANES) < LANES // 2, -1.0, 1.0).astype(F32)
    k_tab = jnp.stack([cos, sin * sign])
    k_tab = k_tab.reshape(2, seq // tm, tm // dilation, dilation, LANES)
    k_tab = k_tab.transpose(0, 1, 3, 2, 4).reshape(2, seq, LANES)
    q_tab = k_tab * (ATTN_HEAD_DIM ** -0.5)
    return jnp.stack([q_tab, k_tab])


def _pair_split_columns(w):
    k = w.shape[0]
    w = w.reshape(k, ATTN_HEADS // 2, 2, 2, HALF)
    return w.transpose(0, 1, 3, 2, 4).reshape(k, D_MODEL)


def _group_weights(w_in, g):
    base = 3 * g * D_MODEL
    wq = _pair_split_columns(w_in[:, base:base + D_MODEL])
    wk = _pair_split_columns(w_in[:, base + D_MODEL:base + 2 * D_MODEL])
    wv = w_in[:, base + 2 * D_MODEL:base + 3 * D_MODEL]
    return jnp.stack([wq, wk, wv]).astype(BF16)


def _qkv_rope(x2d, w3, dilation, batch, seq):
    tm = QKV_TM
    tiles = seq // tm
    n_per = tm // dilation
    tabs = _rope_tables(seq, dilation, tm)
    return pl.pallas_call(
        functools.partial(_qkv_rope_kernel, dilation),
        out_shape=jax.ShapeDtypeStruct((3, batch, dilation, seq // dilation, D_MODEL), BF16),
        grid=(batch, tiles, 3),
        in_specs=[
            pl.BlockSpec((tm, D_MODEL), lambda b, i, j: (b * tiles + i, 0)),
            pl.BlockSpec((None, D_MODEL, D_MODEL), lambda b, i, j: (j, 0, 0)),
            pl.BlockSpec((None, 2, tm, LANES), lambda b, i, j: (jnp.minimum(j, 1), 0, i, 0)),
        ],
        out_specs=pl.BlockSpec((None, None, dilation, n_per, D_MODEL),
                               lambda b, i, j: (j, b, 0, i, 0)),
        scratch_shapes=[pltpu.VMEM((tm, D_MODEL), BF16),
                        pltpu.VMEM((D_MODEL // LANES, tm, LANES), F32)],
        compiler_params=pltpu.CompilerParams(
            dimension_semantics=("arbitrary", "arbitrary", "arbitrary"),
            vmem_limit_bytes=VMEM_LIMIT),
        name=f"qkv_rope_d{dilation}",
    )(x2d, w3, tabs)


def _attn_kernel(q_ref, kp_ref, kc_ref, vp_ref, vc_ref, o_ref, lse_ref):
    i = pl.program_id(2)
    blk = ATTN_BLK
    row = lax.broadcasted_iota(jnp.int32, (blk, 2 * blk), 0)
    col = lax.broadcasted_iota(jnp.int32, (blk, 2 * blk), 1)
    valid = (col >= row) & (col <= row + blk) & ((col >= blk) | (i > 0))
    bias = jnp.where(valid, 0.0, MASK_VALUE).astype(F32)
    bias2 = jnp.concatenate([bias, bias], axis=0)
    lane = lax.broadcasted_iota(jnp.int32, (blk, LANES), 1)
    qk_head0 = (lane // HALF) % 2 == 0
    v_head0 = lane < ATTN_HEAD_DIM
    lse_tile = jnp.zeros((blk, LANES), F32)
    for p in range(ATTN_HEADS // 2):
        sl = slice(p * LANES, (p + 1) * LANES)
        q = q_ref[:, sl]
        zero = jnp.zeros_like(q)
        qs = jnp.concatenate([jnp.where(qk_head0, q, zero), jnp.where(qk_head0, zero, q)], axis=0)
        k = jnp.concatenate([kp_ref[:, sl], kc_ref[:, sl]], axis=0)
        v = jnp.concatenate([vp_ref[:, sl], vc_ref[:, sl]], axis=0)
        s = lax.dot_general(qs, k, (((1,), (1,)), ((), ())), preferred_element_type=F32) + bias2
        m = jnp.max(s, axis=-1, keepdims=True)
        e = jnp.exp(s - m)
        l = jnp.sum(e, axis=-1, keepdims=True)
        acc = jnp.dot(e.astype(BF16), v, preferred_element_type=F32)
        o2 = acc / l
        o_ref[p] = jnp.where(v_head0, o2[:blk], o2[blk:])
        lse = m + jnp.log(l)
        lse_tile = jnp.where(lane == 2 * p, lse[:blk], lse_tile)
        lse_tile = jnp.where(lane == 2 * p + 1, lse[blk:], lse_tile)
    lse_ref[...] = lse_tile


def _attention_group(qkv, dilation, batch, seq):
    n = seq // dilation
    nb = n // ATTN_BLK
    pairs = ATTN_HEADS // 2

    def cur(which):
        return pl.BlockSpec((None, None, None, ATTN_BLK, D_MODEL),
                            lambda b, r, i: (which, b, r, i, 0))

    def prev(which):
        return pl.BlockSpec((None, None, None, ATTN_BLK, D_MODEL),
                            lambda b, r, i: (which, b, r, jnp.maximum(i - 1, 0), 0))

    return pl.pallas_call(
        _attn_kernel,
        out_shape=(jax.ShapeDtypeStruct((batch, dilation, pairs, n, LANES), F32),
                   jax.ShapeDtypeStruct((batch, dilation, n, LANES), F32)),
        grid=(batch, dilation, nb),
        in_specs=[cur(0), prev(1), cur(1), prev(2), cur(2)],
        out_specs=(pl.BlockSpec((None, None, pairs, ATTN_BLK, LANES),
                                lambda b, r, i: (b, r, 0, i, 0)),
                   pl.BlockSpec((None, None, ATTN_BLK, LANES), lambda b, r, i: (b, r, i, 0))),
        compiler_params=pltpu.CompilerParams(
            dimension_semantics=("arbitrary", "arbitrary", "arbitrary"),
            vmem_limit_bytes=VMEM_LIMIT),
        name=f"dilated_attn_d{dilation}",
    )(qkv, qkv, qkv, qkv, qkv)


def _class_rows(ref, lead, dilation, r16, n16):
    step = MAX_DILATION // dilation
    if step == 1:
        return ref[(r16, *lead)]
    return ref[(r16 % dilation, *lead, pl.ds(r16 // dilation, n16, stride=step), slice(None))]


def _attn_out_kernel(o0_ref, o1_ref, o2_ref, l0_ref, l1_ref, l2_ref, x_ref, w_ref, g_ref, b_ref,
                     y_ref, mix_ref, proj_ref):
    tm = x_ref.shape[0]
    n16 = tm // MAX_DILATION
    n_chunks = D_MODEL // LANES
    o_refs = (o0_ref, o1_ref, o2_ref)
    l_refs = (l0_ref, l1_ref, l2_ref)
    dils = [d for _, d in DILATED_PATTERNS]
    lane = lax.broadcasted_iota(jnp.int32, (n16, LANES), 1)
    low = lane < ATTN_HEAD_DIM
    for r16 in range(MAX_DILATION):
        lses = [_class_rows(l_refs[g], (), dils[g], r16, n16) for g in range(N_GROUPS)]
        mx = jnp.maximum(jnp.maximum(lses[0], lses[1]), lses[2])
        es = [jnp.exp(v - mx) for v in lses]
        inv = 1.0 / (es[0] + es[1] + es[2])
        wts = [e * inv for e in es]
        pieces = []
        for p in range(ATTN_HEADS // 2):
            mixed = jnp.zeros((n16, LANES), F32)
            for g in range(N_GROUPS):
                w_pair = jnp.where(low, wts[g][:, 2 * p:2 * p + 1], wts[g][:, 2 * p + 1:2 * p + 2])
                mixed = mixed + w_pair * _class_rows(o_refs[g], (p,), dils[g], r16, n16)
            pieces.append(mixed.astype(BF16))
        mix_ref[r16 * n16:(r16 + 1) * n16, :] = jnp.concatenate(pieces, axis=1)
    proj = jnp.dot(mix_ref[...], w_ref[...], preferred_element_type=F32)
    for c in range(n_chunks):
        proj_ref[c] = proj[:, c * LANES:(c + 1) * LANES]
    g = g_ref[...]
    b = b_ref[...]
    for n in range(n16):
        tok = slice(n * MAX_DILATION, (n + 1) * MAX_DILATION)
        y = jnp.concatenate([proj_ref[c, pl.ds(n, MAX_DILATION, stride=n16), :]
                             for c in range(n_chunks)], axis=1)
        y_ref[tok, :] = _layer_norm(DEEPNORM_ALPHA * x_ref[tok, :] + y, g, b)


def _attn_out(os_, lses, x2d, w_out_bf16, ln_g, ln_b, batch, seq):
    t = x2d.shape[0]
    tm = OUT_TM
    tiles = seq // tm
    pairs = ATTN_HEADS // 2
    dils = [d for _, d in DILATED_PATTERNS]
    o_spec = lambda d: pl.BlockSpec((None, d, pairs, tm // d, LANES), lambda b, i: (b, 0, 0, i, 0))
    l_spec = lambda d: pl.BlockSpec((None, d, tm // d, LANES), lambda b, i: (b, 0, i, 0))
    row = pl.BlockSpec((tm, D_MODEL), lambda b, i: (b * tiles + i, 0))
    return pl.pallas_call(
        _attn_out_kernel,
        out_shape=jax.ShapeDtypeStruct((t, D_MODEL), F32),
        grid=(batch, tiles),
        in_specs=([o_spec(d) for d in dils] + [l_spec(d) for d in dils]
                  + [row, _resident((D_MODEL, D_MODEL)), _resident((1, D_MODEL)),
                     _resident((1, D_MODEL))]),
        out_specs=row,
        scratch_shapes=[pltpu.VMEM((tm, D_MODEL), BF16),
                        pltpu.VMEM((D_MODEL // LANES, tm, LANES), F32)],
        compiler_params=pltpu.CompilerParams(
            dimension_semantics=("arbitrary", "arbitrary"), vmem_limit_bytes=VMEM_LIMIT),
        name="attn_out_ln",
    )(*os_, *lses, x2d, w_out_bf16, ln_g, ln_b)


def _ffn_kernel(x_ref, wu_ref, wd_ref, g_ref, b_ref, y_ref, h_ref):
    x = x_ref[...]
    xb = x.astype(BF16)
    for c in range(D_FF // D_MODEL):
        sl = slice(c * D_MODEL, (c + 1) * D_MODEL)
        h = jnp.dot(xb, wu_ref[:, sl], preferred_element_type=F32)
        h_ref[:, sl] = jnp.square(jnp.maximum(h, 0.0)).astype(BF16)
    y = jnp.dot(h_ref[...], wd_ref[...], preferred_element_type=F32)
    y_ref[...] = _layer_norm(DEEPNORM_ALPHA * x + y, g_ref[...], b_ref[...])


def _ffn(x2d, w_up_bf16, w_down_bf16, ln_g, ln_b):
    t = x2d.shape[0]
    tm = FFN_TM
    return pl.pallas_call(
        _ffn_kernel,
        out_shape=jax.ShapeDtypeStruct((t, D_MODEL), F32),
        grid=(t // tm,),
        in_specs=[pl.BlockSpec((tm, D_MODEL), lambda i: (i, 0)),
                  _resident((D_MODEL, D_FF)), _resident((D_FF, D_MODEL)),
                  _resident((1, D_MODEL)), _resident((1, D_MODEL))],
        out_specs=pl.BlockSpec((tm, D_MODEL), lambda i: (i, 0)),
        scratch_shapes=[pltpu.VMEM((tm, D_FF), BF16)],
        compiler_params=pltpu.CompilerParams(
            dimension_semantics=("arbitrary",), vmem_limit_bytes=VMEM_LIMIT),
        name="ffn_ln",
    )(x2d, w_up_bf16, w_down_bf16, ln_g, ln_b)


def _hgrn_kernel(layer, x_ref, wi_ref, wo_ref, lbl_ref, ng_ref, g_ref, b_ref, y_ref,
                 state_ref, on_ref):
    tm = x_ref.shape[0]
    hk = HGRN_HEADS * HGRN_DK
    c_len = HGRN_CHUNK

    @pl.when(pl.program_id(1) == 0)
    def _():
        state_ref[...] = jnp.zeros_like(state_ref)

    logits = lbl_ref[...]
    ex = jnp.exp(logits - jnp.max(logits, axis=0, keepdims=True))
    sm = ex / jnp.sum(ex, axis=0, keepdims=True)
    lb = jnp.sum(sm[1:layer + 1], axis=0, keepdims=True)

    x = x_ref[...]
    proj = jnp.dot(x.astype(BF16), wi_ref[...], preferred_element_type=F32)
    q_raw = proj[:, :hk]
    z = proj[:, hk:2 * hk]
    v = proj[:, 2 * hk:]

    key = (1.0 - lb) / (1.0 + jnp.exp(z))
    log_f = jnp.log(1.0 - key)
    q = q_raw / (1.0 + jnp.exp(-q_raw))

    ri = lax.broadcasted_iota(jnp.int32, (tm, tm), 0)
    ci = lax.broadcasted_iota(jnp.int32, (tm, tm), 1)
    tri = ((ri // c_len == ci // c_len) & (ci <= ri)).astype(BF16)
    hi = log_f.astype(BF16)
    r1 = log_f - hi.astype(F32)
    mid = r1.astype(BF16)
    lo = (r1 - mid.astype(F32)).astype(BF16)
    bcum = (jnp.dot(tri, hi, preferred_element_type=F32)
            + jnp.dot(tri, mid, preferred_element_type=F32)
            + jnp.dot(tri, lo, preferred_element_type=F32))

    causal = (lax.broadcasted_iota(jnp.int32, (c_len, c_len), 1)
              <= lax.broadcasted_iota(jnp.int32, (c_len, c_len), 0))
    ng = ng_ref[...]
    for c in range(tm // c_len):
        rows = slice(c * c_len, (c + 1) * c_len)
        b_c = bcum[rows]
        b_last = b_c[c_len - 1:c_len]
        q_dec = (q[rows] * jnp.exp(b_c)).astype(BF16)
        k_dec = (key[rows] * jnp.exp(-b_c)).astype(BF16)
        k_end = (key[rows] * jnp.exp(b_last - b_c)).astype(BF16)
        decay = jnp.exp(b_last)
        v_c = v[rows].astype(BF16)
        for h in range(HGRN_HEADS):
            ls = slice(h * HGRN_DK, (h + 1) * HGRN_DK)
            qd, kd, ke, vv = q_dec[:, ls], k_dec[:, ls], k_end[:, ls], v_c[:, ls]
            a = lax.dot_general(qd, kd, (((1,), (1,)), ((), ())), preferred_element_type=F32)
            a = jnp.where(causal, a, 0.0).astype(BF16)
            o = jnp.dot(a, vv, preferred_element_type=F32)
            st = state_ref[h]
            o = o + lax.dot_general(qd, st.astype(BF16), (((1,), (1,)), ((), ())),
                                    preferred_element_type=F32)
            kv_t = lax.dot_general(vv, ke, (((0,), (0,)), ((), ())), preferred_element_type=F32)
            state_ref[h] = decay[:, ls] * st + kv_t
            o = o * lax.rsqrt(jnp.mean(o * o, axis=-1, keepdims=True) + RMS_EPS) * ng[:, ls]
            on_ref[rows, ls] = o.astype(BF16)

    y = jnp.dot(on_ref[...], wo_ref[...], preferred_element_type=F32)
    y_ref[...] = _layer_norm(DEEPNORM_ALPHA * x + y, g_ref[...], b_ref[...])


def _hgrn_mixer(layer, x2d, w_in_bf16, w_out_bf16, lb_logits, norm_g, ln_g, ln_b, batch, seq):
    t = x2d.shape[0]
    tm = HGRN_TM
    tiles = seq // tm
    d_in = w_in_bf16.shape[1]
    row = pl.BlockSpec((tm, D_MODEL), lambda b, i: (b * tiles + i, 0))
    return pl.pallas_call(
        functools.partial(_hgrn_kernel, layer),
        out_shape=jax.ShapeDtypeStruct((t, D_MODEL), F32),
        grid=(batch, tiles),
        in_specs=[row, _resident((D_MODEL, d_in)), _resident((D_MODEL, D_MODEL)),
                  _resident((DEPTH, D_MODEL)), _resident((1, D_MODEL)),
                  _resident((1, D_MODEL)), _resident((1, D_MODEL))],
        out_specs=row,
        scratch_shapes=[pltpu.VMEM((HGRN_HEADS, HGRN_DV, HGRN_DK), F32),
                        pltpu.VMEM((tm, D_MODEL), BF16)],
        compiler_params=pltpu.CompilerParams(
            dimension_semantics=("arbitrary", "arbitrary"), vmem_limit_bytes=VMEM_LIMIT),
        name="hgrn2_mixer_ln",
    )(x2d, w_in_bf16, w_out_bf16, lb_logits, norm_g, ln_g, ln_b)


def kernel(x, attn_w_in, attn_w_out, hgrn_w_in, hgrn_w_out, hgrn_norm_g, lb_logits,
           ln_mix_g, ln_mix_b, ln_ffn_g, ln_ffn_b, ffn_w_up, ffn_w_down):
    batch, seq, d = x.shape
    assert d == D_MODEL and lb_logits.shape[0] == DEPTH
    for window, dilation in DILATED_PATTERNS:
        assert window // dilation == ATTN_BLK and seq % window == 0
        assert QKV_TM % (dilation * 16) == 0 and OUT_TM % (dilation * 8) == 0
    assert seq % QKV_TM == 0 and seq % OUT_TM == 0
    assert seq % HGRN_TM == 0 and HGRN_TM % HGRN_CHUNK == 0
    h = x.reshape(batch * seq, d)
    row = lambda a: a.reshape(1, -1)
    for i in range(DEPTH):
        j = i // 2
        if i % 2 == 0:
            os_, lses = [], []
            for g, (_, dil) in enumerate(DILATED_PATTERNS):
                qkv = _qkv_rope(h, _group_weights(attn_w_in[j], g), dil, batch, seq)
                o, lse = _attention_group(qkv, dil, batch, seq)
                os_.append(o)
                lses.append(lse)
            h = _attn_out(os_, lses, h, attn_w_out[j].astype(BF16),
                          row(ln_mix_g[i]), row(ln_mix_b[i]), batch, seq)
        else:
            h = _hgrn_mixer(i, h, hgrn_w_in[j].astype(BF16), hgrn_w_out[j].astype(BF16),
                            lb_logits, row(hgrn_norm_g[j]), row(ln_mix_g[i]), row(ln_mix_b[i]),
                            batch, seq)
        h = _ffn(h, ffn_w_up[i].astype(BF16), ffn_w_down[i].astype(BF16),
                 row(ln_ffn_g[i]), row(ln_ffn_b[i]))
    return h.reshape(batch, seq, d)
```

```python
import functools
import math

import jax
import jax.numpy as jnp
from jax import lax
from jax.experimental import pallas as pl
from jax.experimental.pallas import tpu as pltpu
import numpy as np

F32 = jnp.float32
BF16 = jnp.bfloat16

D_MODEL = 1024
DEPTH = 2
ATTN_HEAD_DIM = 64
ATTN_HEADS = D_MODEL // ATTN_HEAD_DIM
DILATED_PATTERNS = ((128, 1), (512, 4), (2048, 16))
N_GROUPS = len(DILATED_PATTERNS)
MAX_DILATION = max(d for _, d in DILATED_PATTERNS)
ROPE_THETA = 10000.0
HGRN_HEADS = 8
HGRN_DK = 128
HGRN_DV = 128
HGRN_CHUNK = 64
D_FF = 4 * D_MODEL
LN_EPS = 1e-5
RMS_EPS = 1e-6
DEEPNORM_ALPHA = (2 * DEPTH) ** 0.25

LANES = 128
ATTN_BLK = 128
HALF = ATTN_HEAD_DIM // 2
MASK_VALUE = -1e30
LN2 = math.log(2.0)
Q_SCALE = ATTN_HEAD_DIM ** -0.5 / LN2
VMEM_LIMIT = 56 * 1024 * 1024

QKV_TM = 1024
QKV_SUB = 256
ATTN_TQ = 512
OUT_TM = 512
FFN_TM = 512
HGRN_TM = 256


def _layer_norm(y, g, b):
    mu = jnp.mean(y, axis=-1, keepdims=True)
    d = y - mu
    var = jnp.mean(d * d, axis=-1, keepdims=True)
    return d * lax.rsqrt(var + LN_EPS) * g + b


def _resident(shape):
    nd = len(shape)
    return pl.BlockSpec(shape, lambda *_: (0,) * nd, pipeline_mode=pl.Buffered(1))


def _qkv_rope_kernel(dilation, x_ref, w_ref, tab_ref, o_ref, xb_ref, xs_ref):
    tm = x_ref.shape[0]
    n_per = tm // dilation
    n_chunks = D_MODEL // LANES

    if dilation == 1:
        xb_ref[...] = x_ref[...].astype(BF16)
    else:
        for c in range(n_chunks):
            xs_ref[c] = x_ref[:, c * LANES:(c + 1) * LANES]
        for r in range(dilation):
            xb_ref[r * n_per:(r + 1) * n_per, :] = jnp.concatenate(
                [xs_ref[c, pl.ds(r, n_per, stride=dilation), :].astype(BF16)
                 for c in range(n_chunks)], axis=1)

    def store(kind, s, val):
        if n_per >= QKV_SUB:
            start = s * QKV_SUB
            o_ref[kind, start // n_per, start % n_per:start % n_per + QKV_SUB, :] = val
        else:
            per = QKV_SUB // n_per
            for c in range(per):
                o_ref[kind, s * per + c] = val[c * n_per:(c + 1) * n_per]

    for kind in range(3):
        cols = slice(kind * D_MODEL, (kind + 1) * D_MODEL)
        for s in range(tm // QKV_SUB):
            rows = slice(s * QKV_SUB, (s + 1) * QKV_SUB)
            acc = jnp.dot(xb_ref[rows], w_ref[:, cols], preferred_element_type=F32)
            if kind == 2:
                store(kind, s, acc.astype(BF16))
                continue
            cos = tab_ref[0, rows, :]
            sin = tab_ref[1, rows, :]
            if kind == 0:
                cos = cos * Q_SCALE
                sin = sin * Q_SCALE
            pieces = []
            for c in range(n_chunks):
                a = acc[:, c * LANES:(c + 1) * LANES]
                pieces.append((a * cos + pltpu.roll(a, LANES // 2, 1) * sin).astype(BF16))
            store(kind, s, jnp.concatenate(pieces, axis=1))


def _rope_table(seq, dilation, tm):
    inv = ROPE_THETA ** (-np.arange(HALF, dtype=np.float64) * (2.0 / ATTN_HEAD_DIM))
    ang = np.arange(seq, dtype=np.float64)[:, None] * inv[None, :]
    cos = np.tile(np.cos(ang), (1, LANES // HALF))
    sin = np.tile(np.sin(ang), (1, LANES // HALF))
    sign = np.where(np.arange(LANES) < LANES // 2, -1.0, 1.0)
    tab = np.stack([cos, sin * sign])
    tab = tab.reshape(2, seq // tm, tm // dilation, dilation, LANES)
    tab = tab.transpose(0, 1, 3, 2, 4).reshape(2, seq, LANES)
    return jnp.asarray(tab.astype(np.float32))


def _prepare_attn_w_in(w_in):
    k = w_in.shape[0]
    w = w_in.reshape(k, N_GROUPS, 3, ATTN_HEADS // 2, 2, 2, HALF)
    qk = w[:, :, :2].transpose(0, 1, 2, 3, 5, 4, 6)
    w = jnp.concatenate([qk, w[:, :, 2:]], axis=2)
    return w.reshape(k, N_GROUPS * 3 * D_MODEL).astype(BF16)


def _qkv_rope(x2d, w_all, g, dilation, batch, seq):
    tm = QKV_TM
    tiles = seq // tm
    n_per = tm // dilation
    tab = _rope_table(seq, dilation, tm)
    return pl.pallas_call(
        functools.partial(_qkv_rope_kernel, dilation),
        out_shape=jax.ShapeDtypeStruct((3, batch, dilation, seq // dilation, D_MODEL), BF16),
        grid=(batch, tiles),
        in_specs=[
            pl.BlockSpec((tm, D_MODEL), lambda b, i: (b * tiles + i, 0)),
            pl.BlockSpec((D_MODEL, 3 * D_MODEL), lambda b, i: (0, g), pipeline_mode=pl.Buffered(1)),
            pl.BlockSpec((2, tm, LANES), lambda b, i: (0, i, 0)),
        ],
        out_specs=pl.BlockSpec((3, None, dilation, n_per, D_MODEL), lambda b, i: (0, b, 0, i, 0)),
        scratch_shapes=[pltpu.VMEM((tm, D_MODEL), BF16),
                        pltpu.VMEM((D_MODEL // LANES, tm, LANES), F32)],
        compiler_params=pltpu.CompilerParams(
            dimension_semantics=("arbitrary", "arbitrary"), vmem_limit_bytes=VMEM_LIMIT),
        name=f"qkv_rope_d{dilation}",
    )(x2d, w_all, tab)


def _attn_kernel(q_ref, kp_ref, kc_ref, vp_ref, vc_ref, o_ref, lse_ref):
    i = pl.program_id(2)
    blk = ATTN_BLK
    tq = q_ref.shape[0]
    row = lax.broadcasted_iota(jnp.int32, (2 * blk, 2 * blk), 0) % blk
    col = lax.broadcasted_iota(jnp.int32, (2 * blk, 2 * blk), 1)
    valid = (col >= row) & (col <= row + blk)
    bias = jnp.where(valid, 0.0, MASK_VALUE).astype(F32)
    bias_first = jnp.where(valid & ((col >= blk) | (i > 0)), 0.0, MASK_VALUE).astype(F32)
    lane = lax.broadcasted_iota(jnp.int32, (blk, LANES), 1)
    qk_head0 = ((lane // HALF) % 2 == 0).astype(F32).astype(BF16)
    qk_head1 = ((lane // HALF) % 2 == 1).astype(F32).astype(BF16)
    v_head0 = lane < ATTN_HEAD_DIM
    ones = jnp.ones((2 * blk, LANES), BF16)
    for qb in range(tq // blk):
        rows = slice(qb * blk, (qb + 1) * blk)
        prev_rows = slice((qb - 1) * blk, qb * blk)
        m_tile = jnp.zeros((blk, LANES), F32)
        l_tile = jnp.ones((blk, LANES), F32)
        for p in range(ATTN_HEADS // 2):
            sl = slice(p * LANES, (p + 1) * LANES)
            q = q_ref[rows, sl]
            qs = jnp.concatenate([q * qk_head0, q * qk_head1], axis=0)
            k_prev = kp_ref[:, sl] if qb == 0 else kc_ref[prev_rows, sl]
            v_prev = vp_ref[:, sl] if qb == 0 else vc_ref[prev_rows, sl]
            k = jnp.concatenate([k_prev, kc_ref[rows, sl]], axis=0)
            v = jnp.concatenate([v_prev, vc_ref[rows, sl]], axis=0)
            s = lax.dot_general(qs, k, (((1,), (1,)), ((), ())), preferred_element_type=F32)
            s = s + (bias_first if qb == 0 else bias)
            m = jnp.max(s, axis=-1, keepdims=True)
            e = jnp.exp2(s - m).astype(BF16)
            pv = jnp.dot(e, jnp.concatenate([v, ones], axis=1),
                         preferred_element_type=F32)
            l_rep = pv[:, LANES:]
            o2 = pv[:, :LANES] * (1.0 / l_rep)
            o_ref[p, rows, :] = jnp.where(v_head0, o2[:blk], o2[blk:])
            m_tile = jnp.where(lane == 2 * p, m[:blk], m_tile)
            m_tile = jnp.where(lane == 2 * p + 1, m[blk:], m_tile)
            l_tile = jnp.where(lane == 2 * p, l_rep[:blk], l_tile)
            l_tile = jnp.where(lane == 2 * p + 1, l_rep[blk:], l_tile)
        lse_ref[rows, :] = m_tile * LN2 + jnp.log(l_tile)


def _attention_group(qkv, dilation, batch, seq):
    n = seq // dilation
    tq = min(ATTN_TQ, n)
    per = tq // ATTN_BLK
    pairs = ATTN_HEADS // 2

    def cur(which):
        return pl.BlockSpec((None, None, None, tq, D_MODEL), lambda b, r, i: (which, b, r, i, 0))

    def prev(which):
        return pl.BlockSpec((None, None, None, ATTN_BLK, D_MODEL),
                            lambda b, r, i: (which, b, r, jnp.maximum(i * per - 1, 0), 0))

    return pl.pallas_call(
        _attn_kernel,
        out_shape=(jax.ShapeDtypeStruct((batch, dilation, pairs, n, LANES), F32),
                   jax.ShapeDtypeStruct((batch, dilation, n, LANES), F32)),
        grid=(batch, dilation, n // tq),
        in_specs=[cur(0), prev(1), cur(1), prev(2), cur(2)],
        out_specs=(pl.BlockSpec((None, None, pairs, tq, LANES), lambda b, r, i: (b, r, 0, i, 0)),
                   pl.BlockSpec((None, None, tq, LANES), lambda b, r, i: (b, r, i, 0))),
        compiler_params=pltpu.CompilerParams(
            dimension_semantics=("arbitrary", "arbitrary", "arbitrary"),
            vmem_limit_bytes=VMEM_LIMIT),
        name=f"dilated_attn_d{dilation}",
    )(qkv, qkv, qkv, qkv, qkv)


def _class_rows(ref, lead, dilation, r16, n16):
    step = MAX_DILATION // dilation
    if step == 1:
        return ref[(r16, *lead)]
    return ref[(r16 % dilation, *lead, pl.ds(r16 // dilation, n16, stride=step), slice(None))]


def _attn_out_kernel(o0_ref, o1_ref, o2_ref, l0_ref, l1_ref, l2_ref, x_ref, w_ref, ex_ref,
                     g_ref, b_ref, y_ref, proj_ref):
    tm = x_ref.shape[0]
    n16 = tm // MAX_DILATION
    n_chunks = D_MODEL // LANES
    o_refs = (o0_ref, o1_ref, o2_ref)
    l_refs = (l0_ref, l1_ref, l2_ref)
    dils = [d for _, d in DILATED_PATTERNS]
    classes = range(MAX_DILATION)

    lses = [jnp.concatenate([_class_rows(l_refs[g], (), dils[g], r, n16) for r in classes], axis=0)
            for g in range(N_GROUPS)]
    mx = jnp.maximum(jnp.maximum(lses[0], lses[1]), lses[2])
    es = [jnp.exp(v - mx) for v in lses]
    inv = 1.0 / (es[0] + es[1] + es[2])
    halves = []
    for e in es:
        w = e * inv
        hi = w.astype(BF16)
        halves.append(jnp.concatenate([hi, (w - hi.astype(F32)).astype(BF16)], axis=1))

    pieces = []
    for pp in range(ATTN_HEADS // 4):
        cols = slice(2 * pp * LANES, (2 * pp + 2) * LANES)
        mixed = jnp.zeros((tm, 2 * LANES), F32)
        for g in range(N_GROUPS):
            w_wide = jnp.dot(halves[g], ex_ref[:, cols], preferred_element_type=F32)
            o_g = jnp.concatenate(
                [jnp.concatenate([_class_rows(o_refs[g], (p,), dils[g], r, n16) for r in classes],
                                 axis=0) for p in (2 * pp, 2 * pp + 1)], axis=1)
            mixed = mixed + w_wide * o_g
        pieces.append(mixed.astype(BF16))
    proj = jnp.dot(jnp.concatenate(pieces, axis=1), w_ref[...], preferred_element_type=F32)
    for c in range(n_chunks):
        proj_ref[c] = proj[:, c * LANES:(c + 1) * LANES]
    g = g_ref[...]
    b = b_ref[...]
    for n in range(n16):
        tok = slice(n * MAX_DILATION, (n + 1) * MAX_DILATION)
        y = jnp.concatenate([proj_ref[c, pl.ds(n, MAX_DILATION, stride=n16), :]
                             for c in range(n_chunks)], axis=1)
        y_ref[tok, :] = _layer_norm(DEEPNORM_ALPHA * x_ref[tok, :] + y, g, b)


def _head_expansion():
    e = (np.arange(D_MODEL)[None, :] // ATTN_HEAD_DIM == np.arange(LANES)[:, None])
    return jnp.asarray(np.concatenate([e, e], axis=0).astype(np.float32), dtype=BF16)


def _attn_out(os_, lses, x2d, w_out_bf16, ln_g, ln_b, batch, seq):
    t = x2d.shape[0]
    tm = OUT_TM
    tiles = seq // tm
    pairs = ATTN_HEADS // 2
    dils = [d for _, d in DILATED_PATTERNS]
    o_spec = lambda d: pl.BlockSpec((None, d, pairs, tm // d, LANES), lambda b, i: (b, 0, 0, i, 0))
    l_spec = lambda d: pl.BlockSpec((None, d, tm // d, LANES), lambda b, i: (b, 0, i, 0))
    row = pl.BlockSpec((tm, D_MODEL), lambda b, i: (b * tiles + i, 0))
    return pl.pallas_call(
        _attn_out_kernel,
        out_shape=jax.ShapeDtypeStruct((t, D_MODEL), F32),
        grid=(batch, tiles),
        in_specs=([o_spec(d) for d in dils] + [l_spec(d) for d in dils]
                  + [row, _resident((D_MODEL, D_MODEL)), _resident((2 * LANES, D_MODEL)),
                     _resident((1, D_MODEL)), _resident((1, D_MODEL))]),
        out_specs=row,
        scratch_shapes=[pltpu.VMEM((D_MODEL // LANES, tm, LANES), F32)],
        compiler_params=pltpu.CompilerParams(
            dimension_semantics=("arbitrary", "arbitrary"), vmem_limit_bytes=VMEM_LIMIT),
        name="attn_out_ln",
    )(*os_, *lses, x2d, w_out_bf16, _head_expansion(), ln_g, ln_b)


def _ffn_kernel(x_ref, wu_ref, wd_ref, g_ref, b_ref, y_ref, h_ref):
    x = x_ref[...]
    xb = x.astype(BF16)
    for c in range(D_FF // D_MODEL):
        sl = slice(c * D_MODEL, (c + 1) * D_MODEL)
        h = jnp.dot(xb, wu_ref[:, sl], preferred_element_type=F32)
        h_ref[:, sl] = jnp.square(jnp.maximum(h, 0.0)).astype(BF16)
    y = jnp.dot(h_ref[...], wd_ref[...], preferred_element_type=F32)
    y_ref[...] = _layer_norm(DEEPNORM_ALPHA * x + y, g_ref[...], b_ref[...])


def _ffn(x2d, w_up_bf16, w_down_bf16, ln_g, ln_b):
    t = x2d.shape[0]
    tm = FFN_TM
    return pl.pallas_call(
        _ffn_kernel,
        out_shape=jax.ShapeDtypeStruct((t, D_MODEL), F32),
        grid=(t // tm,),
        in_specs=[pl.BlockSpec((tm, D_MODEL), lambda i: (i, 0)),
                  _resident((D_MODEL, D_FF)), _resident((D_FF, D_MODEL)),
                  _resident((1, D_MODEL)), _resident((1, D_MODEL))],
        out_specs=pl.BlockSpec((tm, D_MODEL), lambda i: (i, 0)),
        scratch_shapes=[pltpu.VMEM((tm, D_FF), BF16)],
        compiler_params=pltpu.CompilerParams(
            dimension_semantics=("arbitrary",), vmem_limit_bytes=VMEM_LIMIT),
        name="ffn_ln",
    )(x2d, w_up_bf16, w_down_bf16, ln_g, ln_b)


def _hgrn_kernel(layer, x_ref, wi_ref, wo_ref, lbl_ref, ng_ref, g_ref, b_ref, y_ref,
                 state_ref, on_ref):
    tm = x_ref.shape[0]
    hk = HGRN_HEADS * HGRN_DK
    c_len = HGRN_CHUNK

    @pl.when(pl.program_id(1) == 0)
    def _():
        state_ref[...] = jnp.zeros_like(state_ref)

    logits = lbl_ref[...]
    ex = jnp.exp(logits - jnp.max(logits, axis=0, keepdims=True))
    sm = ex / jnp.sum(ex, axis=0, keepdims=True)
    lb = jnp.sum(sm[1:layer + 1], axis=0, keepdims=True)

    x = x_ref[...]
    proj = jnp.dot(x.astype(BF16), wi_ref[...], preferred_element_type=F32)
    q_raw = proj[:, :hk]
    z = proj[:, hk:2 * hk]
    v = proj[:, 2 * hk:]

    key = (1.0 - lb) / (1.0 + jnp.exp(z))
    log_f = jnp.log(1.0 - key)
    q = q_raw / (1.0 + jnp.exp(-q_raw))

    ri = lax.broadcasted_iota(jnp.int32, (tm, tm), 0)
    ci = lax.broadcasted_iota(jnp.int32, (tm, tm), 1)
    tri = ((ri // c_len == ci // c_len) & (ci <= ri)).astype(BF16)
    hi = log_f.astype(BF16)
    r1 = log_f - hi.astype(F32)
    mid = r1.astype(BF16)
    lo = (r1 - mid.astype(F32)).astype(BF16)
    bcum = (jnp.dot(tri, hi, preferred_element_type=F32)
            + jnp.dot(tri, mid, preferred_element_type=F32)
            + jnp.dot(tri, lo, preferred_element_type=F32))

    causal = (lax.broadcasted_iota(jnp.int32, (c_len, c_len), 1)
              <= lax.broadcasted_iota(jnp.int32, (c_len, c_len), 0))
    ng = ng_ref[...]
    for c in range(tm // c_len):
        rows = slice(c * c_len, (c + 1) * c_len)
        b_c = bcum[rows]
        b_last = b_c[c_len - 1:c_len]
        q_dec = (q[rows] * jnp.exp(b_c)).astype(BF16)
        k_dec = (key[rows] * jnp.exp(-b_c)).astype(BF16)
        k_end = (key[rows] * jnp.exp(b_last - b_c)).astype(BF16)
        decay = jnp.exp(b_last)
        v_c = v[rows].astype(BF16)
        for h in range(HGRN_HEADS):
            ls = slice(h * HGRN_DK, (h + 1) * HGRN_DK)
            qd, kd, ke, vv = q_dec[:, ls], k_dec[:, ls], k_end[:, ls], v_c[:, ls]
            a = lax.dot_general(qd, kd, (((1,), (1,)), ((), ())), preferred_element_type=F32)
            a = jnp.where(causal, a, 0.0).astype(BF16)
            o = jnp.dot(a, vv, preferred_element_type=F32)
            st = state_ref[h]
            o = o + lax.dot_general(qd, st.astype(BF16), (((1,), (1,)), ((), ())),
                                    preferred_element_type=F32)
            kv_t = lax.dot_general(vv, ke, (((0,), (0,)), ((), ())), preferred_element_type=F32)
            state_ref[h] = decay[:, ls] * st + kv_t
            o = o * lax.rsqrt(jnp.mean(o * o, axis=-1, keepdims=True) + RMS_EPS) * ng[:, ls]
            on_ref[rows, ls] = o.astype(BF16)

    y = jnp.dot(on_ref[...], wo_ref[...], preferred_element_type=F32)
    y_ref[...] = _layer_norm(DEEPNORM_ALPHA * x + y, g_ref[...], b_ref[...])


def _hgrn_mixer(layer, x2d, w_in_bf16, w_out_bf16, lb_logits, norm_g, ln_g, ln_b, batch, seq):
    t = x2d.shape[0]
    tm = HGRN_TM
    tiles = seq // tm
    d_in = w_in_bf16.shape[1]
    row = pl.BlockSpec((tm, D_MODEL), lambda b, i: (b * tiles + i, 0))
    return pl.pallas_call(
        functools.partial(_hgrn_kernel, layer),
        out_shape=jax.ShapeDtypeStruct((t, D_MODEL), F32),
        grid=(batch, tiles),
        in_specs=[row, _resident((D_MODEL, d_in)), _resident((D_MODEL, D_MODEL)),
                  _resident((DEPTH, D_MODEL)), _resident((1, D_MODEL)),
                  _resident((1, D_MODEL)), _resident((1, D_MODEL))],
        out_specs=row,
        scratch_shapes=[pltpu.VMEM((HGRN_HEADS, HGRN_DV, HGRN_DK), F32),
                        pltpu.VMEM((tm, D_MODEL), BF16)],
        compiler_params=pltpu.CompilerParams(
            dimension_semantics=("arbitrary", "arbitrary"), vmem_limit_bytes=VMEM_LIMIT),
        name="hgrn2_mixer_ln",
    )(x2d, w_in_bf16, w_out_bf16, lb_logits, norm_g, ln_g, ln_b)


def kernel(x, attn_w_in, attn_w_out, hgrn_w_in, hgrn_w_out, hgrn_norm_g, lb_logits,
           ln_mix_g, ln_mix_b, ln_ffn_g, ln_ffn_b, ffn_w_up, ffn_w_down):
    batch, seq, d = x.shape
    assert d == D_MODEL and lb_logits.shape[0] == DEPTH
    for window, dilation in DILATED_PATTERNS:
        assert window // dilation == ATTN_BLK and seq % window == 0
        assert QKV_TM % (dilation * 16) == 0 and OUT_TM % (dilation * 8) == 0
    assert seq % QKV_TM == 0 and seq % OUT_TM == 0
    assert seq % HGRN_TM == 0 and HGRN_TM % HGRN_CHUNK == 0
    h = x.reshape(batch * seq, d)
    row = lambda a: a.reshape(1, -1)
    for i in range(DEPTH):
        j = i // 2
        if i % 2 == 0:
            w_all = _prepare_attn_w_in(attn_w_in[j])
            os_, lses = [], []
            for g, (_, dil) in enumerate(DILATED_PATTERNS):
                qkv = _qkv_rope(h, w_all, g, dil, batch, seq)
                o, lse = _attention_group(qkv, dil, batch, seq)
                os_.append(o)
                lses.append(lse)
            h = _attn_out(os_, lses, h, attn_w_out[j].astype(BF16),
                          row(ln_mix_g[i]), row(ln_mix_b[i]), batch, seq)
        else:
            h = _hgrn_mixer(i, h, hgrn_w_in[j].astype(BF16), hgrn_w_out[j].astype(BF16),
                            lb_logits, row(hgrn_norm_g[j]), row(ln_mix_g[i]), row(ln_mix_b[i]),
                            batch, seq)
        h = _ffn(h, ffn_w_up[i].astype(BF16), ffn_w_down[i].astype(BF16),
                 row(ln_ffn_g[i]), row(ln_ffn_b[i]))
    return h.reshape(batch, seq, d)
```

```python
import functools
import math

import jax
import jax.numpy as jnp
from jax import lax
from jax.experimental import pallas as pl
from jax.experimental.pallas import tpu as pltpu
import numpy as np

F32 = jnp.float32
BF16 = jnp.bfloat16

D_MODEL = 1024
DEPTH = 2
ATTN_HEAD_DIM = 64
ATTN_HEADS = D_MODEL // ATTN_HEAD_DIM
DILATED_PATTERNS = ((128, 1), (512, 4), (2048, 16))
N_GROUPS = len(DILATED_PATTERNS)
MAX_DILATION = max(d for _, d in DILATED_PATTERNS)
ROPE_THETA = 10000.0
HGRN_HEADS = 8
HGRN_DK = 128
HGRN_DV = 128
HGRN_CHUNK = 64
D_FF = 4 * D_MODEL
LN_EPS = 1e-5
RMS_EPS = 1e-6
DEEPNORM_ALPHA = (2 * DEPTH) ** 0.25

LANES = 128
ATTN_BLK = 128
HALF = ATTN_HEAD_DIM // 2
MASK_VALUE = -1e30
LN2 = math.log(2.0)
Q_SCALE = ATTN_HEAD_DIM ** -0.5 / LN2
VMEM_LIMIT = 56 * 1024 * 1024

QKV_TM = 1024
QKV_SUB = 256
ATTN_TQ = 512
OUT_TM = 512
FFN_TM = 512
HGRN_TM = 512
HGRN_SUB = 256
HGRN_STAGGER = 1


def _layer_norm(y, g, b):
    mu = jnp.mean(y, axis=-1, keepdims=True)
    d = y - mu
    var = jnp.mean(d * d, axis=-1, keepdims=True)
    return d * lax.rsqrt(var + LN_EPS) * g + b


def _resident(shape):
    nd = len(shape)
    return pl.BlockSpec(shape, lambda *_: (0,) * nd, pipeline_mode=pl.Buffered(1))


def _qkv_rope_kernel(dilation, x_ref, w_ref, tab_ref, o_ref, xb_ref, xs_ref):
    tm = x_ref.shape[0]
    n_per = tm // dilation
    n_chunks = D_MODEL // LANES

    if dilation == 1:
        xb_ref[...] = x_ref[...].astype(BF16)
    else:
        for c in range(n_chunks):
            xs_ref[c] = x_ref[:, c * LANES:(c + 1) * LANES]
        for r in range(dilation):
            xb_ref[r * n_per:(r + 1) * n_per, :] = jnp.concatenate(
                [xs_ref[c, pl.ds(r, n_per, stride=dilation), :].astype(BF16)
                 for c in range(n_chunks)], axis=1)

    def store(kind, s, val):
        if n_per >= QKV_SUB:
            start = s * QKV_SUB
            o_ref[kind, start // n_per, start % n_per:start % n_per + QKV_SUB, :] = val
        else:
            per = QKV_SUB // n_per
            for c in range(per):
                o_ref[kind, s * per + c] = val[c * n_per:(c + 1) * n_per]

    for kind in range(3):
        cols = slice(kind * D_MODEL, (kind + 1) * D_MODEL)
        for s in range(tm // QKV_SUB):
            rows = slice(s * QKV_SUB, (s + 1) * QKV_SUB)
            acc = jnp.dot(xb_ref[rows], w_ref[:, cols], preferred_element_type=F32)
            if kind == 2:
                store(kind, s, acc.astype(BF16))
                continue
            cos = tab_ref[0, rows, :]
            sin = tab_ref[1, rows, :]
            if kind == 0:
                cos = cos * Q_SCALE
                sin = sin * Q_SCALE
            pieces = []
            for c in range(n_chunks):
                a = acc[:, c * LANES:(c + 1) * LANES]
                pieces.append((a * cos + pltpu.roll(a, LANES // 2, 1) * sin).astype(BF16))
            store(kind, s, jnp.concatenate(pieces, axis=1))


def _rope_table(seq, dilation, tm):
    inv = ROPE_THETA ** (-np.arange(HALF, dtype=np.float64) * (2.0 / ATTN_HEAD_DIM))
    ang = np.arange(seq, dtype=np.float64)[:, None] * inv[None, :]
    cos = np.tile(np.cos(ang), (1, LANES // HALF))
    sin = np.tile(np.sin(ang), (1, LANES // HALF))
    sign = np.where(np.arange(LANES) < LANES // 2, -1.0, 1.0)
    tab = np.stack([cos, sin * sign])
    tab = tab.reshape(2, seq // tm, tm // dilation, dilation, LANES)
    tab = tab.transpose(0, 1, 3, 2, 4).reshape(2, seq, LANES)
    return jnp.asarray(tab.astype(np.float32))


def _prepare_attn_w_in(w_in):
    k = w_in.shape[0]
    w = w_in.reshape(k, N_GROUPS, 3, ATTN_HEADS // 2, 2, 2, HALF)
    qk = w[:, :, :2].transpose(0, 1, 2, 3, 5, 4, 6)
    w = jnp.concatenate([qk, w[:, :, 2:]], axis=2)
    return w.reshape(k, N_GROUPS * 3 * D_MODEL).astype(BF16)


def _qkv_rope(x2d, w_all, g, dilation, batch, seq):
    tm = QKV_TM
    tiles = seq // tm
    n_per = tm // dilation
    tab = _rope_table(seq, dilation, tm)
    return pl.pallas_call(
        functools.partial(_qkv_rope_kernel, dilation),
        out_shape=jax.ShapeDtypeStruct((3, batch, dilation, seq // dilation, D_MODEL), BF16),
        grid=(batch, tiles),
        in_specs=[
            pl.BlockSpec((tm, D_MODEL), lambda b, i: (b * tiles + i, 0)),
            pl.BlockSpec((D_MODEL, 3 * D_MODEL), lambda b, i: (0, g), pipeline_mode=pl.Buffered(1)),
            pl.BlockSpec((2, tm, LANES), lambda b, i: (0, i, 0)),
        ],
        out_specs=pl.BlockSpec((3, None, dilation, n_per, D_MODEL), lambda b, i: (0, b, 0, i, 0)),
        scratch_shapes=[pltpu.VMEM((tm, D_MODEL), BF16),
                        pltpu.VMEM((D_MODEL // LANES, tm, LANES), F32)],
        compiler_params=pltpu.CompilerParams(
            dimension_semantics=("arbitrary", "arbitrary"), vmem_limit_bytes=VMEM_LIMIT),
        name=f"qkv_rope_d{dilation}",
    )(x2d, w_all, tab)


def _attn_kernel(q_ref, kp_ref, kc_ref, vp_ref, vc_ref, o_ref, lse_ref):
    i = pl.program_id(2)
    blk = ATTN_BLK
    tq = q_ref.shape[0]
    row = lax.broadcasted_iota(jnp.int32, (2 * blk, 2 * blk), 0) % blk
    col = lax.broadcasted_iota(jnp.int32, (2 * blk, 2 * blk), 1)
    valid = (col >= row) & (col <= row + blk)
    bias = jnp.where(valid, 0.0, MASK_VALUE).astype(F32)
    bias_first = jnp.where(valid & ((col >= blk) | (i > 0)), 0.0, MASK_VALUE).astype(F32)
    lane = lax.broadcasted_iota(jnp.int32, (blk, LANES), 1)
    qk_head0 = ((lane // HALF) % 2 == 0).astype(F32).astype(BF16)
    qk_head1 = ((lane // HALF) % 2 == 1).astype(F32).astype(BF16)
    v_head0 = lane < ATTN_HEAD_DIM
    ones = jnp.ones((2 * blk, LANES), BF16)
    for qb in range(tq // blk):
        rows = slice(qb * blk, (qb + 1) * blk)
        prev_rows = slice((qb - 1) * blk, qb * blk)
        m_tile = jnp.zeros((blk, LANES), F32)
        l_tile = jnp.ones((blk, LANES), F32)
        for p in range(ATTN_HEADS // 2):
            sl = slice(p * LANES, (p + 1) * LANES)
            q = q_ref[rows, sl]
            qs = jnp.concatenate([q * qk_head0, q * qk_head1], axis=0)
            k_prev = kp_ref[:, sl] if qb == 0 else kc_ref[prev_rows, sl]
            v_prev = vp_ref[:, sl] if qb == 0 else vc_ref[prev_rows, sl]
            k = jnp.concatenate([k_prev, kc_ref[rows, sl]], axis=0)
            v = jnp.concatenate([v_prev, vc_ref[rows, sl]], axis=0)
            s = lax.dot_general(qs, k, (((1,), (1,)), ((), ())), preferred_element_type=F32)
            s = s + (bias_first if qb == 0 else bias)
            m = jnp.max(s, axis=-1, keepdims=True)
            e = jnp.exp2(s - m).astype(BF16)
            pv = jnp.dot(e, jnp.concatenate([v, ones], axis=1),
                         preferred_element_type=F32)
            l_rep = pv[:, LANES:]
            o2 = pv[:, :LANES] * (1.0 / l_rep)
            o_ref[p, rows, :] = jnp.where(v_head0, o2[:blk], o2[blk:])
            m_tile = jnp.where(lane == 2 * p, m[:blk], m_tile)
            m_tile = jnp.where(lane == 2 * p + 1, m[blk:], m_tile)
            l_tile = jnp.where(lane == 2 * p, l_rep[:blk], l_tile)
            l_tile = jnp.where(lane == 2 * p + 1, l_rep[blk:], l_tile)
        lse_ref[rows, :] = m_tile * LN2 + jnp.log(l_tile)


def _attention_group(qkv, dilation, batch, seq):
    n = seq // dilation
    tq = min(ATTN_TQ, n)
    per = tq // ATTN_BLK
    pairs = ATTN_HEADS // 2

    def cur(which):
        return pl.BlockSpec((None, None, None, tq, D_MODEL), lambda b, r, i: (which, b, r, i, 0))

    def prev(which):
        return pl.BlockSpec((None, None, None, ATTN_BLK, D_MODEL),
                            lambda b, r, i: (which, b, r, jnp.maximum(i * per - 1, 0), 0))

    return pl.pallas_call(
        _attn_kernel,
        out_shape=(jax.ShapeDtypeStruct((batch, dilation, pairs, n, LANES), F32),
                   jax.ShapeDtypeStruct((batch, dilation, n, LANES), F32)),
        grid=(batch, dilation, n // tq),
        in_specs=[cur(0), prev(1), cur(1), prev(2), cur(2)],
        out_specs=(pl.BlockSpec((None, None, pairs, tq, LANES), lambda b, r, i: (b, r, 0, i, 0)),
                   pl.BlockSpec((None, None, tq, LANES), lambda b, r, i: (b, r, i, 0))),
        compiler_params=pltpu.CompilerParams(
            dimension_semantics=("arbitrary", "arbitrary", "arbitrary"),
            vmem_limit_bytes=VMEM_LIMIT),
        name=f"dilated_attn_d{dilation}",
    )(qkv, qkv, qkv, qkv, qkv)


def _class_rows(ref, lead, dilation, r16, n16):
    step = MAX_DILATION // dilation
    if step == 1:
        return ref[(r16, *lead)]
    return ref[(r16 % dilation, *lead, pl.ds(r16 // dilation, n16, stride=step), slice(None))]


def _attn_out_kernel(o0_ref, o1_ref, o2_ref, l0_ref, l1_ref, l2_ref, x_ref, w_ref, ex_ref,
                     g_ref, b_ref, y_ref, proj_ref):
    tm = x_ref.shape[0]
    n16 = tm // MAX_DILATION
    n_chunks = D_MODEL // LANES
    o_refs = (o0_ref, o1_ref, o2_ref)
    l_refs = (l0_ref, l1_ref, l2_ref)
    dils = [d for _, d in DILATED_PATTERNS]
    classes = range(MAX_DILATION)

    lses = [jnp.concatenate([_class_rows(l_refs[g], (), dils[g], r, n16) for r in classes], axis=0)
            for g in range(N_GROUPS)]
    mx = jnp.maximum(jnp.maximum(lses[0], lses[1]), lses[2])
    es = [jnp.exp(v - mx) for v in lses]
    inv = 1.0 / (es[0] + es[1] + es[2])
    halves = []
    for e in es:
        w = e * inv
        hi = w.astype(BF16)
        halves.append(jnp.concatenate([hi, (w - hi.astype(F32)).astype(BF16)], axis=1))

    pieces = []
    for pp in range(ATTN_HEADS // 4):
        cols = slice(2 * pp * LANES, (2 * pp + 2) * LANES)
        mixed = jnp.zeros((tm, 2 * LANES), F32)
        for g in range(N_GROUPS):
            w_wide = jnp.dot(halves[g], ex_ref[:, cols], preferred_element_type=F32)
            o_g = jnp.concatenate(
                [jnp.concatenate([_class_rows(o_refs[g], (p,), dils[g], r, n16) for r in classes],
                                 axis=0) for p in (2 * pp, 2 * pp + 1)], axis=1)
            mixed = mixed + w_wide * o_g
        pieces.append(mixed.astype(BF16))
    proj = jnp.dot(jnp.concatenate(pieces, axis=1), w_ref[...], preferred_element_type=F32)
    for c in range(n_chunks):
        proj_ref[c] = proj[:, c * LANES:(c + 1) * LANES]
    g = g_ref[...]
    b = b_ref[...]
    for n in range(n16):
        tok = slice(n * MAX_DILATION, (n + 1) * MAX_DILATION)
        y = jnp.concatenate([proj_ref[c, pl.ds(n, MAX_DILATION, stride=n16), :]
                             for c in range(n_chunks)], axis=1)
        y_ref[tok, :] = _layer_norm(DEEPNORM_ALPHA * x_ref[tok, :] + y, g, b)


def _head_expansion():
    e = (np.arange(D_MODEL)[None, :] // ATTN_HEAD_DIM == np.arange(LANES)[:, None])
    return jnp.asarray(np.concatenate([e, e], axis=0).astype(np.float32), dtype=BF16)


def _attn_out(os_, lses, x2d, w_out_bf16, ln_g, ln_b, batch, seq):
    t = x2d.shape[0]
    tm = OUT_TM
    tiles = seq // tm
    pairs = ATTN_HEADS // 2
    dils = [d for _, d in DILATED_PATTERNS]
    o_spec = lambda d: pl.BlockSpec((None, d, pairs, tm // d, LANES), lambda b, i: (b, 0, 0, i, 0))
    l_spec = lambda d: pl.BlockSpec((None, d, tm // d, LANES), lambda b, i: (b, 0, i, 0))
    row = pl.BlockSpec((tm, D_MODEL), lambda b, i: (b * tiles + i, 0))
    return pl.pallas_call(
        _attn_out_kernel,
        out_shape=jax.ShapeDtypeStruct((t, D_MODEL), F32),
        grid=(batch, tiles),
        in_specs=([o_spec(d) for d in dils] + [l_spec(d) for d in dils]
                  + [row, _resident((D_MODEL, D_MODEL)), _resident((2 * LANES, D_MODEL)),
                     _resident((1, D_MODEL)), _resident((1, D_MODEL))]),
        out_specs=row,
        scratch_shapes=[pltpu.VMEM((D_MODEL // LANES, tm, LANES), F32)],
        compiler_params=pltpu.CompilerParams(
            dimension_semantics=("arbitrary", "arbitrary"), vmem_limit_bytes=VMEM_LIMIT),
        name="attn_out_ln",
    )(*os_, *lses, x2d, w_out_bf16, _head_expansion(), ln_g, ln_b)


def _ffn_kernel(x_ref, wu_ref, wd_ref, g_ref, b_ref, y_ref, h_ref):
    x = x_ref[...]
    xb = x.astype(BF16)
    for c in range(D_FF // D_MODEL):
        sl = slice(c * D_MODEL, (c + 1) * D_MODEL)
        h = jnp.dot(xb, wu_ref[:, sl], preferred_element_type=F32)
        h_ref[:, sl] = jnp.square(jnp.maximum(h, 0.0)).astype(BF16)
    y = jnp.dot(h_ref[...], wd_ref[...], preferred_element_type=F32)
    y_ref[...] = _layer_norm(DEEPNORM_ALPHA * x + y, g_ref[...], b_ref[...])


def _ffn(x2d, layer, w_up_bf16, w_down_bf16, ln_g, ln_b):
    t = x2d.shape[0]
    tm = FFN_TM
    one_layer = lambda shape: pl.BlockSpec((None, *shape), lambda i: (layer, 0, 0),
                                           pipeline_mode=pl.Buffered(1))
    return pl.pallas_call(
        _ffn_kernel,
        out_shape=jax.ShapeDtypeStruct((t, D_MODEL), F32),
        grid=(t // tm,),
        in_specs=[pl.BlockSpec((tm, D_MODEL), lambda i: (i, 0)),
                  one_layer((D_MODEL, D_FF)), one_layer((D_FF, D_MODEL)),
                  _resident((1, D_MODEL)), _resident((1, D_MODEL))],
        out_specs=pl.BlockSpec((tm, D_MODEL), lambda i: (i, 0)),
        scratch_shapes=[pltpu.VMEM((tm, D_FF), BF16)],
        compiler_params=pltpu.CompilerParams(
            dimension_semantics=("arbitrary",), vmem_limit_bytes=VMEM_LIMIT),
        name="ffn_ln",
    )(x2d, w_up_bf16, w_down_bf16, ln_g, ln_b)


def _hgrn_pair_chain(xb, wi_ref, p, lb, tri, block_causal, ng, states, on_ref, out_rows, t, done):
    hk = HGRN_HEADS * HGRN_DK
    dk = HGRN_DK
    c_len = HGRN_CHUNK
    pw = 2 * dk
    rows = xb.shape[0]
    n_chunks = rows // c_len
    cols = slice(p * pw, (p + 1) * pw)
    contract_last = (((1,), (1,)), ((), ()))
    contract_rows = (((0,), (0,)), ((), ()))

    q_raw = jnp.dot(xb, wi_ref[:, p * pw:(p + 1) * pw], preferred_element_type=F32)
    z = jnp.dot(xb, wi_ref[:, hk + p * pw:hk + (p + 1) * pw], preferred_element_type=F32)
    v = jnp.dot(xb, wi_ref[:, 2 * hk + p * pw:2 * hk + (p + 1) * pw], preferred_element_type=F32)
    yield
    lb_p = lb[:, cols]
    key = (1.0 - lb_p) / (1.0 + jnp.exp(z))
    log_f = jnp.log(1.0 - key)
    q = q_raw / (1.0 + jnp.exp(-q_raw))
    v_b = v.astype(BF16)
    hi = log_f.astype(BF16)
    lo = (log_f - hi.astype(F32)).astype(BF16)
    yield
    bcum = (jnp.dot(tri, hi, preferred_element_type=F32)
            + jnp.dot(tri, lo, preferred_element_type=F32))
    yield
    last = [bcum[(c + 1) * c_len - 1:(c + 1) * c_len] for c in range(n_chunks)]
    b_last = jnp.concatenate([jnp.broadcast_to(r, (c_len, pw)) for r in last], axis=0)
    q_dec = (q * jnp.exp(bcum)).astype(BF16)
    k_dec = (key * jnp.exp(-bcum)).astype(BF16)
    k_end = (key * jnp.exp(b_last - bcum)).astype(BF16)
    yield
    scores = [lax.dot_general(q_dec[:, hh * dk:(hh + 1) * dk], k_dec[:, hh * dk:(hh + 1) * dk],
                              contract_last, preferred_element_type=F32) for hh in range(2)]
    yield
    intra = [jnp.dot(jnp.where(block_causal, scores[hh], 0.0).astype(BF16),
                     v_b[:, hh * dk:(hh + 1) * dk], preferred_element_type=F32) for hh in range(2)]
    yield
    zero_st = jnp.zeros((HGRN_DV, dk), BF16)
    zero_k = jnp.zeros((c_len, dk), BF16)
    while t > 0 and (t - 1, p) not in done:
        yield
    st0, st1 = states[2 * p], states[2 * p + 1]
    inter = []
    for c in range(n_chunks):
        rs = slice(c * c_len, (c + 1) * c_len)
        st_pair = jnp.concatenate(
            [jnp.concatenate([st0.astype(BF16), zero_st], axis=1),
             jnp.concatenate([zero_st, st1.astype(BF16)], axis=1)], axis=0)
        inter.append(lax.dot_general(q_dec[rs], st_pair, contract_last,
                                     preferred_element_type=F32))
        v_rows = jnp.concatenate([v_b[rs, :dk], v_b[rs, dk:]], axis=0)
        k_rows = jnp.concatenate(
            [jnp.concatenate([k_end[rs, :dk], zero_k], axis=1),
             jnp.concatenate([zero_k, k_end[rs, dk:]], axis=1)], axis=0)
        kv = lax.dot_general(v_rows, k_rows, contract_rows, preferred_element_type=F32)
        decay = jnp.exp(last[c])
        st0 = decay[:, :dk] * st0 + kv[:, :dk]
        st1 = decay[:, dk:] * st1 + kv[:, dk:]
        if c % 2 == 1:
            yield
    states[2 * p], states[2 * p + 1] = st0, st1
    inter = jnp.concatenate(inter, axis=0)
    outs = []
    for hh in range(2):
        ls = slice(hh * dk, (hh + 1) * dk)
        o = intra[hh] + inter[:, ls]
        o = o * lax.rsqrt(jnp.mean(o * o, axis=-1, keepdims=True) + RMS_EPS) * ng[:, cols][:, ls]
        outs.append(o.astype(BF16))
    on_ref[out_rows, cols] = jnp.concatenate(outs, axis=1)
    done.add((t, p))


def _hgrn_out_chain(x_ref, on_ref, wo_ref, g_ref, b_ref, y_ref, rows, t, done):
    while any((t, p) not in done for p in range(HGRN_HEADS // 2)):
        yield
    y = jnp.dot(on_ref[rows, :], wo_ref[...], preferred_element_type=F32)
    yield
    y_ref[rows, :] = _layer_norm(DEEPNORM_ALPHA * x_ref[rows, :] + y, g_ref[...], b_ref[...])


def _run_staggered(chains, stagger):
    active, pending, tick = [], list(chains), 0
    while active or pending:
        if pending and tick % stagger == 0:
            active.append(pending.pop(0))
        tick += 1
        for gen in list(active):
            try:
                next(gen)
            except StopIteration:
                active.remove(gen)


def _hgrn_kernel(layer, sub, x_ref, wi_ref, wo_ref, lbl_ref, ng_ref, g_ref, b_ref, y_ref,
                 state_ref, on_ref):
    tm = x_ref.shape[0]
    c_len = HGRN_CHUNK

    @pl.when(pl.program_id(1) == 0)
    def _():
        state_ref[...] = jnp.zeros_like(state_ref)

    logits = lbl_ref[...]
    ex = jnp.exp(logits - jnp.max(logits, axis=0, keepdims=True))
    sm = ex / jnp.sum(ex, axis=0, keepdims=True)
    lb = jnp.sum(sm[1:layer + 1], axis=0, keepdims=True)

    ri = lax.broadcasted_iota(jnp.int32, (sub, sub), 0)
    ci = lax.broadcasted_iota(jnp.int32, (sub, sub), 1)
    block_causal = (ri // c_len == ci // c_len) & (ci <= ri)
    tri = block_causal.astype(F32).astype(BF16)
    ng = ng_ref[...]
    states = [state_ref[h] for h in range(HGRN_HEADS)]

    chains, done = [], set()
    for t in range(tm // sub):
        rows = slice(t * sub, (t + 1) * sub)
        xb = x_ref[rows, :].astype(BF16)
        for p in range(HGRN_HEADS // 2):
            chains.append(_hgrn_pair_chain(xb, wi_ref, p, lb, tri, block_causal, ng, states,
                                           on_ref, rows, t, done))
        chains.append(_hgrn_out_chain(x_ref, on_ref, wo_ref, g_ref, b_ref, y_ref, rows, t, done))
    _run_staggered(chains, HGRN_STAGGER)
    for h in range(HGRN_HEADS):
        state_ref[h] = states[h]


def _hgrn_mixer(layer, x2d, w_in_bf16, w_out_bf16, lb_logits, norm_g, ln_g, ln_b, batch, seq):
    t = x2d.shape[0]
    tm = HGRN_TM
    tiles = seq // tm
    d_in = w_in_bf16.shape[1]
    row = pl.BlockSpec((tm, D_MODEL), lambda b, i: (b * tiles + i, 0))
    return pl.pallas_call(
        functools.partial(_hgrn_kernel, layer, HGRN_SUB),
        out_shape=jax.ShapeDtypeStruct((t, D_MODEL), F32),
        grid=(batch, tiles),
        in_specs=[row, _resident((D_MODEL, d_in)), _resident((D_MODEL, D_MODEL)),
                  _resident((DEPTH, D_MODEL)), _resident((1, D_MODEL)),
                  _resident((1, D_MODEL)), _resident((1, D_MODEL))],
        out_specs=row,
        scratch_shapes=[pltpu.VMEM((HGRN_HEADS, HGRN_DV, HGRN_DK), F32),
                        pltpu.VMEM((tm, D_MODEL), BF16)],
        compiler_params=pltpu.CompilerParams(
            dimension_semantics=("arbitrary", "arbitrary"), vmem_limit_bytes=VMEM_LIMIT),
        name="hgrn2_mixer_ln",
    )(x2d, w_in_bf16, w_out_bf16, lb_logits, norm_g, ln_g, ln_b)


def kernel(x, attn_w_in, attn_w_out, hgrn_w_in, hgrn_w_out, hgrn_norm_g, lb_logits,
           ln_mix_g, ln_mix_b, ln_ffn_g, ln_ffn_b, ffn_w_up, ffn_w_down):
    batch, seq, d = x.shape
    assert d == D_MODEL and lb_logits.shape[0] == DEPTH
    for window, dilation in DILATED_PATTERNS:
        assert window // dilation == ATTN_BLK and seq % window == 0
        assert QKV_TM % (dilation * 16) == 0 and OUT_TM % (dilation * 8) == 0
    assert seq % QKV_TM == 0 and seq % OUT_TM == 0
    assert seq % HGRN_TM == 0 and HGRN_TM % HGRN_SUB == 0 and HGRN_SUB % HGRN_CHUNK == 0
    h = x.reshape(batch * seq, d)
    row = lambda a: a.reshape(1, -1)
    w_up = ffn_w_up.astype(BF16)
    w_down = ffn_w_down.astype(BF16)
    for i in range(DEPTH):
        j = i // 2
        if i % 2 == 0:
            w_all = _prepare_attn_w_in(attn_w_in[j])
            os_, lses = [], []
            for g, (_, dil) in enumerate(DILATED_PATTERNS):
                qkv = _qkv_rope(h, w_all, g, dil, batch, seq)
                o, lse = _attention_group(qkv, dil, batch, seq)
                os_.append(o)
                lses.append(lse)
            h = _attn_out(os_, lses, h, attn_w_out[j].astype(BF16),
                          row(ln_mix_g[i]), row(ln_mix_b[i]), batch, seq)
        else:
            h = _hgrn_mixer(i, h, hgrn_w_in[j].astype(BF16), hgrn_w_out[j].astype(BF16),
                            lb_logits, row(hgrn_norm_g[j]), row(ln_mix_g[i]), row(ln_mix_b[i]),
                            batch, seq)
        h = _ffn(h, i, w_up, w_down, row(ln_ffn_g[i]), row(ln_ffn_b[i]))
    return h.reshape(batch, seq, d)
```

```python
import functools
import math

import jax
import jax.numpy as jnp
from jax import lax
from jax.experimental import pallas as pl
from jax.experimental.pallas import tpu as pltpu
import numpy as np

F32 = jnp.float32
BF16 = jnp.bfloat16

D_MODEL = 1024
DEPTH = 2
ATTN_HEAD_DIM = 64
ATTN_HEADS = D_MODEL // ATTN_HEAD_DIM
DILATED_PATTERNS = ((128, 1), (512, 4), (2048, 16))
N_GROUPS = len(DILATED_PATTERNS)
MAX_DILATION = max(d for _, d in DILATED_PATTERNS)
ROPE_THETA = 10000.0
HGRN_HEADS = 8
HGRN_DK = 128
HGRN_DV = 128
HGRN_CHUNK = 64
D_FF = 4 * D_MODEL
LN_EPS = 1e-5
RMS_EPS = 1e-6
DEEPNORM_ALPHA = (2 * DEPTH) ** 0.25

LANES = 128
ATTN_BLK = 128
HALF = ATTN_HEAD_DIM // 2
MASK_VALUE = -1e30
LN2 = math.log(2.0)
Q_SCALE = ATTN_HEAD_DIM ** -0.5 / LN2
VMEM_LIMIT = 56 * 1024 * 1024
MAX_ROW_STRIDE = 4

QKV_TM = 1024
QKV_SUB = 256
ATTN_TQ = 512
OUT_TM = 512
FFN_TM = 512
HGRN_TM = 512
HGRN_SUB = 256
HGRN_STAGGER = 1


def _layer_norm(y, g, b):
    mu = jnp.mean(y, axis=-1, keepdims=True)
    d = y - mu
    var = jnp.mean(d * d, axis=-1, keepdims=True)
    return d * lax.rsqrt(var + LN_EPS) * g + b


def _resident(shape):
    nd = len(shape)
    return pl.BlockSpec(shape, lambda *_: (0,) * nd, pipeline_mode=pl.Buffered(1))


def _qkv_rope_kernel(dilation, x_ref, w_ref, tab_ref, o_ref, xb_ref, xs_ref):
    tm = x_ref.shape[0]
    n_per = tm // dilation
    n_chunks = D_MODEL // LANES

    if dilation == 1:
        xb_ref[...] = x_ref[...].astype(BF16)
    else:
        for c in range(n_chunks):
            xs_ref[0, c] = x_ref[:, c * LANES:(c + 1) * LANES]
        passes, left = [], dilation
        while left > 1:
            passes.append(min(left, MAX_ROW_STRIDE))
            left //= passes[-1]
        src, blocks = 0, 1
        for i, st in enumerate(passes):
            rows_blk = tm // blocks
            for blk in range(blocks):
                for r in range(st):
                    lo = (blk + r * blocks) * (rows_blk // st)
                    dst = slice(lo, lo + rows_blk // st)
                    parts = [xs_ref[src, c, pl.ds(blk * rows_blk + r, rows_blk // st, stride=st), :]
                             for c in range(n_chunks)]
                    if i == len(passes) - 1:
                        xb_ref[dst, :] = jnp.concatenate([v.astype(BF16) for v in parts], axis=1)
                    else:
                        for c in range(n_chunks):
                            xs_ref[1 - src, c, dst, :] = parts[c]
            src, blocks = 1 - src, blocks * st

    def store(kind, s, val):
        if n_per >= QKV_SUB:
            start = s * QKV_SUB
            o_ref[kind, start // n_per, start % n_per:start % n_per + QKV_SUB, :] = val
        else:
            per = QKV_SUB // n_per
            for c in range(per):
                o_ref[kind, s * per + c] = val[c * n_per:(c + 1) * n_per]

    for kind in range(3):
        cols = slice(kind * D_MODEL, (kind + 1) * D_MODEL)
        for s in range(tm // QKV_SUB):
            rows = slice(s * QKV_SUB, (s + 1) * QKV_SUB)
            acc = jnp.dot(xb_ref[rows], w_ref[:, cols], preferred_element_type=F32)
            if kind == 2:
                store(kind, s, acc.astype(BF16))
                continue
            cos = tab_ref[0, rows, :]
            sin = tab_ref[1, rows, :]
            if kind == 0:
                cos = cos * Q_SCALE
                sin = sin * Q_SCALE
            pieces = []
            for c in range(n_chunks):
                a = acc[:, c * LANES:(c + 1) * LANES]
                pieces.append((a * cos + pltpu.roll(a, LANES // 2, 1) * sin).astype(BF16))
            store(kind, s, jnp.concatenate(pieces, axis=1))


def _rope_table(seq, dilation, tm):
    inv = ROPE_THETA ** (-np.arange(HALF, dtype=np.float64) * (2.0 / ATTN_HEAD_DIM))
    ang = np.arange(seq, dtype=np.float64)[:, None] * inv[None, :]
    cos = np.tile(np.cos(ang), (1, LANES // HALF))
    sin = np.tile(np.sin(ang), (1, LANES // HALF))
    sign = np.where(np.arange(LANES) < LANES // 2, -1.0, 1.0)
    tab = np.stack([cos, sin * sign])
    tab = tab.reshape(2, seq // tm, tm // dilation, dilation, LANES)
    tab = tab.transpose(0, 1, 3, 2, 4).reshape(2, seq, LANES)
    return jnp.asarray(tab.astype(np.float32))


def _prepare_attn_w_in(w_in):
    k = w_in.shape[0]
    w = w_in.reshape(k, N_GROUPS, 3, ATTN_HEADS // 2, 2, 2, HALF)
    qk = w[:, :, :2].transpose(0, 1, 2, 3, 5, 4, 6)
    w = jnp.concatenate([qk, w[:, :, 2:]], axis=2)
    return w.reshape(k, N_GROUPS * 3 * D_MODEL).astype(BF16)


def _qkv_rope(x2d, w_all, g, dilation, batch, seq):
    tm = QKV_TM
    tiles = seq // tm
    n_per = tm // dilation
    tab = _rope_table(seq, dilation, tm)
    return pl.pallas_call(
        functools.partial(_qkv_rope_kernel, dilation),
        out_shape=jax.ShapeDtypeStruct((3, batch, dilation, seq // dilation, D_MODEL), BF16),
        grid=(batch, tiles),
        in_specs=[
            pl.BlockSpec((tm, D_MODEL), lambda b, i: (b * tiles + i, 0)),
            pl.BlockSpec((D_MODEL, 3 * D_MODEL), lambda b, i: (0, g), pipeline_mode=pl.Buffered(1)),
            pl.BlockSpec((2, tm, LANES), lambda b, i: (0, i, 0)),
        ],
        out_specs=pl.BlockSpec((3, None, dilation, n_per, D_MODEL), lambda b, i: (0, b, 0, i, 0)),
        scratch_shapes=[pltpu.VMEM((tm, D_MODEL), BF16),
                        pltpu.VMEM((2, D_MODEL // LANES, tm, LANES), F32)],
        compiler_params=pltpu.CompilerParams(
            dimension_semantics=("arbitrary", "arbitrary"), vmem_limit_bytes=VMEM_LIMIT),
        name=f"qkv_rope_d{dilation}",
    )(x2d, w_all, tab)


def _attn_kernel(q_ref, kp_ref, kc_ref, vp_ref, vc_ref, o_ref, lse_ref):
    i = pl.program_id(2)
    blk = ATTN_BLK
    tq = q_ref.shape[0]
    row = lax.broadcasted_iota(jnp.int32, (2 * blk, 2 * blk), 0) % blk
    col = lax.broadcasted_iota(jnp.int32, (2 * blk, 2 * blk), 1)
    valid = (col >= row) & (col <= row + blk)
    bias = jnp.where(valid, 0.0, MASK_VALUE).astype(F32)
    bias_first = jnp.where(valid & ((col >= blk) | (i > 0)), 0.0, MASK_VALUE).astype(F32)
    lane = lax.broadcasted_iota(jnp.int32, (blk, LANES), 1)
    qk_head0 = ((lane // HALF) % 2 == 0).astype(F32).astype(BF16)
    qk_head1 = ((lane // HALF) % 2 == 1).astype(F32).astype(BF16)
    v_head0 = lane < ATTN_HEAD_DIM
    ones = jnp.ones((2 * blk, LANES), BF16)
    for qb in range(tq // blk):
        rows = slice(qb * blk, (qb + 1) * blk)
        prev_rows = slice((qb - 1) * blk, qb * blk)
        m_tile = jnp.zeros((blk, LANES), F32)
        l_tile = jnp.ones((blk, LANES), F32)
        for p in range(ATTN_HEADS // 2):
            sl = slice(p * LANES, (p + 1) * LANES)
            q = q_ref[rows, sl]
            qs = jnp.concatenate([q * qk_head0, q * qk_head1], axis=0)
            k_prev = kp_ref[:, sl] if qb == 0 else kc_ref[prev_rows, sl]
            v_prev = vp_ref[:, sl] if qb == 0 else vc_ref[prev_rows, sl]
            k = jnp.concatenate([k_prev, kc_ref[rows, sl]], axis=0)
            v = jnp.concatenate([v_prev, vc_ref[rows, sl]], axis=0)
            s = lax.dot_general(qs, k, (((1,), (1,)), ((), ())), preferred_element_type=F32)
            s = s + (bias_first if qb == 0 else bias)
            m = jnp.max(s, axis=-1, keepdims=True)
            e = jnp.exp2(s - m).astype(BF16)
            pv = jnp.dot(e, jnp.concatenate([v, ones], axis=1),
                         preferred_element_type=F32)
            l_rep = pv[:, LANES:]
            o2 = pv[:, :LANES] * (1.0 / l_rep)
            o_ref[p, rows, :] = jnp.where(v_head0, o2[:blk], o2[blk:])
            m_tile = jnp.where(lane == 2 * p, m[:blk], m_tile)
            m_tile = jnp.where(lane == 2 * p + 1, m[blk:], m_tile)
            l_tile = jnp.where(lane == 2 * p, l_rep[:blk], l_tile)
            l_tile = jnp.where(lane == 2 * p + 1, l_rep[blk:], l_tile)
        lse_ref[rows, :] = m_tile * LN2 + jnp.log(l_tile)


def _attention_group(qkv, dilation, batch, seq):
    n = seq // dilation
    tq = min(ATTN_TQ, n)
    per = tq // ATTN_BLK
    pairs = ATTN_HEADS // 2

    def cur(which):
        return pl.BlockSpec((None, None, None, tq, D_MODEL), lambda b, r, i: (which, b, r, i, 0))

    def prev(which):
        return pl.BlockSpec((None, None, None, ATTN_BLK, D_MODEL),
                            lambda b, r, i: (which, b, r, jnp.maximum(i * per - 1, 0), 0))

    return pl.pallas_call(
        _attn_kernel,
        out_shape=(jax.ShapeDtypeStruct((batch, dilation, pairs, n, LANES), F32),
                   jax.ShapeDtypeStruct((batch, dilation, n, LANES), F32)),
        grid=(batch, dilation, n // tq),
        in_specs=[cur(0), prev(1), cur(1), prev(2), cur(2)],
        out_specs=(pl.BlockSpec((None, None, pairs, tq, LANES), lambda b, r, i: (b, r, 0, i, 0)),
                   pl.BlockSpec((None, None, tq, LANES), lambda b, r, i: (b, r, i, 0))),
        compiler_params=pltpu.CompilerParams(
            dimension_semantics=("arbitrary", "arbitrary", "arbitrary"),
            vmem_limit_bytes=VMEM_LIMIT),
        name=f"dilated_attn_d{dilation}",
    )(qkv, qkv, qkv, qkv, qkv)


def _class_rows(ref, lead, dilation, r16, n16):
    step = MAX_DILATION // dilation
    if step == 1:
        return ref[(r16, *lead)]
    return ref[(r16 % dilation, *lead, pl.ds(r16 // dilation, n16, stride=step), slice(None))]


def _attn_out_kernel(o0_ref, o1_ref, o2_ref, l0_ref, l1_ref, l2_ref, x_ref, w_ref, ex_ref,
                     g_ref, b_ref, y_ref, proj_ref, o0s_ref, l0s_ref):
    tm = x_ref.shape[0]
    n16 = tm // MAX_DILATION
    n_chunks = D_MODEL // LANES
    pairs = ATTN_HEADS // 2
    classes = range(MAX_DILATION)
    mid = DILATED_PATTERNS[1][1]
    assert [d for _, d in DILATED_PATTERNS] == [1, mid, MAX_DILATION]

    for r in range(mid):
        l0s_ref[r, :, :] = l0_ref[0, pl.ds(r, tm // mid, stride=mid), :]
        for p in range(pairs):
            o0s_ref[r, p, :, :] = o0_ref[0, p, pl.ds(r, tm // mid, stride=mid), :]
    o_refs = ((o0s_ref, mid), (o1_ref, mid), (o2_ref, MAX_DILATION))
    l_refs = ((l0s_ref, mid), (l1_ref, mid), (l2_ref, MAX_DILATION))

    lses = [jnp.concatenate([_class_rows(ref, (), d, r, n16) for r in classes], axis=0)
            for ref, d in l_refs]
    mx = jnp.maximum(jnp.maximum(lses[0], lses[1]), lses[2])
    es = [jnp.exp(v - mx) for v in lses]
    inv = 1.0 / (es[0] + es[1] + es[2])
    halves = []
    for e in es:
        w = e * inv
        hi = w.astype(BF16)
        halves.append(jnp.concatenate([hi, (w - hi.astype(F32)).astype(BF16)], axis=1))

    pieces = []
    for pp in range(ATTN_HEADS // 4):
        cols = slice(2 * pp * LANES, (2 * pp + 2) * LANES)
        mixed = jnp.zeros((tm, 2 * LANES), F32)
        for g, (ref, d) in enumerate(o_refs):
            w_wide = jnp.dot(halves[g], ex_ref[:, cols], preferred_element_type=F32)
            o_g = jnp.concatenate(
                [jnp.concatenate([_class_rows(ref, (p,), d, r, n16) for r in classes], axis=0)
                 for p in (2 * pp, 2 * pp + 1)], axis=1)
            mixed = mixed + w_wide * o_g
        pieces.append(mixed.astype(BF16))
    proj = jnp.dot(jnp.concatenate(pieces, axis=1), w_ref[...], preferred_element_type=F32)
    pitch = proj_ref.shape[1] // MAX_DILATION
    for c in range(n_chunks):
        for r in classes:
            proj_ref[c, r * pitch:r * pitch + n16, :] = proj[r * n16:(r + 1) * n16,
                                                             c * LANES:(c + 1) * LANES]
    g = g_ref[...]
    b = b_ref[...]
    for n in range(n16):
        tok = slice(n * MAX_DILATION, (n + 1) * MAX_DILATION)
        y = jnp.concatenate([proj_ref[c, pl.ds(n, MAX_DILATION, stride=pitch), :]
                             for c in range(n_chunks)], axis=1)
        y_ref[tok, :] = _layer_norm(DEEPNORM_ALPHA * x_ref[tok, :] + y, g, b)


def _head_expansion():
    e = (np.arange(D_MODEL)[None, :] // ATTN_HEAD_DIM == np.arange(LANES)[:, None])
    return jnp.asarray(np.concatenate([e, e], axis=0).astype(np.float32), dtype=BF16)


def _attn_out(os_, lses, x2d, w_out_bf16, ln_g, ln_b, batch, seq):
    t = x2d.shape[0]
    tm = OUT_TM
    tiles = seq // tm
    pairs = ATTN_HEADS // 2
    dils = [d for _, d in DILATED_PATTERNS]
    o_spec = lambda d: pl.BlockSpec((None, d, pairs, tm // d, LANES), lambda b, i: (b, 0, 0, i, 0))
    l_spec = lambda d: pl.BlockSpec((None, d, tm // d, LANES), lambda b, i: (b, 0, i, 0))
    row = pl.BlockSpec((tm, D_MODEL), lambda b, i: (b * tiles + i, 0))
    return pl.pallas_call(
        _attn_out_kernel,
        out_shape=jax.ShapeDtypeStruct((t, D_MODEL), F32),
        grid=(batch, tiles),
        in_specs=([o_spec(d) for d in dils] + [l_spec(d) for d in dils]
                  + [row, _resident((D_MODEL, D_MODEL)), _resident((2 * LANES, D_MODEL)),
                     _resident((1, D_MODEL)), _resident((1, D_MODEL))]),
        out_specs=row,
        scratch_shapes=[pltpu.VMEM((D_MODEL // LANES, tm + 8 * MAX_DILATION, LANES), F32),
                        pltpu.VMEM((dils[1], pairs, tm // dils[1], LANES), F32),
                        pltpu.VMEM((dils[1], tm // dils[1], LANES), F32)],
        compiler_params=pltpu.CompilerParams(
            dimension_semantics=("arbitrary", "arbitrary"), vmem_limit_bytes=VMEM_LIMIT),
        name="attn_out_ln",
    )(*os_, *lses, x2d, w_out_bf16, _head_expansion(), ln_g, ln_b)


def _ffn_kernel(x_ref, wu_ref, wd_ref, g_ref, b_ref, y_ref, h_ref):
    x = x_ref[...]
    xb = x.astype(BF16)
    for c in range(D_FF // D_MODEL):
        sl = slice(c * D_MODEL, (c + 1) * D_MODEL)
        h = jnp.dot(xb, wu_ref[:, sl], preferred_element_type=F32)
        h_ref[:, sl] = jnp.square(jnp.maximum(h, 0.0)).astype(BF16)
    y = jnp.dot(h_ref[...], wd_ref[...], preferred_element_type=F32)
    y_ref[...] = _layer_norm(DEEPNORM_ALPHA * x + y, g_ref[...], b_ref[...])


def _ffn(x2d, layer, w_up_bf16, w_down_bf16, ln_g, ln_b):
    t = x2d.shape[0]
    tm = FFN_TM
    one_layer = lambda shape: pl.BlockSpec((None, *shape), lambda i: (layer, 0, 0),
                                           pipeline_mode=pl.Buffered(1))
    return pl.pallas_call(
        _ffn_kernel,
        out_shape=jax.ShapeDtypeStruct((t, D_MODEL), F32),
        grid=(t // tm,),
        in_specs=[pl.BlockSpec((tm, D_MODEL), lambda i: (i, 0)),
                  one_layer((D_MODEL, D_FF)), one_layer((D_FF, D_MODEL)),
                  _resident((1, D_MODEL)), _resident((1, D_MODEL))],
        out_specs=pl.BlockSpec((tm, D_MODEL), lambda i: (i, 0)),
        scratch_shapes=[pltpu.VMEM((tm, D_FF), BF16)],
        compiler_params=pltpu.CompilerParams(
            dimension_semantics=("arbitrary",), vmem_limit_bytes=VMEM_LIMIT),
        name="ffn_ln",
    )(x2d, w_up_bf16, w_down_bf16, ln_g, ln_b)


def _hgrn_pair_chain(xb, wi_ref, p, lb, tri, block_causal, ng, states, on_ref, out_rows, t, done):
    hk = HGRN_HEADS * HGRN_DK
    dk = HGRN_DK
    c_len = HGRN_CHUNK
    pw = 2 * dk
    rows = xb.shape[0]
    n_chunks = rows // c_len
    cols = slice(p * pw, (p + 1) * pw)
    contract_last = (((1,), (1,)), ((), ()))
    contract_rows = (((0,), (0,)), ((), ()))

    q_raw = jnp.dot(xb, wi_ref[:, p * pw:(p + 1) * pw], preferred_element_type=F32)
    z = jnp.dot(xb, wi_ref[:, hk + p * pw:hk + (p + 1) * pw], preferred_element_type=F32)
    v = jnp.dot(xb, wi_ref[:, 2 * hk + p * pw:2 * hk + (p + 1) * pw], preferred_element_type=F32)
    yield
    lb_p = lb[:, cols]
    key = (1.0 - lb_p) / (1.0 + jnp.exp(z))
    log_f = jnp.log(1.0 - key)
    q = q_raw / (1.0 + jnp.exp(-q_raw))
    v_b = v.astype(BF16)
    hi = log_f.astype(BF16)
    lo = (log_f - hi.astype(F32)).astype(BF16)
    yield
    bcum = (jnp.dot(tri, hi, preferred_element_type=F32)
            + jnp.dot(tri, lo, preferred_element_type=F32))
    yield
    last = [bcum[(c + 1) * c_len - 1:(c + 1) * c_len] for c in range(n_chunks)]
    b_last = jnp.concatenate([jnp.broadcast_to(r, (c_len, pw)) for r in last], axis=0)
    q_dec = (q * jnp.exp(bcum)).astype(BF16)
    k_dec = (key * jnp.exp(-bcum)).astype(BF16)
    k_end = (key * jnp.exp(b_last - bcum)).astype(BF16)
    yield
    scores = [lax.dot_general(q_dec[:, hh * dk:(hh + 1) * dk], k_dec[:, hh * dk:(hh + 1) * dk],
                              contract_last, preferred_element_type=F32) for hh in range(2)]
    yield
    intra = [jnp.dot(jnp.where(block_causal, scores[hh], 0.0).astype(BF16),
                     v_b[:, hh * dk:(hh + 1) * dk], preferred_element_type=F32) for hh in range(2)]
    yield
    zero_st = jnp.zeros((HGRN_DV, dk), BF16)
    zero_k = jnp.zeros((c_len, dk), BF16)
    while t > 0 and (t - 1, p) not in done:
        yield
    st0, st1 = states[2 * p], states[2 * p + 1]
    inter = []
    for c in range(n_chunks):
        rs = slice(c * c_len, (c + 1) * c_len)
        st_pair = jnp.concatenate(
            [jnp.concatenate([st0.astype(BF16), zero_st], axis=1),
             jnp.concatenate([zero_st, st1.astype(BF16)], axis=1)], axis=0)
        inter.append(lax.dot_general(q_dec[rs], st_pair, contract_last,
                                     preferred_element_type=F32))
        v_rows = jnp.concatenate([v_b[rs, :dk], v_b[rs, dk:]], axis=0)
        k_rows = jnp.concatenate(
            [jnp.concatenate([k_end[rs, :dk], zero_k], axis=1),
             jnp.concatenate([zero_k, k_end[rs, dk:]], axis=1)], axis=0)
        kv = lax.dot_general(v_rows, k_rows, contract_rows, preferred_element_type=F32)
        decay = jnp.exp(last[c])
        st0 = decay[:, :dk] * st0 + kv[:, :dk]
        st1 = decay[:, dk:] * st1 + kv[:, dk:]
        if c % 2 == 1:
            yield
    states[2 * p], states[2 * p + 1] = st0, st1
    inter = jnp.concatenate(inter, axis=0)
    outs = []
    for hh in range(2):
        ls = slice(hh * dk, (hh + 1) * dk)
        o = intra[hh] + inter[:, ls]
        o = o * lax.rsqrt(jnp.mean(o * o, axis=-1, keepdims=True) + RMS_EPS) * ng[:, cols][:, ls]
        outs.append(o.astype(BF16))
    on_ref[out_rows, cols] = jnp.concatenate(outs, axis=1)
    done.add((t, p))


def _hgrn_out_chain(x_ref, on_ref, wo_ref, g_ref, b_ref, y_ref, rows, t, done):
    while any((t, p) not in done for p in range(HGRN_HEADS // 2)):
        yield
    y = jnp.dot(on_ref[rows, :], wo_ref[...], preferred_element_type=F32)
    yield
    y_ref[rows, :] = _layer_norm(DEEPNORM_ALPHA * x_ref[rows, :] + y, g_ref[...], b_ref[...])


def _run_staggered(chains, stagger):
    active, pending, tick = [], list(chains), 0
    while active or pending:
        if pending and tick % stagger == 0:
            active.append(pending.pop(0))
        tick += 1
        for gen in list(active):
            try:
                next(gen)
            except StopIteration:
                active.remove(gen)


def _hgrn_kernel(layer, sub, x_ref, wi_ref, wo_ref, lbl_ref, ng_ref, g_ref, b_ref, y_ref,
                 state_ref, on_ref):
    tm = x_ref.shape[0]
    c_len = HGRN_CHUNK

    @pl.when(pl.program_id(1) == 0)
    def _():
        state_ref[...] = jnp.zeros_like(state_ref)

    logits = lbl_ref[...]
    ex = jnp.exp(logits - jnp.max(logits, axis=0, keepdims=True))
    sm = ex / jnp.sum(ex, axis=0, keepdims=True)
    lb = jnp.sum(sm[1:layer + 1], axis=0, keepdims=True)

    ri = lax.broadcasted_iota(jnp.int32, (sub, sub), 0)
    ci = lax.broadcasted_iota(jnp.int32, (sub, sub), 1)
    block_causal = (ri // c_len == ci // c_len) & (ci <= ri)
    tri = block_causal.astype(F32).astype(BF16)
    ng = ng_ref[...]
    states = [state_ref[h] for h in range(HGRN_HEADS)]

    chains, done = [], set()
    for t in range(tm // sub):
        rows = slice(t * sub, (t + 1) * sub)
        xb = x_ref[rows, :].astype(BF16)
        for p in range(HGRN_HEADS // 2):
            chains.append(_hgrn_pair_chain(xb, wi_ref, p, lb, tri, block_causal, ng, states,
                                           on_ref, rows, t, done))
        chains.append(_hgrn_out_chain(x_ref, on_ref, wo_ref, g_ref, b_ref, y_ref, rows, t, done))
    _run_staggered(chains, HGRN_STAGGER)
    for h in range(HGRN_HEADS):
        state_ref[h] = states[h]


def _hgrn_mixer(layer, x2d, w_in_bf16, w_out_bf16, lb_logits, norm_g, ln_g, ln_b, batch, seq):
    t = x2d.shape[0]
    tm = HGRN_TM
    tiles = seq // tm
    d_in = w_in_bf16.shape[1]
    row = pl.BlockSpec((tm, D_MODEL), lambda b, i: (b * tiles + i, 0))
    return pl.pallas_call(
        functools.partial(_hgrn_kernel, layer, HGRN_SUB),
        out_shape=jax.ShapeDtypeStruct((t, D_MODEL), F32),
        grid=(batch, tiles),
        in_specs=[row, _resident((D_MODEL, d_in)), _resident((D_MODEL, D_MODEL)),
                  _resident((DEPTH, D_MODEL)), _resident((1, D_MODEL)),
                  _resident((1, D_MODEL)), _resident((1, D_MODEL))],
        out_specs=row,
        scratch_shapes=[pltpu.VMEM((HGRN_HEADS, HGRN_DV, HGRN_DK), F32),
                        pltpu.VMEM((tm, D_MODEL), BF16)],
        compiler_params=pltpu.CompilerParams(
            dimension_semantics=("arbitrary", "arbitrary"), vmem_limit_bytes=VMEM_LIMIT),
        name="hgrn2_mixer_ln",
    )(x2d, w_in_bf16, w_out_bf16, lb_logits, norm_g, ln_g, ln_b)


def kernel(x, attn_w_in, attn_w_out, hgrn_w_in, hgrn_w_out, hgrn_norm_g, lb_logits,
           ln_mix_g, ln_mix_b, ln_ffn_g, ln_ffn_b, ffn_w_up, ffn_w_down):
    batch, seq, d = x.shape
    assert d == D_MODEL and lb_logits.shape[0] == DEPTH
    for window, dilation in DILATED_PATTERNS:
        assert window // dilation == ATTN_BLK and seq % window == 0
        assert QKV_TM % (dilation * 16) == 0 and OUT_TM % (dilation * 8) == 0
    assert seq % QKV_TM == 0 and seq % OUT_TM == 0
    assert seq % HGRN_TM == 0 and HGRN_TM % HGRN_SUB == 0 and HGRN_SUB % HGRN_CHUNK == 0
    h = x.reshape(batch * seq, d)
    row = lambda a: a.reshape(1, -1)
    w_up = ffn_w_up.astype(BF16)
    w_down = ffn_w_down.astype(BF16)
    for i in range(DEPTH):
        j = i // 2
        if i % 2 == 0:
            w_all = _prepare_attn_w_in(attn_w_in[j])
            os_, lses = [], []
            for g, (_, dil) in enumerate(DILATED_PATTERNS):
                qkv = _qkv_rope(h, w_all, g, dil, batch, seq)
                o, lse = _attention_group(qkv, dil, batch, seq)
                os_.append(o)
                lses.append(lse)
            h = _attn_out(os_, lses, h, attn_w_out[j].astype(BF16),
                          row(ln_mix_g[i]), row(ln_mix_b[i]), batch, seq)
        else:
            h = _hgrn_mixer(i, h, hgrn_w_in[j].astype(BF16), hgrn_w_out[j].astype(BF16),
                            lb_logits, row(hgrn_norm_g[j]), row(ln_mix_g[i]), row(ln_mix_b[i]),
                            batch, seq)
        h = _ffn(h, i, w_up, w_down, row(ln_ffn_g[i]), row(ln_ffn_b[i]))
    return h.reshape(batch, seq, d)
```

```python
import functools
import math

import jax
import jax.numpy as jnp
from jax import lax
from jax.experimental import pallas as pl
from jax.experimental.pallas import tpu as pltpu
import numpy as np

F32 = jnp.float32
BF16 = jnp.bfloat16

D_MODEL = 1024
DEPTH = 2
ATTN_HEAD_DIM = 64
ATTN_HEADS = D_MODEL // ATTN_HEAD_DIM
DILATED_PATTERNS = ((128, 1), (512, 4), (2048, 16))
N_GROUPS = len(DILATED_PATTERNS)
MAX_DILATION = max(d for _, d in DILATED_PATTERNS)
ROPE_THETA = 10000.0
HGRN_HEADS = 8
HGRN_DK = 128
HGRN_DV = 128
HGRN_CHUNK = 64
D_FF = 4 * D_MODEL
LN_EPS = 1e-5
RMS_EPS = 1e-6
DEEPNORM_ALPHA = (2 * DEPTH) ** 0.25

LANES = 128
ATTN_BLK = 128
HALF = ATTN_HEAD_DIM // 2
MASK_VALUE = -1e30
LN2 = math.log(2.0)
Q_SCALE = ATTN_HEAD_DIM ** -0.5 / LN2
VMEM_LIMIT = 56 * 1024 * 1024
MAX_ROW_STRIDE = 4
W_CHUNK_ROWS = 1024
W_CHUNK_COLS = 256

QKV_TM = 1024
QKV_SUB = 256
ATTN_TQ = 512
OUT_TM = 512
FFN_TM = 1024
FFN_SUB = 256
HGRN_TM = 512
HGRN_SUB = 256
HGRN_STAGGER = 1


def _layer_norm(y, g, b):
    mu = jnp.mean(y, axis=-1, keepdims=True)
    d = y - mu
    var = jnp.mean(d * d, axis=-1, keepdims=True)
    return d * lax.rsqrt(var + LN_EPS) * g + b


def _resident(shape):
    nd = len(shape)
    return pl.BlockSpec(shape, lambda *_: (0,) * nd, pipeline_mode=pl.Buffered(1))


_HBM = pl.BlockSpec(memory_space=pl.ANY)


def _first_step(grid_rank):
    ids = [pl.program_id(a) == 0 for a in range(grid_rank)]
    return functools.reduce(jnp.logical_and, ids)


def _weight_scratch():
    return [pltpu.VMEM((2, W_CHUNK_ROWS, W_CHUNK_COLS), F32), pltpu.SemaphoreType.DMA((2,))]


def _load_weights(chunks, stage_ref, sem_ref, store):
    copies = [pltpu.make_async_copy(src, stage_ref.at[i % 2], sem_ref.at[i % 2])
              for i, src in enumerate(chunks)]
    copies[0].start()
    for i, cp in enumerate(copies):
        if i + 1 < len(copies):
            copies[i + 1].start()
        cp.wait()
        store(i, stage_ref[i % 2])


def _weight_chunks(w_hbm, lead, n_rows, col0, n_cols):
    out = []
    for r in range(0, n_rows, W_CHUNK_ROWS):
        for c in range(0, n_cols, W_CHUNK_COLS):
            view = w_hbm.at[(*lead, pl.ds(r, W_CHUNK_ROWS), pl.ds(col0 + c, W_CHUNK_COLS))]
            out.append((view, r, c))
    return out


def _load_plain_weights(w_hbm, lead, w_ref, stage_ref, sem_ref):
    chunks = _weight_chunks(w_hbm, lead, w_ref.shape[0], 0, w_ref.shape[1])

    def store(i, val):
        _, r, c = chunks[i]
        w_ref[r:r + W_CHUNK_ROWS, c:c + W_CHUNK_COLS] = val.astype(BF16)

    _load_weights([v for v, _, _ in chunks], stage_ref, sem_ref, store)


def _qkv_rope_kernel(dilation, layer, g, x_ref, w_hbm, tab_ref, o_ref, xb_ref, xs_ref, w_ref,
                     stage_ref, sem_ref):
    tm = x_ref.shape[0]
    n_per = tm // dilation
    n_chunks = D_MODEL // LANES

    @pl.when(_first_step(2))
    def _():
        chunks = _weight_chunks(w_hbm, (layer,), D_MODEL, 3 * g * D_MODEL, 3 * D_MODEL)
        lane = lax.broadcasted_iota(jnp.int32, (1, LANES), 1)
        from_right = (lane >= HALF) & (lane < 2 * HALF)
        from_left = (lane >= 2 * HALF) & (lane < 3 * HALF)

        def store(i, val):
            _, _, c = chunks[i]
            if c < 2 * D_MODEL:
                parts = []
                for j in range(W_CHUNK_COLS // LANES):
                    a = val[:, j * LANES:(j + 1) * LANES]
                    parts.append(jnp.where(from_right, pltpu.roll(a, LANES - HALF, 1),
                                           jnp.where(from_left, pltpu.roll(a, HALF, 1), a)))
                val = jnp.concatenate(parts, axis=1)
            w_ref[:, c:c + W_CHUNK_COLS] = val.astype(BF16)

        _load_weights([v for v, _, _ in chunks], stage_ref, sem_ref, store)

    if dilation == 1:
        xb_ref[...] = x_ref[...].astype(BF16)
    else:
        for c in range(n_chunks):
            xs_ref[0, c] = x_ref[:, c * LANES:(c + 1) * LANES]
        passes, left = [], dilation
        while left > 1:
            passes.append(min(left, MAX_ROW_STRIDE))
            left //= passes[-1]
        src, blocks = 0, 1
        for i, st in enumerate(passes):
            rows_blk = tm // blocks
            for blk in range(blocks):
                for r in range(st):
                    lo = (blk + r * blocks) * (rows_blk // st)
                    dst = slice(lo, lo + rows_blk // st)
                    parts = [xs_ref[src, c, pl.ds(blk * rows_blk + r, rows_blk // st, stride=st), :]
                             for c in range(n_chunks)]
                    if i == len(passes) - 1:
                        xb_ref[dst, :] = jnp.concatenate([v.astype(BF16) for v in parts], axis=1)
                    else:
                        for c in range(n_chunks):
                            xs_ref[1 - src, c, dst, :] = parts[c]
            src, blocks = 1 - src, blocks * st

    def store(kind, s, val):
        if n_per >= QKV_SUB:
            start = s * QKV_SUB
            o_ref[kind, start // n_per, start % n_per:start % n_per + QKV_SUB, :] = val
        else:
            per = QKV_SUB // n_per
            for c in range(per):
                o_ref[kind, s * per + c] = val[c * n_per:(c + 1) * n_per]

    for kind in range(3):
        cols = slice(kind * D_MODEL, (kind + 1) * D_MODEL)
        for s in range(tm // QKV_SUB):
            rows = slice(s * QKV_SUB, (s + 1) * QKV_SUB)
            acc = jnp.dot(xb_ref[rows], w_ref[:, cols], preferred_element_type=F32)
            if kind == 2:
                store(kind, s, acc.astype(BF16))
                continue
            cos = tab_ref[0, rows, :]
            sin = tab_ref[1, rows, :]
            if kind == 0:
                cos = cos * Q_SCALE
                sin = sin * Q_SCALE
            pieces = []
            for c in range(n_chunks):
                a = acc[:, c * LANES:(c + 1) * LANES]
                pieces.append((a * cos + pltpu.roll(a, LANES // 2, 1) * sin).astype(BF16))
            store(kind, s, jnp.concatenate(pieces, axis=1))


def _rope_table(seq, dilation, tm):
    inv = ROPE_THETA ** (-np.arange(HALF, dtype=np.float64) * (2.0 / ATTN_HEAD_DIM))
    ang = np.arange(seq, dtype=np.float64)[:, None] * inv[None, :]
    cos = np.tile(np.cos(ang), (1, LANES // HALF))
    sin = np.tile(np.sin(ang), (1, LANES // HALF))
    sign = np.where(np.arange(LANES) < LANES // 2, -1.0, 1.0)
    tab = np.stack([cos, sin * sign])
    tab = tab.reshape(2, seq // tm, tm // dilation, dilation, LANES)
    tab = tab.transpose(0, 1, 3, 2, 4).reshape(2, seq, LANES)
    return jnp.asarray(tab.astype(np.float32))


def _qkv_rope(x2d, w_in, layer, g, dilation, batch, seq):
    tm = QKV_TM
    tiles = seq // tm
    n_per = tm // dilation
    tab = _rope_table(seq, dilation, tm)
    return pl.pallas_call(
        functools.partial(_qkv_rope_kernel, dilation, layer, g),
        out_shape=jax.ShapeDtypeStruct((3, batch, dilation, seq // dilation, D_MODEL), BF16),
        grid=(batch, tiles),
        in_specs=[
            pl.BlockSpec((tm, D_MODEL), lambda b, i: (b * tiles + i, 0)),
            _HBM,
            pl.BlockSpec((2, tm, LANES), lambda b, i: (0, i, 0)),
        ],
        out_specs=pl.BlockSpec((3, None, dilation, n_per, D_MODEL), lambda b, i: (0, b, 0, i, 0)),
        scratch_shapes=[pltpu.VMEM((tm, D_MODEL), BF16),
                        pltpu.VMEM((2, D_MODEL // LANES, tm, LANES), F32),
                        pltpu.VMEM((D_MODEL, 3 * D_MODEL), BF16)] + _weight_scratch(),
        compiler_params=pltpu.CompilerParams(
            dimension_semantics=("arbitrary", "arbitrary"), vmem_limit_bytes=VMEM_LIMIT),
        name=f"qkv_rope_d{dilation}",
    )(x2d, w_in, tab)


def _attn_kernel(q_ref, kp_ref, kc_ref, vp_ref, vc_ref, o_ref, lse_ref):
    i = pl.program_id(2)
    blk = ATTN_BLK
    tq = q_ref.shape[0]
    row = lax.broadcasted_iota(jnp.int32, (2 * blk, 2 * blk), 0) % blk
    col = lax.broadcasted_iota(jnp.int32, (2 * blk, 2 * blk), 1)
    valid = (col >= row) & (col <= row + blk)
    bias = jnp.where(valid, 0.0, MASK_VALUE).astype(F32)
    bias_first = jnp.where(valid & ((col >= blk) | (i > 0)), 0.0, MASK_VALUE).astype(F32)
    lane = lax.broadcasted_iota(jnp.int32, (blk, LANES), 1)
    qk_head0 = ((lane // HALF) % 2 == 0).astype(F32).astype(BF16)
    qk_head1 = ((lane // HALF) % 2 == 1).astype(F32).astype(BF16)
    v_head0 = lane < ATTN_HEAD_DIM
    ones = jnp.ones((2 * blk, LANES), BF16)
    for qb in range(tq // blk):
        rows = slice(qb * blk, (qb + 1) * blk)
        prev_rows = slice((qb - 1) * blk, qb * blk)
        m_tile = jnp.zeros((blk, LANES), F32)
        l_tile = jnp.ones((blk, LANES), F32)
        for p in range(ATTN_HEADS // 2):
            sl = slice(p * LANES, (p + 1) * LANES)
            q = q_ref[rows, sl]
            qs = jnp.concatenate([q * qk_head0, q * qk_head1], axis=0)
            k_prev = kp_ref[:, sl] if qb == 0 else kc_ref[prev_rows, sl]
            v_prev = vp_ref[:, sl] if qb == 0 else vc_ref[prev_rows, sl]
            k = jnp.concatenate([k_prev, kc_ref[rows, sl]], axis=0)
            v = jnp.concatenate([v_prev, vc_ref[rows, sl]], axis=0)
            s = lax.dot_general(qs, k, (((1,), (1,)), ((), ())), preferred_element_type=F32)
            s = s + (bias_first if qb == 0 else bias)
            m = jnp.max(s, axis=-1, keepdims=True)
            e = jnp.exp2(s - m).astype(BF16)
            pv = jnp.dot(e, jnp.concatenate([v, ones], axis=1),
                         preferred_element_type=F32)
            l_rep = pv[:, LANES:]
            o2 = pv[:, :LANES] * (1.0 / l_rep)
            o_ref[p, rows, :] = jnp.where(v_head0, o2[:blk], o2[blk:])
            m_tile = jnp.where(lane == 2 * p, m[:blk], m_tile)
            m_tile = jnp.where(lane == 2 * p + 1, m[blk:], m_tile)
            l_tile = jnp.where(lane == 2 * p, l_rep[:blk], l_tile)
            l_tile = jnp.where(lane == 2 * p + 1, l_rep[blk:], l_tile)
        lse_ref[rows, :] = m_tile * LN2 + jnp.log(l_tile)


def _attention_group(qkv, dilation, batch, seq):
    n = seq // dilation
    tq = min(ATTN_TQ, n)
    per = tq // ATTN_BLK
    pairs = ATTN_HEADS // 2

    def cur(which):
        return pl.BlockSpec((None, None, None, tq, D_MODEL), lambda b, r, i: (which, b, r, i, 0))

    def prev(which):
        return pl.BlockSpec((None, None, None, ATTN_BLK, D_MODEL),
                            lambda b, r, i: (which, b, r, jnp.maximum(i * per - 1, 0), 0))

    return pl.pallas_call(
        _attn_kernel,
        out_shape=(jax.ShapeDtypeStruct((batch, dilation, pairs, n, LANES), F32),
                   jax.ShapeDtypeStruct((batch, dilation, n, LANES), F32)),
        grid=(batch, dilation, n // tq),
        in_specs=[cur(0), prev(1), cur(1), prev(2), cur(2)],
        out_specs=(pl.BlockSpec((None, None, pairs, tq, LANES), lambda b, r, i: (b, r, 0, i, 0)),
                   pl.BlockSpec((None, None, tq, LANES), lambda b, r, i: (b, r, i, 0))),
        compiler_params=pltpu.CompilerParams(
            dimension_semantics=("arbitrary", "arbitrary", "arbitrary"),
            vmem_limit_bytes=VMEM_LIMIT),
        name=f"dilated_attn_d{dilation}",
    )(qkv, qkv, qkv, qkv, qkv)


def _class_rows(ref, lead, dilation, r16, n16):
    step = MAX_DILATION // dilation
    if step == 1:
        return ref[(r16, *lead)]
    return ref[(r16 % dilation, *lead, pl.ds(r16 // dilation, n16, stride=step), slice(None))]


def _attn_out_kernel(layer, o0_ref, o1_ref, o2_ref, l0_ref, l1_ref, l2_ref, x_ref, w_hbm, ex_ref,
                     g_ref, b_ref, y_ref, proj_ref, o0s_ref, l0s_ref, w_ref, stage_ref, sem_ref):
    tm = x_ref.shape[0]

    @pl.when(_first_step(2))
    def _():
        _load_plain_weights(w_hbm, (layer,), w_ref, stage_ref, sem_ref)

    n16 = tm // MAX_DILATION
    n_chunks = D_MODEL // LANES
    pairs = ATTN_HEADS // 2
    classes = range(MAX_DILATION)
    mid = DILATED_PATTERNS[1][1]
    assert [d for _, d in DILATED_PATTERNS] == [1, mid, MAX_DILATION]

    for r in range(mid):
        l0s_ref[r, :, :] = l0_ref[0, pl.ds(r, tm // mid, stride=mid), :]
        for p in range(pairs):
            o0s_ref[r, p, :, :] = o0_ref[0, p, pl.ds(r, tm // mid, stride=mid), :]
    o_refs = ((o0s_ref, mid), (o1_ref, mid), (o2_ref, MAX_DILATION))
    l_refs = ((l0s_ref, mid), (l1_ref, mid), (l2_ref, MAX_DILATION))

    lses = [jnp.concatenate([_class_rows(ref, (), d, r, n16) for r in classes], axis=0)
            for ref, d in l_refs]
    mx = jnp.maximum(jnp.maximum(lses[0], lses[1]), lses[2])
    es = [jnp.exp(v - mx) for v in lses]
    inv = 1.0 / (es[0] + es[1] + es[2])
    halves = []
    for e in es:
        w = e * inv
        hi = w.astype(BF16)
        halves.append(jnp.concatenate([hi, (w - hi.astype(F32)).astype(BF16)], axis=1))

    pieces = []
    for pp in range(ATTN_HEADS // 4):
        cols = slice(2 * pp * LANES, (2 * pp + 2) * LANES)
        mixed = jnp.zeros((tm, 2 * LANES), F32)
        for g, (ref, d) in enumerate(o_refs):
            w_wide = jnp.dot(halves[g], ex_ref[:, cols], preferred_element_type=F32)
            o_g = jnp.concatenate(
                [jnp.concatenate([_class_rows(ref, (p,), d, r, n16) for r in classes], axis=0)
                 for p in (2 * pp, 2 * pp + 1)], axis=1)
            mixed = mixed + w_wide * o_g
        pieces.append(mixed.astype(BF16))
    proj = jnp.dot(jnp.concatenate(pieces, axis=1), w_ref[...], preferred_element_type=F32)
    pitch = proj_ref.shape[1] // MAX_DILATION
    for c in range(n_chunks):
        for r in classes:
            proj_ref[c, r * pitch:r * pitch + n16, :] = proj[r * n16:(r + 1) * n16,
                                                             c * LANES:(c + 1) * LANES]
    g = g_ref[...]
    b = b_ref[...]
    for n in range(n16):
        tok = slice(n * MAX_DILATION, (n + 1) * MAX_DILATION)
        y = jnp.concatenate([proj_ref[c, pl.ds(n, MAX_DILATION, stride=pitch), :]
                             for c in range(n_chunks)], axis=1)
        y_ref[tok, :] = _layer_norm(DEEPNORM_ALPHA * x_ref[tok, :] + y, g, b)


def _head_expansion():
    e = (np.arange(D_MODEL)[None, :] // ATTN_HEAD_DIM == np.arange(LANES)[:, None])
    return jnp.asarray(np.concatenate([e, e], axis=0).astype(np.float32), dtype=BF16)


def _attn_out(os_, lses, x2d, w_out, layer, ln_g, ln_b, batch, seq):
    t = x2d.shape[0]
    tm = OUT_TM
    tiles = seq // tm
    pairs = ATTN_HEADS // 2
    dils = [d for _, d in DILATED_PATTERNS]
    o_spec = lambda d: pl.BlockSpec((None, d, pairs, tm // d, LANES), lambda b, i: (b, 0, 0, i, 0))
    l_spec = lambda d: pl.BlockSpec((None, d, tm // d, LANES), lambda b, i: (b, 0, i, 0))
    row = pl.BlockSpec((tm, D_MODEL), lambda b, i: (b * tiles + i, 0))
    return pl.pallas_call(
        functools.partial(_attn_out_kernel, layer),
        out_shape=jax.ShapeDtypeStruct((t, D_MODEL), F32),
        grid=(batch, tiles),
        in_specs=([o_spec(d) for d in dils] + [l_spec(d) for d in dils]
                  + [row, _HBM, _resident((2 * LANES, D_MODEL)),
                     _resident((1, D_MODEL)), _resident((1, D_MODEL))]),
        out_specs=row,
        scratch_shapes=[pltpu.VMEM((D_MODEL // LANES, tm + 8 * MAX_DILATION, LANES), F32),
                        pltpu.VMEM((dils[1], pairs, tm // dils[1], LANES), F32),
                        pltpu.VMEM((dils[1], tm // dils[1], LANES), F32),
                        pltpu.VMEM((D_MODEL, D_MODEL), BF16)] + _weight_scratch(),
        compiler_params=pltpu.CompilerParams(
            dimension_semantics=("arbitrary", "arbitrary"), vmem_limit_bytes=VMEM_LIMIT),
        name="attn_out_ln",
    )(*os_, *lses, x2d, w_out, _head_expansion(), ln_g, ln_b)


def _ffn_kernel(layer, x_ref, wu_hbm, wd_hbm, g_ref, b_ref, y_ref, h_ref, wu_ref, wd_ref,
                stage_ref, sem_ref):
    @pl.when(_first_step(1))
    def _():
        _load_plain_weights(wu_hbm, (layer,), wu_ref, stage_ref, sem_ref)
        _load_plain_weights(wd_hbm, (layer,), wd_ref, stage_ref, sem_ref)

    for t in range(x_ref.shape[0] // FFN_SUB):
        rows = slice(t * FFN_SUB, (t + 1) * FFN_SUB)
        x = x_ref[rows, :]
        xb = x.astype(BF16)
        for c in range(D_FF // D_MODEL):
            sl = slice(c * D_MODEL, (c + 1) * D_MODEL)
            h = jnp.dot(xb, wu_ref[:, sl], preferred_element_type=F32)
            h_ref[rows, sl] = jnp.square(jnp.maximum(h, 0.0)).astype(BF16)
        y = jnp.dot(h_ref[rows, :], wd_ref[...], preferred_element_type=F32)
        y_ref[rows, :] = _layer_norm(DEEPNORM_ALPHA * x + y, g_ref[...], b_ref[...])


def _ffn(x2d, layer, w_up, w_down, ln_g, ln_b):
    t = x2d.shape[0]
    tm = FFN_TM
    return pl.pallas_call(
        functools.partial(_ffn_kernel, layer),
        out_shape=jax.ShapeDtypeStruct((t, D_MODEL), F32),
        grid=(t // tm,),
        in_specs=[pl.BlockSpec((tm, D_MODEL), lambda i: (i, 0)),
                  _HBM, _HBM, _resident((1, D_MODEL)), _resident((1, D_MODEL))],
        out_specs=pl.BlockSpec((tm, D_MODEL), lambda i: (i, 0)),
        scratch_shapes=[pltpu.VMEM((tm, D_FF), BF16), pltpu.VMEM((D_MODEL, D_FF), BF16),
                        pltpu.VMEM((D_FF, D_MODEL), BF16)] + _weight_scratch(),
        compiler_params=pltpu.CompilerParams(
            dimension_semantics=("arbitrary",), vmem_limit_bytes=VMEM_LIMIT),
        name="ffn_ln",
    )(x2d, w_up, w_down, ln_g, ln_b)


def _hgrn_pair_chain(xb, wi_ref, p, lb, tri, block_causal, ng, states, on_ref, out_rows, t, done):
    hk = HGRN_HEADS * HGRN_DK
    dk = HGRN_DK
    c_len = HGRN_CHUNK
    pw = 2 * dk
    rows = xb.shape[0]
    n_chunks = rows // c_len
    cols = slice(p * pw, (p + 1) * pw)
    contract_last = (((1,), (1,)), ((), ()))
    contract_rows = (((0,), (0,)), ((), ()))

    q_raw = jnp.dot(xb, wi_ref[:, p * pw:(p + 1) * pw], preferred_element_type=F32)
    z = jnp.dot(xb, wi_ref[:, hk + p * pw:hk + (p + 1) * pw], preferred_element_type=F32)
    v = jnp.dot(xb, wi_ref[:, 2 * hk + p * pw:2 * hk + (p + 1) * pw], preferred_element_type=F32)
    yield
    lb_p = lb[:, cols]
    key = (1.0 - lb_p) / (1.0 + jnp.exp(z))
    log_f = jnp.log(1.0 - key)
    q = q_raw / (1.0 + jnp.exp(-q_raw))
    v_b = v.astype(BF16)
    hi = log_f.astype(BF16)
    lo = (log_f - hi.astype(F32)).astype(BF16)
    yield
    bcum = (jnp.dot(tri, hi, preferred_element_type=F32)
            + jnp.dot(tri, lo, preferred_element_type=F32))
    yield
    last = [bcum[(c + 1) * c_len - 1:(c + 1) * c_len] for c in range(n_chunks)]
    b_last = jnp.concatenate([jnp.broadcast_to(r, (c_len, pw)) for r in last], axis=0)
    q_dec = (q * jnp.exp(bcum)).astype(BF16)
    k_dec = (key * jnp.exp(-bcum)).astype(BF16)
    k_end = (key * jnp.exp(b_last - bcum)).astype(BF16)
    yield
    scores = [lax.dot_general(q_dec[:, hh * dk:(hh + 1) * dk], k_dec[:, hh * dk:(hh + 1) * dk],
                              contract_last, preferred_element_type=F32) for hh in range(2)]
    yield
    intra = [jnp.dot(jnp.where(block_causal, scores[hh], 0.0).astype(BF16),
                     v_b[:, hh * dk:(hh + 1) * dk], preferred_element_type=F32) for hh in range(2)]
    yield
    zero_st = jnp.zeros((HGRN_DV, dk), BF16)
    zero_k = jnp.zeros((c_len, dk), BF16)
    while t > 0 and (t - 1, p) not in done:
        yield
    st0, st1 = states[2 * p], states[2 * p + 1]
    inter = []
    for c in range(n_chunks):
        rs = slice(c * c_len, (c + 1) * c_len)
        st_pair = jnp.concatenate(
            [jnp.concatenate([st0.astype(BF16), zero_st], axis=1),
             jnp.concatenate([zero_st, st1.astype(BF16)], axis=1)], axis=0)
        inter.append(lax.dot_general(q_dec[rs], st_pair, contract_last,
                                     preferred_element_type=F32))
        v_rows = jnp.concatenate([v_b[rs, :dk], v_b[rs, dk:]], axis=0)
        k_rows = jnp.concatenate(
            [jnp.concatenate([k_end[rs, :dk], zero_k], axis=1),
             jnp.concatenate([zero_k, k_end[rs, dk:]], axis=1)], axis=0)
        kv = lax.dot_general(v_rows, k_rows, contract_rows, preferred_element_type=F32)
        decay = jnp.exp(last[c])
        st0 = decay[:, :dk] * st0 + kv[:, :dk]
        st1 = decay[:, dk:] * st1 + kv[:, dk:]
        if c % 2 == 1:
            yield
    states[2 * p], states[2 * p + 1] = st0, st1
    inter = jnp.concatenate(inter, axis=0)
    outs = []
    for hh in range(2):
        ls = slice(hh * dk, (hh + 1) * dk)
        o = intra[hh] + inter[:, ls]
        o = o * lax.rsqrt(jnp.mean(o * o, axis=-1, keepdims=True) + RMS_EPS) * ng[:, cols][:, ls]
        outs.append(o.astype(BF16))
    on_ref[out_rows, cols] = jnp.concatenate(outs, axis=1)
    done.add((t, p))


def _hgrn_out_chain(x_ref, on_ref, wo_ref, g_ref, b_ref, y_ref, rows, t, done):
    while any((t, p) not in done for p in range(HGRN_HEADS // 2)):
        yield
    y = jnp.dot(on_ref[rows, :], wo_ref[...], preferred_element_type=F32)
    yield
    y_ref[rows, :] = _layer_norm(DEEPNORM_ALPHA * x_ref[rows, :] + y, g_ref[...], b_ref[...])


def _run_staggered(chains, stagger):
    active, pending, tick = [], list(chains), 0
    while active or pending:
        if pending and tick % stagger == 0:
            active.append(pending.pop(0))
        tick += 1
        for gen in list(active):
            try:
                next(gen)
            except StopIteration:
                active.remove(gen)


def _hgrn_kernel(layer, w_layer, sub, x_ref, wi_hbm, wo_hbm, lbl_ref, ng_ref, g_ref, b_ref, y_ref,
                 state_ref, on_ref, wi_ref, wo_ref, stage_ref, sem_ref):
    tm = x_ref.shape[0]
    c_len = HGRN_CHUNK

    @pl.when(_first_step(2))
    def _():
        _load_plain_weights(wi_hbm, (w_layer,), wi_ref, stage_ref, sem_ref)
        _load_plain_weights(wo_hbm, (w_layer,), wo_ref, stage_ref, sem_ref)

    @pl.when(pl.program_id(1) == 0)
    def _():
        state_ref[...] = jnp.zeros_like(state_ref)

    logits = lbl_ref[...]
    ex = jnp.exp(logits - jnp.max(logits, axis=0, keepdims=True))
    sm = ex / jnp.sum(ex, axis=0, keepdims=True)
    lb = jnp.sum(sm[1:layer + 1], axis=0, keepdims=True)

    ri = lax.broadcasted_iota(jnp.int32, (sub, sub), 0)
    ci = lax.broadcasted_iota(jnp.int32, (sub, sub), 1)
    block_causal = (ri // c_len == ci // c_len) & (ci <= ri)
    tri = block_causal.astype(F32).astype(BF16)
    ng = ng_ref[...]
    states = [state_ref[h] for h in range(HGRN_HEADS)]

    chains, done = [], set()
    for t in range(tm // sub):
        rows = slice(t * sub, (t + 1) * sub)
        xb = x_ref[rows, :].astype(BF16)
        for p in range(HGRN_HEADS // 2):
            chains.append(_hgrn_pair_chain(xb, wi_ref, p, lb, tri, block_causal, ng, states,
                                           on_ref, rows, t, done))
        chains.append(_hgrn_out_chain(x_ref, on_ref, wo_ref, g_ref, b_ref, y_ref, rows, t, done))
    _run_staggered(chains, HGRN_STAGGER)
    for h in range(HGRN_HEADS):
        state_ref[h] = states[h]


def _hgrn_mixer(layer, w_layer, x2d, w_in, w_out, lb_logits, norm_g, ln_g, ln_b, batch, seq):
    t = x2d.shape[0]
    tm = HGRN_TM
    tiles = seq // tm
    d_in = w_in.shape[-1]
    row = pl.BlockSpec((tm, D_MODEL), lambda b, i: (b * tiles + i, 0))
    return pl.pallas_call(
        functools.partial(_hgrn_kernel, layer, w_layer, HGRN_SUB),
        out_shape=jax.ShapeDtypeStruct((t, D_MODEL), F32),
        grid=(batch, tiles),
        in_specs=[row, _HBM, _HBM, _resident((DEPTH, D_MODEL)), _resident((1, D_MODEL)),
                  _resident((1, D_MODEL)), _resident((1, D_MODEL))],
        out_specs=row,
        scratch_shapes=[pltpu.VMEM((HGRN_HEADS, HGRN_DV, HGRN_DK), F32),
                        pltpu.VMEM((tm, D_MODEL), BF16), pltpu.VMEM((D_MODEL, d_in), BF16),
                        pltpu.VMEM((D_MODEL, D_MODEL), BF16)] + _weight_scratch(),
        compiler_params=pltpu.CompilerParams(
            dimension_semantics=("arbitrary", "arbitrary"), vmem_limit_bytes=VMEM_LIMIT),
        name="hgrn2_mixer_ln",
    )(x2d, w_in, w_out, lb_logits, norm_g, ln_g, ln_b)


def kernel(x, attn_w_in, attn_w_out, hgrn_w_in, hgrn_w_out, hgrn_norm_g, lb_logits,
           ln_mix_g, ln_mix_b, ln_ffn_g, ln_ffn_b, ffn_w_up, ffn_w_down):
    batch, seq, d = x.shape
    assert d == D_MODEL and lb_logits.shape[0] == DEPTH
    for window, dilation in DILATED_PATTERNS:
        assert window // dilation == ATTN_BLK and seq % window == 0
        assert QKV_TM % (dilation * 16) == 0 and OUT_TM % (dilation * 8) == 0
    assert seq % QKV_TM == 0 and seq % OUT_TM == 0
    assert seq % HGRN_TM == 0 and HGRN_TM % HGRN_SUB == 0 and HGRN_SUB % HGRN_CHUNK == 0
    h = x.reshape(batch * seq, d)
    row = lambda a: a.reshape(1, -1)
    for i in range(DEPTH):
        j = i // 2
        if i % 2 == 0:
            os_, lses = [], []
            for g, (_, dil) in enumerate(DILATED_PATTERNS):
                qkv = _qkv_rope(h, attn_w_in, j, g, dil, batch, seq)
                o, lse = _attention_group(qkv, dil, batch, seq)
                os_.append(o)
                lses.append(lse)
            h = _attn_out(os_, lses, h, attn_w_out, j, row(ln_mix_g[i]), row(ln_mix_b[i]),
                          batch, seq)
        else:
            h = _hgrn_mixer(i, j, h, hgrn_w_in, hgrn_w_out, lb_logits, row(hgrn_norm_g[j]),
                            row(ln_mix_g[i]), row(ln_mix_b[i]), batch, seq)
        h = _ffn(h, i, ffn_w_up, ffn_w_down, row(ln_ffn_g[i]), row(ln_ffn_b[i]))
    return h.reshape(batch, seq, d)
```

```python
import functools
import math

import jax
import jax.numpy as jnp
from jax import lax
from jax.experimental import pallas as pl
from jax.experimental.pallas import tpu as pltpu
import numpy as np

F32 = jnp.float32
BF16 = jnp.bfloat16

D_MODEL = 1024
DEPTH = 2
ATTN_HEAD_DIM = 64
ATTN_HEADS = D_MODEL // ATTN_HEAD_DIM
DILATED_PATTERNS = ((128, 1), (512, 4), (2048, 16))
N_GROUPS = len(DILATED_PATTERNS)
MAX_DILATION = max(d for _, d in DILATED_PATTERNS)
ROPE_THETA = 10000.0
HGRN_HEADS = 8
HGRN_DK = 128
HGRN_DV = 128
HGRN_CHUNK = 64
D_FF = 4 * D_MODEL
LN_EPS = 1e-5
RMS_EPS = 1e-6
DEEPNORM_ALPHA = (2 * DEPTH) ** 0.25

LANES = 128
ATTN_BLK = 128
HALF = ATTN_HEAD_DIM // 2
MASK_VALUE = -1e30
LN2 = math.log(2.0)
Q_SCALE = ATTN_HEAD_DIM ** -0.5 / LN2
VMEM_LIMIT = 56 * 1024 * 1024
MAX_ROW_STRIDE = 4
W_CHUNK_ROWS = 1024
W_CHUNK_COLS = 256
W_SLOTS = 4

QKV_TM = 1024
QKV_SUB = 256
ATTN_TQ = 512
OUT_TM = 512
FFN_TM = 1024
FFN_SUB = 256
HGRN_TM = 512
HGRN_SUB = 256
HGRN_STAGGER = 1


def _layer_norm(y, g, b):
    mu = jnp.mean(y, axis=-1, keepdims=True)
    d = y - mu
    var = jnp.mean(d * d, axis=-1, keepdims=True)
    return d * lax.rsqrt(var + LN_EPS) * g + b


def _resident(shape):
    nd = len(shape)
    return pl.BlockSpec(shape, lambda *_: (0,) * nd, pipeline_mode=pl.Buffered(1))


_HBM = pl.BlockSpec(memory_space=pl.ANY)


def _first_step(grid_rank):
    ids = [pl.program_id(a) == 0 for a in range(grid_rank)]
    return functools.reduce(jnp.logical_and, ids)


def _weight_scratch():
    return [pltpu.VMEM((W_SLOTS, W_CHUNK_ROWS, W_CHUNK_COLS), F32),
            pltpu.SemaphoreType.DMA((W_SLOTS,))]


def _load_weights(chunks, stage_ref, sem_ref, store):
    copies = [pltpu.make_async_copy(src, stage_ref.at[i % W_SLOTS], sem_ref.at[i % W_SLOTS])
              for i, src in enumerate(chunks)]
    ahead = W_SLOTS - 1
    for cp in copies[:ahead]:
        cp.start()
    for i, cp in enumerate(copies):
        if i + ahead < len(copies):
            copies[i + ahead].start()
        cp.wait()
        store(i, stage_ref[i % W_SLOTS])


def _weight_chunks(w_hbm, lead, n_rows, col0, n_cols):
    out = []
    for r in range(0, n_rows, W_CHUNK_ROWS):
        for c in range(0, n_cols, W_CHUNK_COLS):
            view = w_hbm.at[(*lead, pl.ds(r, W_CHUNK_ROWS), pl.ds(col0 + c, W_CHUNK_COLS))]
            out.append((view, r, c))
    return out


def _load_plain_weights(w_hbm, lead, w_ref, stage_ref, sem_ref):
    chunks = _weight_chunks(w_hbm, lead, w_ref.shape[0], 0, w_ref.shape[1])

    def store(i, val):
        _, r, c = chunks[i]
        w_ref[r:r + W_CHUNK_ROWS, c:c + W_CHUNK_COLS] = val.astype(BF16)

    _load_weights([v for v, _, _ in chunks], stage_ref, sem_ref, store)


def _qkv_rope_kernel(dilation, layer, g, x_ref, w_hbm, tab_ref, o_ref, xb_ref, xs_ref, w_ref,
                     stage_ref, sem_ref):
    tm = x_ref.shape[0]
    n_per = tm // dilation
    n_chunks = D_MODEL // LANES

    @pl.when(_first_step(2))
    def _():
        chunks = _weight_chunks(w_hbm, (layer,), D_MODEL, 3 * g * D_MODEL, 3 * D_MODEL)
        lane = lax.broadcasted_iota(jnp.int32, (1, LANES), 1)
        from_right = (lane >= HALF) & (lane < 2 * HALF)
        from_left = (lane >= 2 * HALF) & (lane < 3 * HALF)

        def store(i, val):
            _, _, c = chunks[i]
            if c < 2 * D_MODEL:
                parts = []
                for j in range(W_CHUNK_COLS // LANES):
                    a = val[:, j * LANES:(j + 1) * LANES]
                    parts.append(jnp.where(from_right, pltpu.roll(a, LANES - HALF, 1),
                                           jnp.where(from_left, pltpu.roll(a, HALF, 1), a)))
                val = jnp.concatenate(parts, axis=1)
            w_ref[:, c:c + W_CHUNK_COLS] = val.astype(BF16)

        _load_weights([v for v, _, _ in chunks], stage_ref, sem_ref, store)

    if dilation == 1:
        xb_ref[...] = x_ref[...].astype(BF16)
    else:
        for c in range(n_chunks):
            xs_ref[0, c] = x_ref[:, c * LANES:(c + 1) * LANES]
        passes, left = [], dilation
        while left > 1:
            passes.append(min(left, MAX_ROW_STRIDE))
            left //= passes[-1]
        src, blocks = 0, 1
        for i, st in enumerate(passes):
            rows_blk = tm // blocks
            for blk in range(blocks):
                for r in range(st):
                    lo = (blk + r * blocks) * (rows_blk // st)
                    dst = slice(lo, lo + rows_blk // st)
                    parts = [xs_ref[src, c, pl.ds(blk * rows_blk + r, rows_blk // st, stride=st), :]
                             for c in range(n_chunks)]
                    if i == len(passes) - 1:
                        xb_ref[dst, :] = jnp.concatenate([v.astype(BF16) for v in parts], axis=1)
                    else:
                        for c in range(n_chunks):
                            xs_ref[1 - src, c, dst, :] = parts[c]
            src, blocks = 1 - src, blocks * st

    def store(kind, s, val):
        if n_per >= QKV_SUB:
            start = s * QKV_SUB
            o_ref[kind, start // n_per, start % n_per:start % n_per + QKV_SUB, :] = val
        else:
            per = QKV_SUB // n_per
            for c in range(per):
                o_ref[kind, s * per + c] = val[c * n_per:(c + 1) * n_per]

    for kind in range(3):
        cols = slice(kind * D_MODEL, (kind + 1) * D_MODEL)
        for s in range(tm // QKV_SUB):
            rows = slice(s * QKV_SUB, (s + 1) * QKV_SUB)
            acc = jnp.dot(xb_ref[rows], w_ref[:, cols], preferred_element_type=F32)
            if kind == 2:
                store(kind, s, acc.astype(BF16))
                continue
            cos = tab_ref[0, rows, :]
            sin = tab_ref[1, rows, :]
            if kind == 0:
                cos = cos * Q_SCALE
                sin = sin * Q_SCALE
            pieces = []
            for c in range(n_chunks):
                a = acc[:, c * LANES:(c + 1) * LANES]
                pieces.append((a * cos + pltpu.roll(a, LANES // 2, 1) * sin).astype(BF16))
            store(kind, s, jnp.concatenate(pieces, axis=1))


def _rope_table(seq, dilation, tm):
    inv = ROPE_THETA ** (-np.arange(HALF, dtype=np.float64) * (2.0 / ATTN_HEAD_DIM))
    ang = np.arange(seq, dtype=np.float64)[:, None] * inv[None, :]
    cos = np.tile(np.cos(ang), (1, LANES // HALF))
    sin = np.tile(np.sin(ang), (1, LANES // HALF))
    sign = np.where(np.arange(LANES) < LANES // 2, -1.0, 1.0)
    tab = np.stack([cos, sin * sign])
    tab = tab.reshape(2, seq // tm, tm // dilation, dilation, LANES)
    tab = tab.transpose(0, 1, 3, 2, 4).reshape(2, seq, LANES)
    return jnp.asarray(tab.astype(np.float32))


def _qkv_rope(x2d, w_in, layer, g, dilation, batch, seq):
    tm = QKV_TM
    tiles = seq // tm
    n_per = tm // dilation
    tab = _rope_table(seq, dilation, tm)
    return pl.pallas_call(
        functools.partial(_qkv_rope_kernel, dilation, layer, g),
        out_shape=jax.ShapeDtypeStruct((3, batch, dilation, seq // dilation, D_MODEL), BF16),
        grid=(batch, tiles),
        in_specs=[
            pl.BlockSpec((tm, D_MODEL), lambda b, i: (b * tiles + i, 0)),
            _HBM,
            pl.BlockSpec((2, tm, LANES), lambda b, i: (0, i, 0)),
        ],
        out_specs=pl.BlockSpec((3, None, dilation, n_per, D_MODEL), lambda b, i: (0, b, 0, i, 0)),
        scratch_shapes=[pltpu.VMEM((tm, D_MODEL), BF16),
                        pltpu.VMEM((2, D_MODEL // LANES, tm, LANES), F32),
                        pltpu.VMEM((D_MODEL, 3 * D_MODEL), BF16)] + _weight_scratch(),
        compiler_params=pltpu.CompilerParams(
            dimension_semantics=("arbitrary", "arbitrary"), vmem_limit_bytes=VMEM_LIMIT),
        name=f"qkv_rope_d{dilation}",
    )(x2d, w_in, tab)


def _attn_kernel(q_ref, kp_ref, kc_ref, vp_ref, vc_ref, o_ref, lse_ref):
    i = pl.program_id(2)
    blk = ATTN_BLK
    n_cls, tq = q_ref.shape[0], q_ref.shape[1]
    row = lax.broadcasted_iota(jnp.int32, (2 * blk, 2 * blk), 0) % blk
    col = lax.broadcasted_iota(jnp.int32, (2 * blk, 2 * blk), 1)
    valid = (col >= row) & (col <= row + blk)
    bias = jnp.where(valid, 0.0, MASK_VALUE).astype(F32)
    bias_first = jnp.where(valid & ((col >= blk) | (i > 0)), 0.0, MASK_VALUE).astype(F32)
    lane = lax.broadcasted_iota(jnp.int32, (blk, LANES), 1)
    qk_head0 = ((lane // HALF) % 2 == 0).astype(F32).astype(BF16)
    qk_head1 = ((lane // HALF) % 2 == 1).astype(F32).astype(BF16)
    v_head0 = lane < ATTN_HEAD_DIM
    ones = jnp.ones((2 * blk, LANES), BF16)
    for cls, qb in [(c, b) for c in range(n_cls) for b in range(tq // blk)]:
        rows = slice(qb * blk, (qb + 1) * blk)
        prev_rows = slice((qb - 1) * blk, qb * blk)
        m_tile = jnp.zeros((blk, LANES), F32)
        l_tile = jnp.ones((blk, LANES), F32)
        for p in range(ATTN_HEADS // 2):
            sl = slice(p * LANES, (p + 1) * LANES)
            q = q_ref[cls, rows, sl]
            qs = jnp.concatenate([q * qk_head0, q * qk_head1], axis=0)
            k_prev = kp_ref[cls, :, sl] if qb == 0 else kc_ref[cls, prev_rows, sl]
            v_prev = vp_ref[cls, :, sl] if qb == 0 else vc_ref[cls, prev_rows, sl]
            k = jnp.concatenate([k_prev, kc_ref[cls, rows, sl]], axis=0)
            v = jnp.concatenate([v_prev, vc_ref[cls, rows, sl]], axis=0)
            s = lax.dot_general(qs, k, (((1,), (1,)), ((), ())), preferred_element_type=F32)
            s = s + (bias_first if qb == 0 else bias)
            m = jnp.max(s, axis=-1, keepdims=True)
            e = jnp.exp2(s - m).astype(BF16)
            pv = jnp.dot(e, jnp.concatenate([v, ones], axis=1),
                         preferred_element_type=F32)
            l_rep = pv[:, LANES:]
            o2 = pv[:, :LANES] * (1.0 / l_rep)
            o_ref[cls, p, rows, :] = jnp.where(v_head0, o2[:blk], o2[blk:])
            m_tile = jnp.where(lane == 2 * p, m[:blk], m_tile)
            m_tile = jnp.where(lane == 2 * p + 1, m[blk:], m_tile)
            l_tile = jnp.where(lane == 2 * p, l_rep[:blk], l_tile)
            l_tile = jnp.where(lane == 2 * p + 1, l_rep[blk:], l_tile)
        lse_ref[cls, rows, :] = m_tile * LN2 + jnp.log(l_tile)


def _attention_group(qkv, dilation, batch, seq):
    n = seq // dilation
    tq = min(ATTN_TQ, n)
    n_cls = ATTN_TQ // tq
    per = tq // ATTN_BLK
    pairs = ATTN_HEADS // 2

    def cur(which):
        return pl.BlockSpec((None, None, n_cls, tq, D_MODEL), lambda b, r, i: (which, b, r, i, 0))

    def prev(which):
        return pl.BlockSpec((None, None, n_cls, ATTN_BLK, D_MODEL),
                            lambda b, r, i: (which, b, r, jnp.maximum(i * per - 1, 0), 0))

    return pl.pallas_call(
        _attn_kernel,
        out_shape=(jax.ShapeDtypeStruct((batch, dilation, pairs, n, LANES), F32),
                   jax.ShapeDtypeStruct((batch, dilation, n, LANES), F32)),
        grid=(batch, dilation // n_cls, n // tq),
        in_specs=[cur(0), prev(1), cur(1), prev(2), cur(2)],
        out_specs=(pl.BlockSpec((None, n_cls, pairs, tq, LANES), lambda b, r, i: (b, r, 0, i, 0)),
                   pl.BlockSpec((None, n_cls, tq, LANES), lambda b, r, i: (b, r, i, 0))),
        compiler_params=pltpu.CompilerParams(
            dimension_semantics=("arbitrary", "arbitrary", "arbitrary"),
            vmem_limit_bytes=VMEM_LIMIT),
        name=f"dilated_attn_d{dilation}",
    )(qkv, qkv, qkv, qkv, qkv)


def _class_rows(ref, lead, dilation, r16, n16):
    step = MAX_DILATION // dilation
    if step == 1:
        return ref[(r16, *lead)]
    return ref[(r16 % dilation, *lead, pl.ds(r16 // dilation, n16, stride=step), slice(None))]


def _attn_out_kernel(layer, o0_ref, o1_ref, o2_ref, l0_ref, l1_ref, l2_ref, x_ref, w_hbm, ex_ref,
                     g_ref, b_ref, y_ref, proj_ref, o0s_ref, l0s_ref, w_ref, stage_ref, sem_ref):
    tm = x_ref.shape[0]

    @pl.when(_first_step(2))
    def _():
        _load_plain_weights(w_hbm, (layer,), w_ref, stage_ref, sem_ref)

    n16 = tm // MAX_DILATION
    n_chunks = D_MODEL // LANES
    pairs = ATTN_HEADS // 2
    classes = range(MAX_DILATION)
    mid = DILATED_PATTERNS[1][1]
    assert [d for _, d in DILATED_PATTERNS] == [1, mid, MAX_DILATION]

    for r in range(mid):
        l0s_ref[r, :, :] = l0_ref[0, pl.ds(r, tm // mid, stride=mid), :]
        for p in range(pairs):
            o0s_ref[r, p, :, :] = o0_ref[0, p, pl.ds(r, tm // mid, stride=mid), :]
    o_refs = ((o0s_ref, mid), (o1_ref, mid), (o2_ref, MAX_DILATION))
    l_refs = ((l0s_ref, mid), (l1_ref, mid), (l2_ref, MAX_DILATION))

    lses = [jnp.concatenate([_class_rows(ref, (), d, r, n16) for r in classes], axis=0)
            for ref, d in l_refs]
    mx = jnp.maximum(jnp.maximum(lses[0], lses[1]), lses[2])
    es = [jnp.exp(v - mx) for v in lses]
    inv = 1.0 / (es[0] + es[1] + es[2])
    halves = []
    for e in es:
        w = e * inv
        hi = w.astype(BF16)
        halves.append(jnp.concatenate([hi, (w - hi.astype(F32)).astype(BF16)], axis=1))

    pieces = []
    for pp in range(ATTN_HEADS // 4):
        cols = slice(2 * pp * LANES, (2 * pp + 2) * LANES)
        mixed = jnp.zeros((tm, 2 * LANES), F32)
        for g, (ref, d) in enumerate(o_refs):
            w_wide = jnp.dot(halves[g], ex_ref[:, cols], preferred_element_type=F32)
            o_g = jnp.concatenate(
                [jnp.concatenate([_class_rows(ref, (p,), d, r, n16) for r in classes], axis=0)
                 for p in (2 * pp, 2 * pp + 1)], axis=1)
            mixed = mixed + w_wide * o_g
        pieces.append(mixed.astype(BF16))
    proj = jnp.dot(jnp.concatenate(pieces, axis=1), w_ref[...], preferred_element_type=F32)
    pitch = proj_ref.shape[1] // MAX_DILATION
    for c in range(n_chunks):
        for r in classes:
            proj_ref[c, r * pitch:r * pitch + n16, :] = proj[r * n16:(r + 1) * n16,
                                                             c * LANES:(c + 1) * LANES]
    g = g_ref[...]
    b = b_ref[...]
    for n in range(n16):
        tok = slice(n * MAX_DILATION, (n + 1) * MAX_DILATION)
        y = jnp.concatenate([proj_ref[c, pl.ds(n, MAX_DILATION, stride=pitch), :]
                             for c in range(n_chunks)], axis=1)
        y_ref[tok, :] = _layer_norm(DEEPNORM_ALPHA * x_ref[tok, :] + y, g, b)


def _head_expansion():
    e = (np.arange(D_MODEL)[None, :] // ATTN_HEAD_DIM == np.arange(LANES)[:, None])
    return jnp.asarray(np.concatenate([e, e], axis=0).astype(np.float32), dtype=BF16)


def _attn_out(os_, lses, x2d, w_out, layer, ln_g, ln_b, batch, seq):
    t = x2d.shape[0]
    tm = OUT_TM
    tiles = seq // tm
    pairs = ATTN_HEADS // 2
    dils = [d for _, d in DILATED_PATTERNS]
    o_spec = lambda d: pl.BlockSpec((None, d, pairs, tm // d, LANES), lambda b, i: (b, 0, 0, i, 0))
    l_spec = lambda d: pl.BlockSpec((None, d, tm // d, LANES), lambda b, i: (b, 0, i, 0))
    row = pl.BlockSpec((tm, D_MODEL), lambda b, i: (b * tiles + i, 0))
    return pl.pallas_call(
        functools.partial(_attn_out_kernel, layer),
        out_shape=jax.ShapeDtypeStruct((t, D_MODEL), F32),
        grid=(batch, tiles),
        in_specs=([o_spec(d) for d in dils] + [l_spec(d) for d in dils]
                  + [row, _HBM, _resident((2 * LANES, D_MODEL)),
                     _resident((1, D_MODEL)), _resident((1, D_MODEL))]),
        out_specs=row,
        scratch_shapes=[pltpu.VMEM((D_MODEL // LANES, tm + 8 * MAX_DILATION, LANES), F32),
                        pltpu.VMEM((dils[1], pairs, tm // dils[1], LANES), F32),
                        pltpu.VMEM((dils[1], tm // dils[1], LANES), F32),
                        pltpu.VMEM((D_MODEL, D_MODEL), BF16)] + _weight_scratch(),
        compiler_params=pltpu.CompilerParams(
            dimension_semantics=("arbitrary", "arbitrary"), vmem_limit_bytes=VMEM_LIMIT),
        name="attn_out_ln",
    )(*os_, *lses, x2d, w_out, _head_expansion(), ln_g, ln_b)


def _ffn_kernel(layer, x_ref, wu_hbm, wd_hbm, g_ref, b_ref, y_ref, h_ref, wu_ref, wd_ref,
                stage_ref, sem_ref):
    @pl.when(_first_step(1))
    def _():
        _load_plain_weights(wu_hbm, (layer,), wu_ref, stage_ref, sem_ref)
        _load_plain_weights(wd_hbm, (layer,), wd_ref, stage_ref, sem_ref)

    for t in range(x_ref.shape[0] // FFN_SUB):
        rows = slice(t * FFN_SUB, (t + 1) * FFN_SUB)
        x = x_ref[rows, :]
        xb = x.astype(BF16)
        for c in range(D_FF // D_MODEL):
            sl = slice(c * D_MODEL, (c + 1) * D_MODEL)
            h = jnp.dot(xb, wu_ref[:, sl], preferred_element_type=F32)
            h_ref[rows, sl] = jnp.square(jnp.maximum(h, 0.0)).astype(BF16)
        y = jnp.dot(h_ref[rows, :], wd_ref[...], preferred_element_type=F32)
        y_ref[rows, :] = _layer_norm(DEEPNORM_ALPHA * x + y, g_ref[...], b_ref[...])


def _ffn(x2d, layer, w_up, w_down, ln_g, ln_b):
    t = x2d.shape[0]
    tm = FFN_TM
    return pl.pallas_call(
        functools.partial(_ffn_kernel, layer),
        out_shape=jax.ShapeDtypeStruct((t, D_MODEL), F32),
        grid=(t // tm,),
        in_specs=[pl.BlockSpec((tm, D_MODEL), lambda i: (i, 0)),
                  _HBM, _HBM, _resident((1, D_MODEL)), _resident((1, D_MODEL))],
        out_specs=pl.BlockSpec((tm, D_MODEL), lambda i: (i, 0)),
        scratch_shapes=[pltpu.VMEM((tm, D_FF), BF16), pltpu.VMEM((D_MODEL, D_FF), BF16),
                        pltpu.VMEM((D_FF, D_MODEL), BF16)] + _weight_scratch(),
        compiler_params=pltpu.CompilerParams(
            dimension_semantics=("arbitrary",), vmem_limit_bytes=VMEM_LIMIT),
        name="ffn_ln",
    )(x2d, w_up, w_down, ln_g, ln_b)


def _hgrn_pair_chain(xb, wi_ref, p, lb, tri, block_causal, ng, states, on_ref, out_rows, t, done):
    hk = HGRN_HEADS * HGRN_DK
    dk = HGRN_DK
    c_len = HGRN_CHUNK
    pw = 2 * dk
    rows = xb.shape[0]
    n_chunks = rows // c_len
    cols = slice(p * pw, (p + 1) * pw)
    contract_last = (((1,), (1,)), ((), ()))
    contract_rows = (((0,), (0,)), ((), ()))

    q_raw = jnp.dot(xb, wi_ref[:, p * pw:(p + 1) * pw], preferred_element_type=F32)
    z = jnp.dot(xb, wi_ref[:, hk + p * pw:hk + (p + 1) * pw], preferred_element_type=F32)
    v = jnp.dot(xb, wi_ref[:, 2 * hk + p * pw:2 * hk + (p + 1) * pw], preferred_element_type=F32)
    yield
    lb_p = lb[:, cols]
    key = (1.0 - lb_p) / (1.0 + jnp.exp(z))
    log_f = jnp.log(1.0 - key)
    q = q_raw / (1.0 + jnp.exp(-q_raw))
    v_b = v.astype(BF16)
    hi = log_f.astype(BF16)
    lo = (log_f - hi.astype(F32)).astype(BF16)
    yield
    bcum = (jnp.dot(tri, hi, preferred_element_type=F32)
            + jnp.dot(tri, lo, preferred_element_type=F32))
    yield
    last = [bcum[(c + 1) * c_len - 1:(c + 1) * c_len] for c in range(n_chunks)]
    b_last = jnp.concatenate([jnp.broadcast_to(r, (c_len, pw)) for r in last], axis=0)
    q_dec = (q * jnp.exp(bcum)).astype(BF16)
    k_dec = (key * jnp.exp(-bcum)).astype(BF16)
    k_end = (key * jnp.exp(b_last - bcum)).astype(BF16)
    yield
    scores = [lax.dot_general(q_dec[:, hh * dk:(hh + 1) * dk], k_dec[:, hh * dk:(hh + 1) * dk],
                              contract_last, preferred_element_type=F32) for hh in range(2)]
    yield
    intra = [jnp.dot(jnp.where(block_causal, scores[hh], 0.0).astype(BF16),
                     v_b[:, hh * dk:(hh + 1) * dk], preferred_element_type=F32) for hh in range(2)]
    yield
    zero_st = jnp.zeros((HGRN_DV, dk), BF16)
    zero_k = jnp.zeros((c_len, dk), BF16)
    kvs = []
    for c in range(n_chunks):
        rs = slice(c * c_len, (c + 1) * c_len)
        v_rows = jnp.concatenate([v_b[rs, :dk], v_b[rs, dk:]], axis=0)
        k_rows = jnp.concatenate(
            [jnp.concatenate([k_end[rs, :dk], zero_k], axis=1),
             jnp.concatenate([zero_k, k_end[rs, dk:]], axis=1)], axis=0)
        kvs.append(lax.dot_general(v_rows, k_rows, contract_rows,
                                   preferred_element_type=F32))
        if c % 2 == 1:
            yield
    while t > 0 and (t - 1, p) not in done:
        yield
    st0, st1 = states[2 * p], states[2 * p + 1]
    inter = []
    for c in range(n_chunks):
        rs = slice(c * c_len, (c + 1) * c_len)
        st_pair = jnp.concatenate(
            [jnp.concatenate([st0.astype(BF16), zero_st], axis=1),
             jnp.concatenate([zero_st, st1.astype(BF16)], axis=1)], axis=0)
        inter.append(lax.dot_general(q_dec[rs], st_pair, contract_last,
                                     preferred_element_type=F32))
        decay = jnp.exp(last[c])
        st0 = decay[:, :dk] * st0 + kvs[c][:, :dk]
        st1 = decay[:, dk:] * st1 + kvs[c][:, dk:]
        yield
    states[2 * p], states[2 * p + 1] = st0, st1
    inter = jnp.concatenate(inter, axis=0)
    outs = []
    for hh in range(2):
        ls = slice(hh * dk, (hh + 1) * dk)
        o = intra[hh] + inter[:, ls]
        o = o * lax.rsqrt(jnp.mean(o * o, axis=-1, keepdims=True) + RMS_EPS) * ng[:, cols][:, ls]
        outs.append(o.astype(BF16))
    on_ref[out_rows, cols] = jnp.concatenate(outs, axis=1)
    done.add((t, p))


def _hgrn_out_chain(x_ref, on_ref, wo_ref, g_ref, b_ref, y_ref, rows, t, done):
    while any((t, p) not in done for p in range(HGRN_HEADS // 2)):
        yield
    y = jnp.dot(on_ref[rows, :], wo_ref[...], preferred_element_type=F32)
    yield
    y_ref[rows, :] = _layer_norm(DEEPNORM_ALPHA * x_ref[rows, :] + y, g_ref[...], b_ref[...])


def _run_staggered(chains, stagger):
    active, pending, tick = [], list(chains), 0
    while active or pending:
        if pending and tick % stagger == 0:
            active.append(pending.pop(0))
        tick += 1
        for gen in list(active):
            try:
                next(gen)
            except StopIteration:
                active.remove(gen)


def _hgrn_kernel(layer, w_layer, sub, x_ref, wi_hbm, wo_hbm, lbl_ref, ng_ref, g_ref, b_ref, y_ref,
                 state_ref, on_ref, wi_ref, wo_ref, stage_ref, sem_ref):
    tm = x_ref.shape[0]
    c_len = HGRN_CHUNK

    @pl.when(_first_step(2))
    def _():
        _load_plain_weights(wi_hbm, (w_layer,), wi_ref, stage_ref, sem_ref)
        _load_plain_weights(wo_hbm, (w_layer,), wo_ref, stage_ref, sem_ref)

    @pl.when(pl.program_id(1) == 0)
    def _():
        state_ref[...] = jnp.zeros_like(state_ref)

    logits = lbl_ref[...]
    ex = jnp.exp(logits - jnp.max(logits, axis=0, keepdims=True))
    sm = ex / jnp.sum(ex, axis=0, keepdims=True)
    lb = jnp.sum(sm[1:layer + 1], axis=0, keepdims=True)

    ri = lax.broadcasted_iota(jnp.int32, (sub, sub), 0)
    ci = lax.broadcasted_iota(jnp.int32, (sub, sub), 1)
    block_causal = (ri // c_len == ci // c_len) & (ci <= ri)
    tri = block_causal.astype(F32).astype(BF16)
    ng = ng_ref[...]
    states = [state_ref[h] for h in range(HGRN_HEADS)]

    chains, done = [], set()
    for t in range(tm // sub):
        rows = slice(t * sub, (t + 1) * sub)
        xb = x_ref[rows, :].astype(BF16)
        for p in range(HGRN_HEADS // 2):
            chains.append(_hgrn_pair_chain(xb, wi_ref, p, lb, tri, block_causal, ng, states,
                                           on_ref, rows, t, done))
        chains.append(_hgrn_out_chain(x_ref, on_ref, wo_ref, g_ref, b_ref, y_ref, rows, t, done))
    _run_staggered(chains, HGRN_STAGGER)
    for h in range(HGRN_HEADS):
        state_ref[h] = states[h]


def _hgrn_mixer(layer, w_layer, x2d, w_in, w_out, lb_logits, norm_g, ln_g, ln_b, batch, seq):
    t = x2d.shape[0]
    tm = HGRN_TM
    tiles = seq // tm
    d_in = w_in.shape[-1]
    row = pl.BlockSpec((tm, D_MODEL), lambda b, i: (b * tiles + i, 0))
    return pl.pallas_call(
        functools.partial(_hgrn_kernel, layer, w_layer, HGRN_SUB),
        out_shape=jax.ShapeDtypeStruct((t, D_MODEL), F32),
        grid=(batch, tiles),
        in_specs=[row, _HBM, _HBM, _resident((DEPTH, D_MODEL)), _resident((1, D_MODEL)),
                  _resident((1, D_MODEL)), _resident((1, D_MODEL))],
        out_specs=row,
        scratch_shapes=[pltpu.VMEM((HGRN_HEADS, HGRN_DV, HGRN_DK), F32),
                        pltpu.VMEM((tm, D_MODEL), BF16), pltpu.VMEM((D_MODEL, d_in), BF16),
                        pltpu.VMEM((D_MODEL, D_MODEL), BF16)] + _weight_scratch(),
        compiler_params=pltpu.CompilerParams(
            dimension_semantics=("arbitrary", "arbitrary"), vmem_limit_bytes=VMEM_LIMIT),
        name="hgrn2_mixer_ln",
    )(x2d, w_in, w_out, lb_logits, norm_g, ln_g, ln_b)


def kernel(x, attn_w_in, attn_w_out, hgrn_w_in, hgrn_w_out, hgrn_norm_g, lb_logits,
           ln_mix_g, ln_mix_b, ln_ffn_g, ln_ffn_b, ffn_w_up, ffn_w_down):
    batch, seq, d = x.shape
    assert d == D_MODEL and lb_logits.shape[0] == DEPTH
    for window, dilation in DILATED_PATTERNS:
        assert window // dilation == ATTN_BLK and seq % window == 0
        assert QKV_TM % (dilation * 16) == 0 and OUT_TM % (dilation * 8) == 0
        assert dilation % max(1, ATTN_TQ // (seq // dilation)) == 0
    assert seq % QKV_TM == 0 and seq % OUT_TM == 0
    assert seq % HGRN_TM == 0 and HGRN_TM % HGRN_SUB == 0 and HGRN_SUB % HGRN_CHUNK == 0
    h = x.reshape(batch * seq, d)
    row = lambda a: a.reshape(1, -1)
    for i in range(DEPTH):
        j = i // 2
        if i % 2 == 0:
            os_, lses = [], []
            for g, (_, dil) in enumerate(DILATED_PATTERNS):
                qkv = _qkv_rope(h, attn_w_in, j, g, dil, batch, seq)
                o, lse = _attention_group(qkv, dil, batch, seq)
                os_.append(o)
                lses.append(lse)
            h = _attn_out(os_, lses, h, attn_w_out, j, row(ln_mix_g[i]), row(ln_mix_b[i]),
                          batch, seq)
        else:
            h = _hgrn_mixer(i, j, h, hgrn_w_in, hgrn_w_out, lb_logits, row(hgrn_norm_g[j]),
                            row(ln_mix_g[i]), row(ln_mix_b[i]), batch, seq)
        h = _ffn(h, i, ffn_w_up, ffn_w_down, row(ln_ffn_g[i]), row(ln_ffn_b[i]))
    return h.reshape(batch, seq, d)
```

```python
import functools
import math

import jax
import jax.numpy as jnp
from jax import lax
from jax.experimental import pallas as pl
from jax.experimental.pallas import tpu as pltpu
import numpy as np

F32 = jnp.float32
BF16 = jnp.bfloat16

D_MODEL = 1024
DEPTH = 2
ATTN_HEAD_DIM = 64
ATTN_HEADS = D_MODEL // ATTN_HEAD_DIM
DILATED_PATTERNS = ((128, 1), (512, 4), (2048, 16))
N_GROUPS = len(DILATED_PATTERNS)
MAX_DILATION = max(d for _, d in DILATED_PATTERNS)
ROPE_THETA = 10000.0
HGRN_HEADS = 8
HGRN_DK = 128
HGRN_DV = 128
HGRN_CHUNK = 64
D_FF = 4 * D_MODEL
LN_EPS = 1e-5
RMS_EPS = 1e-6
DEEPNORM_ALPHA = (2 * DEPTH) ** 0.25

LANES = 128
ATTN_BLK = 128
HALF = ATTN_HEAD_DIM // 2
MASK_VALUE = -1e30
LN2 = math.log(2.0)
Q_SCALE = ATTN_HEAD_DIM ** -0.5 / LN2
VMEM_LIMIT = 56 * 1024 * 1024
MAX_ROW_STRIDE = 4
W_CHUNK_ROWS = 1024
W_CHUNK_COLS = 256
W_SLOTS = 4

QKV_TM = 1024
QKV_SUB = 256
ATTN_TQ = 1024
OUT_TM = 512
FFN_TM = 1024
FFN_SUB = 256
HGRN_TM = 512
HGRN_SUB = 256
HGRN_STAGGER = 1


def _layer_norm(y, g, b):
    mu = jnp.mean(y, axis=-1, keepdims=True)
    d = y - mu
    var = jnp.mean(d * d, axis=-1, keepdims=True)
    return d * lax.rsqrt(var + LN_EPS) * g + b


def _resident(shape):
    nd = len(shape)
    return pl.BlockSpec(shape, lambda *_: (0,) * nd, pipeline_mode=pl.Buffered(1))


_HBM = pl.BlockSpec(memory_space=pl.ANY)


def _first_step(grid_rank):
    ids = [pl.program_id(a) == 0 for a in range(grid_rank)]
    return functools.reduce(jnp.logical_and, ids)


def _weight_scratch():
    return [pltpu.VMEM((W_SLOTS, W_CHUNK_ROWS, W_CHUNK_COLS), F32),
            pltpu.SemaphoreType.DMA((W_SLOTS,))]


def _load_weights(chunks, stage_ref, sem_ref, store):
    copies = [pltpu.make_async_copy(src, stage_ref.at[i % W_SLOTS], sem_ref.at[i % W_SLOTS])
              for i, src in enumerate(chunks)]
    ahead = W_SLOTS - 1
    for cp in copies[:ahead]:
        cp.start()
    for i, cp in enumerate(copies):
        if i + ahead < len(copies):
            copies[i + ahead].start()
        cp.wait()
        store(i, stage_ref[i % W_SLOTS])


def _weight_chunks(w_hbm, lead, n_rows, col0, n_cols):
    out = []
    for r in range(0, n_rows, W_CHUNK_ROWS):
        for c in range(0, n_cols, W_CHUNK_COLS):
            view = w_hbm.at[(*lead, pl.ds(r, W_CHUNK_ROWS), pl.ds(col0 + c, W_CHUNK_COLS))]
            out.append((view, r, c))
    return out


def _load_plain_weights(w_hbm, lead, w_ref, stage_ref, sem_ref):
    chunks = _weight_chunks(w_hbm, lead, w_ref.shape[0], 0, w_ref.shape[1])

    def store(i, val):
        _, r, c = chunks[i]
        w_ref[r:r + W_CHUNK_ROWS, c:c + W_CHUNK_COLS] = val.astype(BF16)

    _load_weights([v for v, _, _ in chunks], stage_ref, sem_ref, store)


def _qkv_rope_kernel(dilation, layer, g, x_ref, w_hbm, tab_ref, o_ref, xb_ref, xs_ref, w_ref,
                     stage_ref, sem_ref):
    tm = x_ref.shape[0]
    n_per = tm // dilation
    n_chunks = D_MODEL // LANES

    @pl.when(_first_step(2))
    def _():
        chunks = _weight_chunks(w_hbm, (layer,), D_MODEL, 3 * g * D_MODEL, 3 * D_MODEL)
        lane = lax.broadcasted_iota(jnp.int32, (1, LANES), 1)
        from_right = (lane >= HALF) & (lane < 2 * HALF)
        from_left = (lane >= 2 * HALF) & (lane < 3 * HALF)

        def store(i, val):
            _, _, c = chunks[i]
            if c < 2 * D_MODEL:
                parts = []
                for j in range(W_CHUNK_COLS // LANES):
                    a = val[:, j * LANES:(j + 1) * LANES]
                    parts.append(jnp.where(from_right, pltpu.roll(a, LANES - HALF, 1),
                                           jnp.where(from_left, pltpu.roll(a, HALF, 1), a)))
                val = jnp.concatenate(parts, axis=1)
            w_ref[:, c:c + W_CHUNK_COLS] = val.astype(BF16)

        _load_weights([v for v, _, _ in chunks], stage_ref, sem_ref, store)

    if dilation == 1:
        xb_ref[...] = x_ref[...].astype(BF16)
    else:
        for c in range(n_chunks):
            xs_ref[0, c] = x_ref[:, c * LANES:(c + 1) * LANES]
        passes, left = [], dilation
        while left > 1:
            passes.append(min(left, MAX_ROW_STRIDE))
            left //= passes[-1]
        src, blocks = 0, 1
        for i, st in enumerate(passes):
            rows_blk = tm // blocks
            for blk in range(blocks):
                for r in range(st):
                    lo = (blk + r * blocks) * (rows_blk // st)
                    dst = slice(lo, lo + rows_blk // st)
                    parts = [xs_ref[src, c, pl.ds(blk * rows_blk + r, rows_blk // st, stride=st), :]
                             for c in range(n_chunks)]
                    if i == len(passes) - 1:
                        xb_ref[dst, :] = jnp.concatenate([v.astype(BF16) for v in parts], axis=1)
                    else:
                        for c in range(n_chunks):
                            xs_ref[1 - src, c, dst, :] = parts[c]
            src, blocks = 1 - src, blocks * st

    def store(kind, s, val):
        if n_per >= QKV_SUB:
            start = s * QKV_SUB
            o_ref[kind, start // n_per, start % n_per:start % n_per + QKV_SUB, :] = val
        else:
            per = QKV_SUB // n_per
            for c in range(per):
                o_ref[kind, s * per + c] = val[c * n_per:(c + 1) * n_per]

    for kind in range(3):
        cols = slice(kind * D_MODEL, (kind + 1) * D_MODEL)
        for s in range(tm // QKV_SUB):
            rows = slice(s * QKV_SUB, (s + 1) * QKV_SUB)
            acc = jnp.dot(xb_ref[rows], w_ref[:, cols], preferred_element_type=F32)
            if kind == 2:
                store(kind, s, acc.astype(BF16))
                continue
            cos = tab_ref[0, rows, :]
            sin = tab_ref[1, rows, :]
            if kind == 0:
                cos = cos * Q_SCALE
                sin = sin * Q_SCALE
            pieces = []
            for c in range(n_chunks):
                a = acc[:, c * LANES:(c + 1) * LANES]
                pieces.append((a * cos + pltpu.roll(a, LANES // 2, 1) * sin).astype(BF16))
            store(kind, s, jnp.concatenate(pieces, axis=1))


def _rope_table(seq, dilation, tm):
    inv = ROPE_THETA ** (-np.arange(HALF, dtype=np.float64) * (2.0 / ATTN_HEAD_DIM))
    ang = np.arange(seq, dtype=np.float64)[:, None] * inv[None, :]
    cos = np.tile(np.cos(ang), (1, LANES // HALF))
    sin = np.tile(np.sin(ang), (1, LANES // HALF))
    sign = np.where(np.arange(LANES) < LANES // 2, -1.0, 1.0)
    tab = np.stack([cos, sin * sign])
    tab = tab.reshape(2, seq // tm, tm // dilation, dilation, LANES)
    tab = tab.transpose(0, 1, 3, 2, 4).reshape(2, seq, LANES)
    return jnp.asarray(tab.astype(np.float32))


def _qkv_rope(x2d, w_in, layer, g, dilation, batch, seq):
    tm = QKV_TM
    tiles = seq // tm
    n_per = tm // dilation
    tab = _rope_table(seq, dilation, tm)
    return pl.pallas_call(
        functools.partial(_qkv_rope_kernel, dilation, layer, g),
        out_shape=jax.ShapeDtypeStruct((3, batch, dilation, seq // dilation, D_MODEL), BF16),
        grid=(batch, tiles),
        in_specs=[
            pl.BlockSpec((tm, D_MODEL), lambda b, i: (b * tiles + i, 0)),
            _HBM,
            pl.BlockSpec((2, tm, LANES), lambda b, i: (0, i, 0)),
        ],
        out_specs=pl.BlockSpec((3, None, dilation, n_per, D_MODEL), lambda b, i: (0, b, 0, i, 0)),
        scratch_shapes=[pltpu.VMEM((tm, D_MODEL), BF16),
                        pltpu.VMEM((2, D_MODEL // LANES, tm, LANES), F32),
                        pltpu.VMEM((D_MODEL, 3 * D_MODEL), BF16)] + _weight_scratch(),
        compiler_params=pltpu.CompilerParams(
            dimension_semantics=("arbitrary", "arbitrary"), vmem_limit_bytes=VMEM_LIMIT),
        name=f"qkv_rope_d{dilation}",
    )(x2d, w_in, tab)


def _attn_kernel(q_ref, kp_ref, kc_ref, vp_ref, vc_ref, o_ref, lse_ref):
    i = pl.program_id(2)
    blk = ATTN_BLK
    n_cls, tq = q_ref.shape[0], q_ref.shape[1]
    row = lax.broadcasted_iota(jnp.int32, (2 * blk, 2 * blk), 0) % blk
    col = lax.broadcasted_iota(jnp.int32, (2 * blk, 2 * blk), 1)
    valid = (col >= row) & (col <= row + blk)
    bias = jnp.where(valid, 0.0, MASK_VALUE).astype(F32)
    bias_first = jnp.where(valid & ((col >= blk) | (i > 0)), 0.0, MASK_VALUE).astype(F32)
    lane = lax.broadcasted_iota(jnp.int32, (blk, LANES), 1)
    qk_head0 = ((lane // HALF) % 2 == 0).astype(F32).astype(BF16)
    qk_head1 = ((lane // HALF) % 2 == 1).astype(F32).astype(BF16)
    v_head0 = lane < ATTN_HEAD_DIM
    ones = jnp.ones((2 * blk, LANES), BF16)
    for cls, qb in [(c, b) for c in range(n_cls) for b in range(tq // blk)]:
        rows = slice(qb * blk, (qb + 1) * blk)
        prev_rows = slice((qb - 1) * blk, qb * blk)
        m_tile = jnp.zeros((blk, LANES), F32)
        l_tile = jnp.ones((blk, LANES), F32)
        for p in range(ATTN_HEADS // 2):
            sl = slice(p * LANES, (p + 1) * LANES)
            q = q_ref[cls, rows, sl]
            qs = jnp.concatenate([q * qk_head0, q * qk_head1], axis=0)
            k_prev = kp_ref[cls, :, sl] if qb == 0 else kc_ref[cls, prev_rows, sl]
            v_prev = vp_ref[cls, :, sl] if qb == 0 else vc_ref[cls, prev_rows, sl]
            k = jnp.concatenate([k_prev, kc_ref[cls, rows, sl]], axis=0)
            v = jnp.concatenate([v_prev, vc_ref[cls, rows, sl]], axis=0)
            s = lax.dot_general(qs, k, (((1,), (1,)), ((), ())), preferred_element_type=F32)
            s = s + (bias_first if qb == 0 else bias)
            m = jnp.max(s, axis=-1, keepdims=True)
            e = jnp.exp2(s - m).astype(BF16)
            pv = jnp.dot(e, jnp.concatenate([v, ones], axis=1),
                         preferred_element_type=F32)
            l_rep = pv[:, LANES:]
            o2 = pv[:, :LANES] * (1.0 / l_rep)
            o_ref[cls, p, rows, :] = jnp.where(v_head0, o2[:blk], o2[blk:])
            m_tile = jnp.where(lane == 2 * p, m[:blk], m_tile)
            m_tile = jnp.where(lane == 2 * p + 1, m[blk:], m_tile)
            l_tile = jnp.where(lane == 2 * p, l_rep[:blk], l_tile)
            l_tile = jnp.where(lane == 2 * p + 1, l_rep[blk:], l_tile)
        lse_ref[cls, rows, :] = m_tile * LN2 + jnp.log(l_tile)


def _attention_group(qkv, dilation, batch, seq):
    n = seq // dilation
    tq = min(ATTN_TQ, n)
    n_cls = ATTN_TQ // tq
    per = tq // ATTN_BLK
    pairs = ATTN_HEADS // 2

    def cur(which):
        return pl.BlockSpec((None, None, n_cls, tq, D_MODEL), lambda b, r, i: (which, b, r, i, 0))

    def prev(which):
        return pl.BlockSpec((None, None, n_cls, ATTN_BLK, D_MODEL),
                            lambda b, r, i: (which, b, r, jnp.maximum(i * per - 1, 0), 0))

    return pl.pallas_call(
        _attn_kernel,
        out_shape=(jax.ShapeDtypeStruct((batch, dilation, pairs, n, LANES), F32),
                   jax.ShapeDtypeStruct((batch, dilation, n, LANES), F32)),
        grid=(batch, dilation // n_cls, n // tq),
        in_specs=[cur(0), prev(1), cur(1), prev(2), cur(2)],
        out_specs=(pl.BlockSpec((None, n_cls, pairs, tq, LANES), lambda b, r, i: (b, r, 0, i, 0)),
                   pl.BlockSpec((None, n_cls, tq, LANES), lambda b, r, i: (b, r, i, 0))),
        compiler_params=pltpu.CompilerParams(
            dimension_semantics=("arbitrary", "arbitrary", "arbitrary"),
            vmem_limit_bytes=VMEM_LIMIT),
        name=f"dilated_attn_d{dilation}",
    )(qkv, qkv, qkv, qkv, qkv)


def _class_rows(ref, lead, dilation, r16, n16):
    step = MAX_DILATION // dilation
    if step == 1:
        return ref[(r16, *lead)]
    return ref[(r16 % dilation, *lead, pl.ds(r16 // dilation, n16, stride=step), slice(None))]


def _attn_out_kernel(layer, o0_ref, o1_ref, o2_ref, l0_ref, l1_ref, l2_ref, x_ref, w_hbm, ex_ref,
                     g_ref, b_ref, y_ref, proj_ref, o0s_ref, l0s_ref, w_ref, stage_ref, sem_ref):
    tm = x_ref.shape[0]

    @pl.when(_first_step(2))
    def _():
        _load_plain_weights(w_hbm, (layer,), w_ref, stage_ref, sem_ref)

    n16 = tm // MAX_DILATION
    n_chunks = D_MODEL // LANES
    pairs = ATTN_HEADS // 2
    classes = range(MAX_DILATION)
    mid = DILATED_PATTERNS[1][1]
    assert [d for _, d in DILATED_PATTERNS] == [1, mid, MAX_DILATION]

    for r in range(mid):
        l0s_ref[r, :, :] = l0_ref[0, pl.ds(r, tm // mid, stride=mid), :]
        for p in range(pairs):
            o0s_ref[r, p, :, :] = o0_ref[0, p, pl.ds(r, tm // mid, stride=mid), :]
    o_refs = ((o0s_ref, mid), (o1_ref, mid), (o2_ref, MAX_DILATION))
    l_refs = ((l0s_ref, mid), (l1_ref, mid), (l2_ref, MAX_DILATION))

    lses = [jnp.concatenate([_class_rows(ref, (), d, r, n16) for r in classes], axis=0)
            for ref, d in l_refs]
    mx = jnp.maximum(jnp.maximum(lses[0], lses[1]), lses[2])
    es = [jnp.exp(v - mx) for v in lses]
    inv = 1.0 / (es[0] + es[1] + es[2])
    halves = []
    for e in es:
        w = e * inv
        hi = w.astype(BF16)
        halves.append(jnp.concatenate([hi, (w - hi.astype(F32)).astype(BF16)], axis=1))

    pieces = []
    for pp in range(ATTN_HEADS // 4):
        cols = slice(2 * pp * LANES, (2 * pp + 2) * LANES)
        mixed = jnp.zeros((tm, 2 * LANES), F32)
        for g, (ref, d) in enumerate(o_refs):
            w_wide = jnp.dot(halves[g], ex_ref[:, cols], preferred_element_type=F32)
            o_g = jnp.concatenate(
                [jnp.concatenate([_class_rows(ref, (p,), d, r, n16) for r in classes], axis=0)
                 for p in (2 * pp, 2 * pp + 1)], axis=1)
            mixed = mixed + w_wide * o_g
        pieces.append(mixed.astype(BF16))
    proj = jnp.dot(jnp.concatenate(pieces, axis=1), w_ref[...], preferred_element_type=F32)
    pitch = proj_ref.shape[1] // MAX_DILATION
    for c in range(n_chunks):
        for r in classes:
            proj_ref[c, r * pitch:r * pitch + n16, :] = proj[r * n16:(r + 1) * n16,
                                                             c * LANES:(c + 1) * LANES]
    g = g_ref[...]
    b = b_ref[...]
    for n in range(n16):
        tok = slice(n * MAX_DILATION, (n + 1) * MAX_DILATION)
        y = jnp.concatenate([proj_ref[c, pl.ds(n, MAX_DILATION, stride=pitch), :]
                             for c in range(n_chunks)], axis=1)
        y_ref[tok, :] = _layer_norm(DEEPNORM_ALPHA * x_ref[tok, :] + y, g, b)


def _head_expansion():
    e = (np.arange(D_MODEL)[None, :] // ATTN_HEAD_DIM == np.arange(LANES)[:, None])
    return jnp.asarray(np.concatenate([e, e], axis=0).astype(np.float32), dtype=BF16)


def _attn_out(os_, lses, x2d, w_out, layer, ln_g, ln_b, batch, seq):
    t = x2d.shape[0]
    tm = OUT_TM
    tiles = seq // tm
    pairs = ATTN_HEADS // 2
    dils = [d for _, d in DILATED_PATTERNS]
    o_spec = lambda d: pl.BlockSpec((None, d, pairs, tm // d, LANES), lambda b, i: (b, 0, 0, i, 0))
    l_spec = lambda d: pl.BlockSpec((None, d, tm // d, LANES), lambda b, i: (b, 0, i, 0))
    row = pl.BlockSpec((tm, D_MODEL), lambda b, i: (b * tiles + i, 0))
    return pl.pallas_call(
        functools.partial(_attn_out_kernel, layer),
        out_shape=jax.ShapeDtypeStruct((t, D_MODEL), F32),
        grid=(batch, tiles),
        in_specs=([o_spec(d) for d in dils] + [l_spec(d) for d in dils]
                  + [row, _HBM, _resident((2 * LANES, D_MODEL)),
                     _resident((1, D_MODEL)), _resident((1, D_MODEL))]),
        out_specs=row,
        scratch_shapes=[pltpu.VMEM((D_MODEL // LANES, tm + 8 * MAX_DILATION, LANES), F32),
                        pltpu.VMEM((dils[1], pairs, tm // dils[1], LANES), F32),
                        pltpu.VMEM((dils[1], tm // dils[1], LANES), F32),
                        pltpu.VMEM((D_MODEL, D_MODEL), BF16)] + _weight_scratch(),
        compiler_params=pltpu.CompilerParams(
            dimension_semantics=("arbitrary", "arbitrary"), vmem_limit_bytes=VMEM_LIMIT),
        name="attn_out_ln",
    )(*os_, *lses, x2d, w_out, _head_expansion(), ln_g, ln_b)


def _ffn_kernel(layer, x_ref, wu_hbm, wd_hbm, g_ref, b_ref, y_ref, h_ref, wu_ref, wd_ref,
                stage_ref, sem_ref):
    @pl.when(_first_step(1))
    def _():
        _load_plain_weights(wu_hbm, (layer,), wu_ref, stage_ref, sem_ref)
        _load_plain_weights(wd_hbm, (layer,), wd_ref, stage_ref, sem_ref)

    for t in range(x_ref.shape[0] // FFN_SUB):
        rows = slice(t * FFN_SUB, (t + 1) * FFN_SUB)
        x = x_ref[rows, :]
        xb = x.astype(BF16)
        for c in range(D_FF // D_MODEL):
            sl = slice(c * D_MODEL, (c + 1) * D_MODEL)
            h = jnp.dot(xb, wu_ref[:, sl], preferred_element_type=F32)
            h_ref[rows, sl] = jnp.square(jnp.maximum(h, 0.0)).astype(BF16)
        y = jnp.dot(h_ref[rows, :], wd_ref[...], preferred_element_type=F32)
        y_ref[rows, :] = _layer_norm(DEEPNORM_ALPHA * x + y, g_ref[...], b_ref[...])


def _ffn(x2d, layer, w_up, w_down, ln_g, ln_b):
    t = x2d.shape[0]
    tm = FFN_TM
    return pl.pallas_call(
        functools.partial(_ffn_kernel, layer),
        out_shape=jax.ShapeDtypeStruct((t, D_MODEL), F32),
        grid=(t // tm,),
        in_specs=[pl.BlockSpec((tm, D_MODEL), lambda i: (i, 0)),
                  _HBM, _HBM, _resident((1, D_MODEL)), _resident((1, D_MODEL))],
        out_specs=pl.BlockSpec((tm, D_MODEL), lambda i: (i, 0)),
        scratch_shapes=[pltpu.VMEM((tm, D_FF), BF16), pltpu.VMEM((D_MODEL, D_FF), BF16),
                        pltpu.VMEM((D_FF, D_MODEL), BF16)] + _weight_scratch(),
        compiler_params=pltpu.CompilerParams(
            dimension_semantics=("arbitrary",), vmem_limit_bytes=VMEM_LIMIT),
        name="ffn_ln",
    )(x2d, w_up, w_down, ln_g, ln_b)


def _hgrn_pair_chain(xb, wi_ref, p, lb, tri, block_causal, ng, states, on_ref, out_rows, t, done):
    hk = HGRN_HEADS * HGRN_DK
    dk = HGRN_DK
    c_len = HGRN_CHUNK
    pw = 2 * dk
    rows = xb.shape[0]
    n_chunks = rows // c_len
    cols = slice(p * pw, (p + 1) * pw)
    contract_last = (((1,), (1,)), ((), ()))
    contract_rows = (((0,), (0,)), ((), ()))

    q_raw = jnp.dot(xb, wi_ref[:, p * pw:(p + 1) * pw], preferred_element_type=F32)
    z = jnp.dot(xb, wi_ref[:, hk + p * pw:hk + (p + 1) * pw], preferred_element_type=F32)
    v = jnp.dot(xb, wi_ref[:, 2 * hk + p * pw:2 * hk + (p + 1) * pw], preferred_element_type=F32)
    yield
    lb_p = lb[:, cols]
    key = (1.0 - lb_p) / (1.0 + jnp.exp(z))
    log_f = jnp.log(1.0 - key)
    q = q_raw / (1.0 + jnp.exp(-q_raw))
    v_b = v.astype(BF16)
    hi = log_f.astype(BF16)
    lo = (log_f - hi.astype(F32)).astype(BF16)
    yield
    bcum = (jnp.dot(tri, hi, preferred_element_type=F32)
            + jnp.dot(tri, lo, preferred_element_type=F32))
    yield
    last = [bcum[(c + 1) * c_len - 1:(c + 1) * c_len] for c in range(n_chunks)]
    b_last = jnp.concatenate([jnp.broadcast_to(r, (c_len, pw)) for r in last], axis=0)
    q_dec = (q * jnp.exp(bcum)).astype(BF16)
    k_dec = (key * jnp.exp(-bcum)).astype(BF16)
    k_end = (key * jnp.exp(b_last - bcum)).astype(BF16)
    yield
    scores = [lax.dot_general(q_dec[:, hh * dk:(hh + 1) * dk], k_dec[:, hh * dk:(hh + 1) * dk],
                              contract_last, preferred_element_type=F32) for hh in range(2)]
    yield
    intra = [jnp.dot(jnp.where(block_causal, scores[hh], 0.0).astype(BF16),
                     v_b[:, hh * dk:(hh + 1) * dk], preferred_element_type=F32) for hh in range(2)]
    yield
    zero_st = jnp.zeros((HGRN_DV, dk), BF16)
    zero_k = jnp.zeros((c_len, dk), BF16)
    kvs = []
    for c in range(n_chunks):
        rs = slice(c * c_len, (c + 1) * c_len)
        v_rows = jnp.concatenate([v_b[rs, :dk], v_b[rs, dk:]], axis=0)
        k_rows = jnp.concatenate(
            [jnp.concatenate([k_end[rs, :dk], zero_k], axis=1),
             jnp.concatenate([zero_k, k_end[rs, dk:]], axis=1)], axis=0)
        kvs.append(lax.dot_general(v_rows, k_rows, contract_rows,
                                   preferred_element_type=F32))
        if c % 2 == 1:
            yield
    while t > 0 and (t - 1, p) not in done:
        yield
    st0, st1 = states[2 * p], states[2 * p + 1]
    inter = []
    for c in range(n_chunks):
        rs = slice(c * c_len, (c + 1) * c_len)
        st_pair = jnp.concatenate(
            [jnp.concatenate([st0.astype(BF16), zero_st], axis=1),
             jnp.concatenate([zero_st, st1.astype(BF16)], axis=1)], axis=0)
        inter.append(lax.dot_general(q_dec[rs], st_pair, contract_last,
                                     preferred_element_type=F32))
        decay = jnp.exp(last[c])
        st0 = decay[:, :dk] * st0 + kvs[c][:, :dk]
        st1 = decay[:, dk:] * st1 + kvs[c][:, dk:]
        yield
    states[2 * p], states[2 * p + 1] = st0, st1
    inter = jnp.concatenate(inter, axis=0)
    outs = []
    for hh in range(2):
        ls = slice(hh * dk, (hh + 1) * dk)
        o = intra[hh] + inter[:, ls]
        o = o * lax.rsqrt(jnp.mean(o * o, axis=-1, keepdims=True) + RMS_EPS) * ng[:, cols][:, ls]
        outs.append(o.astype(BF16))
    on_ref[out_rows, cols] = jnp.concatenate(outs, axis=1)
    done.add((t, p))


def _hgrn_out_chain(x_ref, on_ref, wo_ref, g_ref, b_ref, y_ref, rows, t, done):
    while any((t, p) not in done for p in range(HGRN_HEADS // 2)):
        yield
    y = jnp.dot(on_ref[rows, :], wo_ref[...], preferred_element_type=F32)
    yield
    y_ref[rows, :] = _layer_norm(DEEPNORM_ALPHA * x_ref[rows, :] + y, g_ref[...], b_ref[...])


def _run_staggered(chains, stagger):
    active, pending, tick = [], list(chains), 0
    while active or pending:
        if pending and tick % stagger == 0:
            active.append(pending.pop(0))
        tick += 1
        for gen in list(active):
            try:
                next(gen)
            except StopIteration:
                active.remove(gen)


def _hgrn_kernel(layer, w_layer, sub, x_ref, wi_hbm, wo_hbm, lbl_ref, ng_ref, g_ref, b_ref, y_ref,
                 state_ref, on_ref, wi_ref, wo_ref, stage_ref, sem_ref):
    tm = x_ref.shape[0]
    c_len = HGRN_CHUNK

    @pl.when(_first_step(2))
    def _():
        _load_plain_weights(wi_hbm, (w_layer,), wi_ref, stage_ref, sem_ref)
        _load_plain_weights(wo_hbm, (w_layer,), wo_ref, stage_ref, sem_ref)

    @pl.when(pl.program_id(1) == 0)
    def _():
        state_ref[...] = jnp.zeros_like(state_ref)

    logits = lbl_ref[...]
    ex = jnp.exp(logits - jnp.max(logits, axis=0, keepdims=True))
    sm = ex / jnp.sum(ex, axis=0, keepdims=True)
    lb = jnp.sum(sm[1:layer + 1], axis=0, keepdims=True)

    ri = lax.broadcasted_iota(jnp.int32, (sub, sub), 0)
    ci = lax.broadcasted_iota(jnp.int32, (sub, sub), 1)
    block_causal = (ri // c_len == ci // c_len) & (ci <= ri)
    tri = block_causal.astype(F32).astype(BF16)
    ng = ng_ref[...]
    states = [state_ref[h] for h in range(HGRN_HEADS)]

    chains, done = [], set()
    for t in range(tm // sub):
        rows = slice(t * sub, (t + 1) * sub)
        xb = x_ref[rows, :].astype(BF16)
        for p in range(HGRN_HEADS // 2):
            chains.append(_hgrn_pair_chain(xb, wi_ref, p, lb, tri, block_causal, ng, states,
                                           on_ref, rows, t, done))
        chains.append(_hgrn_out_chain(x_ref, on_ref, wo_ref, g_ref, b_ref, y_ref, rows, t, done))
    _run_staggered(chains, HGRN_STAGGER)
    for h in range(HGRN_HEADS):
        state_ref[h] = states[h]


def _hgrn_mixer(layer, w_layer, x2d, w_in, w_out, lb_logits, norm_g, ln_g, ln_b, batch, seq):
    t = x2d.shape[0]
    tm = HGRN_TM
    tiles = seq // tm
    d_in = w_in.shape[-1]
    row = pl.BlockSpec((tm, D_MODEL), lambda b, i: (b * tiles + i, 0))
    return pl.pallas_call(
        functools.partial(_hgrn_kernel, layer, w_layer, HGRN_SUB),
        out_shape=jax.ShapeDtypeStruct((t, D_MODEL), F32),
        grid=(batch, tiles),
        in_specs=[row, _HBM, _HBM, _resident((DEPTH, D_MODEL)), _resident((1, D_MODEL)),
                  _resident((1, D_MODEL)), _resident((1, D_MODEL))],
        out_specs=row,
        scratch_shapes=[pltpu.VMEM((HGRN_HEADS, HGRN_DV, HGRN_DK), F32),
                        pltpu.VMEM((tm, D_MODEL), BF16), pltpu.VMEM((D_MODEL, d_in), BF16),
                        pltpu.VMEM((D_MODEL, D_MODEL), BF16)] + _weight_scratch(),
        compiler_params=pltpu.CompilerParams(
            dimension_semantics=("arbitrary", "arbitrary"), vmem_limit_bytes=VMEM_LIMIT),
        name="hgrn2_mixer_ln",
    )(x2d, w_in, w_out, lb_logits, norm_g, ln_g, ln_b)


def kernel(x, attn_w_in, attn_w_out, hgrn_w_in, hgrn_w_out, hgrn_norm_g, lb_logits,
           ln_mix_g, ln_mix_b, ln_ffn_g, ln_ffn_b, ffn_w_up, ffn_w_down):
    batch, seq, d = x.shape
    assert d == D_MODEL and lb_logits.shape[0] == DEPTH
    for window, dilation in DILATED_PATTERNS:
        assert window // dilation == ATTN_BLK and seq % window == 0
        assert QKV_TM % (dilation * 16) == 0 and OUT_TM % (dilation * 8) == 0
        assert dilation % max(1, ATTN_TQ // (seq // dilation)) == 0
    assert seq % QKV_TM == 0 and seq % OUT_TM == 0
    assert seq % HGRN_TM == 0 and HGRN_TM % HGRN_SUB == 0 and HGRN_SUB % HGRN_CHUNK == 0
    h = x.reshape(batch * seq, d)
    row = lambda a: a.reshape(1, -1)
    for i in range(DEPTH):
        j = i // 2
        if i % 2 == 0:
            os_, lses = [], []
            for g, (_, dil) in enumerate(DILATED_PATTERNS):
                qkv = _qkv_rope(h, attn_w_in, j, g, dil, batch, seq)
                o, lse = _attention_group(qkv, dil, batch, seq)
                os_.append(o)
                lses.append(lse)
            h = _attn_out(os_, lses, h, attn_w_out, j, row(ln_mix_g[i]), row(ln_mix_b[i]),
                          batch, seq)
        else:
            h = _hgrn_mixer(i, j, h, hgrn_w_in, hgrn_w_out, lb_logits, row(hgrn_norm_g[j]),
                            row(ln_mix_g[i]), row(ln_mix_b[i]), batch, seq)
        h = _ffn(h, i, ffn_w_up, ffn_w_down, row(ln_ffn_g[i]), row(ln_ffn_b[i]))
    return h.reshape(batch, seq, d)
```

```python
import functools
import math

import jax
import jax.numpy as jnp
from jax import lax
from jax.experimental import pallas as pl
from jax.experimental.pallas import tpu as pltpu
import numpy as np

F32 = jnp.float32
BF16 = jnp.bfloat16

D_MODEL = 1024
DEPTH = 2
ATTN_HEAD_DIM = 64
ATTN_HEADS = D_MODEL // ATTN_HEAD_DIM
DILATED_PATTERNS = ((128, 1), (512, 4), (2048, 16))
N_GROUPS = len(DILATED_PATTERNS)
MAX_DILATION = max(d for _, d in DILATED_PATTERNS)
ROPE_THETA = 10000.0
HGRN_HEADS = 8
HGRN_DK = 128
HGRN_DV = 128
HGRN_CHUNK = 64
D_FF = 4 * D_MODEL
LN_EPS = 1e-5
RMS_EPS = 1e-6
DEEPNORM_ALPHA = (2 * DEPTH) ** 0.25

LANES = 128
SUBLANES = 8
BF16_ROWS = 2 * SUBLANES
ATTN_BLK = 128
HALF = ATTN_HEAD_DIM // 2
MASK_VALUE = -1e30
LN2 = math.log(2.0)
Q_SCALE = ATTN_HEAD_DIM ** -0.5 / LN2
VMEM_LIMIT = 56 * 1024 * 1024
MAX_ROW_STRIDE = 4
W_CHUNK_ROWS = 1024
W_CHUNK_COLS = 256
W_SLOTS = 4

QKV_TM = 1024
QKV_SUB = 512
ATTN_TQ = 1024
OUT_TM = 512
FFN_TM = 1024
FFN_SUB = 256
HGRN_TM = 512
HGRN_SUB = 256
HGRN_STAGGER = 1


def _layer_norm(y, g, b):
    mu = jnp.mean(y, axis=-1, keepdims=True)
    d = y - mu
    var = jnp.mean(d * d, axis=-1, keepdims=True)
    return d * lax.rsqrt(var + LN_EPS) * g + b


def _resident(shape):
    nd = len(shape)
    return pl.BlockSpec(shape, lambda *_: (0,) * nd, pipeline_mode=pl.Buffered(1))


_HBM = pl.BlockSpec(memory_space=pl.ANY)


def _first_step(grid_rank):
    ids = [pl.program_id(a) == 0 for a in range(grid_rank)]
    return functools.reduce(jnp.logical_and, ids)


def _weight_scratch():
    return [pltpu.VMEM((W_SLOTS, W_CHUNK_ROWS, W_CHUNK_COLS), F32),
            pltpu.SemaphoreType.DMA((W_SLOTS,))]


def _load_weights(chunks, stage_ref, sem_ref, store):
    copies = [pltpu.make_async_copy(src, stage_ref.at[i % W_SLOTS], sem_ref.at[i % W_SLOTS])
              for i, src in enumerate(chunks)]
    ahead = W_SLOTS - 1
    for cp in copies[:ahead]:
        cp.start()
    for i, cp in enumerate(copies):
        if i + ahead < len(copies):
            copies[i + ahead].start()
        cp.wait()
        store(i, stage_ref[i % W_SLOTS])


def _weight_chunks(w_hbm, lead, n_rows, col0, n_cols):
    out = []
    for r in range(0, n_rows, W_CHUNK_ROWS):
        for c in range(0, n_cols, W_CHUNK_COLS):
            view = w_hbm.at[(*lead, pl.ds(r, W_CHUNK_ROWS), pl.ds(col0 + c, W_CHUNK_COLS))]
            out.append((view, r, c))
    return out


def _load_plain_weights(w_hbm, lead, w_ref, stage_ref, sem_ref):
    chunks = _weight_chunks(w_hbm, lead, w_ref.shape[0], 0, w_ref.shape[1])

    def store(i, val):
        _, r, c = chunks[i]
        w_ref[r:r + W_CHUNK_ROWS, c:c + W_CHUNK_COLS] = val.astype(BF16)

    _load_weights([v for v, _, _ in chunks], stage_ref, sem_ref, store)


def _qkv_rope_kernel(dilation, layer, g, x_ref, w_hbm, tab_ref, o_ref, xb_ref, xs_ref, w_ref,
                     stage_ref, sem_ref):
    tm = x_ref.shape[0]
    n_per = tm // dilation
    n_chunks = D_MODEL // LANES

    @pl.when(_first_step(2))
    def _():
        chunks = _weight_chunks(w_hbm, (layer,), D_MODEL, 3 * g * D_MODEL, 3 * D_MODEL)
        lane = lax.broadcasted_iota(jnp.int32, (1, LANES), 1)
        from_right = (lane >= HALF) & (lane < 2 * HALF)
        from_left = (lane >= 2 * HALF) & (lane < 3 * HALF)

        def store(i, val):
            _, _, c = chunks[i]
            if c < 2 * D_MODEL:
                parts = []
                for j in range(W_CHUNK_COLS // LANES):
                    a = val[:, j * LANES:(j + 1) * LANES]
                    parts.append(jnp.where(from_right, pltpu.roll(a, LANES - HALF, 1),
                                           jnp.where(from_left, pltpu.roll(a, HALF, 1), a)))
                val = jnp.concatenate(parts, axis=1)
            w_ref[:, c:c + W_CHUNK_COLS] = val.astype(BF16)

        _load_weights([v for v, _, _ in chunks], stage_ref, sem_ref, store)

    if dilation == 1:
        xb_ref[...] = x_ref[...].astype(BF16)
    else:
        for c in range(n_chunks):
            xs_ref[0, c] = x_ref[:, c * LANES:(c + 1) * LANES]
        passes, left = [], dilation
        while left > 1:
            passes.append(min(left, MAX_ROW_STRIDE))
            left //= passes[-1]
        src, blocks = 0, 1
        for i, st in enumerate(passes):
            rows_blk = tm // blocks
            for blk in range(blocks):
                for r in range(st):
                    lo = (blk + r * blocks) * (rows_blk // st)
                    dst = slice(lo, lo + rows_blk // st)
                    parts = [xs_ref[src, c, pl.ds(blk * rows_blk + r, rows_blk // st, stride=st), :]
                             for c in range(n_chunks)]
                    if i == len(passes) - 1:
                        xb_ref[dst, :] = jnp.concatenate([v.astype(BF16) for v in parts], axis=1)
                    else:
                        for c in range(n_chunks):
                            xs_ref[1 - src, c, dst, :] = parts[c]
            src, blocks = 1 - src, blocks * st

    def store(kind, s, val):
        if n_per >= QKV_SUB:
            start = s * QKV_SUB
            o_ref[kind, start // n_per, start % n_per:start % n_per + QKV_SUB, :] = val
        else:
            per = QKV_SUB // n_per
            for c in range(per):
                o_ref[kind, s * per + c] = val[c * n_per:(c + 1) * n_per]

    for kind in range(3):
        cols = slice(kind * D_MODEL, (kind + 1) * D_MODEL)
        for s in range(tm // QKV_SUB):
            rows = slice(s * QKV_SUB, (s + 1) * QKV_SUB)
            acc = jnp.dot(xb_ref[rows], w_ref[:, cols], preferred_element_type=F32)
            if kind == 2:
                store(kind, s, acc.astype(BF16))
                continue
            cos = tab_ref[0, rows, :]
            sin = tab_ref[1, rows, :]
            if kind == 0:
                cos = cos * Q_SCALE
                sin = sin * Q_SCALE
            pieces = []
            for c in range(n_chunks):
                a = acc[:, c * LANES:(c + 1) * LANES]
                pieces.append((a * cos + pltpu.roll(a, LANES // 2, 1) * sin).astype(BF16))
            store(kind, s, jnp.concatenate(pieces, axis=1))


def _rope_table(seq, dilation, tm):
    inv = ROPE_THETA ** (-np.arange(HALF, dtype=np.float64) * (2.0 / ATTN_HEAD_DIM))
    ang = np.arange(seq, dtype=np.float64)[:, None] * inv[None, :]
    cos = np.tile(np.cos(ang), (1, LANES // HALF))
    sin = np.tile(np.sin(ang), (1, LANES // HALF))
    sign = np.where(np.arange(LANES) < LANES // 2, -1.0, 1.0)
    tab = np.stack([cos, sin * sign])
    tab = tab.reshape(2, seq // tm, tm // dilation, dilation, LANES)
    tab = tab.transpose(0, 1, 3, 2, 4).reshape(2, seq, LANES)
    return jnp.asarray(tab.astype(np.float32))


def _qkv_rope(x2d, w_in, layer, g, dilation, batch, seq):
    tm = QKV_TM
    tiles = seq // tm
    n_per = tm // dilation
    tab = _rope_table(seq, dilation, tm)
    return pl.pallas_call(
        functools.partial(_qkv_rope_kernel, dilation, layer, g),
        out_shape=jax.ShapeDtypeStruct((3, batch, dilation, seq // dilation, D_MODEL), BF16),
        grid=(batch, tiles),
        in_specs=[
            pl.BlockSpec((tm, D_MODEL), lambda b, i: (b * tiles + i, 0)),
            _HBM,
            pl.BlockSpec((2, tm, LANES), lambda b, i: (0, i, 0)),
        ],
        out_specs=pl.BlockSpec((3, None, dilation, n_per, D_MODEL), lambda b, i: (0, b, 0, i, 0)),
        scratch_shapes=[pltpu.VMEM((tm, D_MODEL), BF16),
                        pltpu.VMEM((2, D_MODEL // LANES, tm, LANES), F32),
                        pltpu.VMEM((D_MODEL, 3 * D_MODEL), BF16)] + _weight_scratch(),
        compiler_params=pltpu.CompilerParams(
            dimension_semantics=("arbitrary", "arbitrary"), vmem_limit_bytes=VMEM_LIMIT),
        name=f"qkv_rope_d{dilation}",
    )(x2d, w_in, tab)


def _attn_kernel(q_ref, kp_ref, kc_ref, vp_ref, vc_ref, o_ref, stat_ref):
    i = pl.program_id(2)
    blk = ATTN_BLK
    n_cls, tq = q_ref.shape[0], q_ref.shape[1]
    row = lax.broadcasted_iota(jnp.int32, (2 * blk, 2 * blk), 0) % blk
    col = lax.broadcasted_iota(jnp.int32, (2 * blk, 2 * blk), 1)
    valid = (col >= row) & (col <= row + blk)
    bias = jnp.where(valid, 0.0, MASK_VALUE).astype(F32)
    bias_first = jnp.where(valid & ((col >= blk) | (i > 0)), 0.0, MASK_VALUE).astype(F32)
    lane = lax.broadcasted_iota(jnp.int32, (blk, LANES), 1)
    qk_head0 = ((lane // HALF) % 2 == 0).astype(F32).astype(BF16)
    qk_head1 = ((lane // HALF) % 2 == 1).astype(F32).astype(BF16)
    v_head0 = lane < ATTN_HEAD_DIM
    ones = jnp.ones((2 * blk, LANES), BF16)
    for cls, qb in [(c, b) for c in range(n_cls) for b in range(tq // blk)]:
        rows = slice(qb * blk, (qb + 1) * blk)
        prev_rows = slice((qb - 1) * blk, qb * blk)
        m_tile = jnp.zeros((blk, LANES), F32)
        l_tile = jnp.ones((blk, LANES), F32)
        for p in range(ATTN_HEADS // 2):
            sl = slice(p * LANES, (p + 1) * LANES)
            q = q_ref[cls, rows, sl]
            qs = jnp.concatenate([q * qk_head0, q * qk_head1], axis=0)
            k_prev = kp_ref[cls, :, sl] if qb == 0 else kc_ref[cls, prev_rows, sl]
            v_prev = vp_ref[cls, :, sl] if qb == 0 else vc_ref[cls, prev_rows, sl]
            k = jnp.concatenate([k_prev, kc_ref[cls, rows, sl]], axis=0)
            v = jnp.concatenate([v_prev, vc_ref[cls, rows, sl]], axis=0)
            s = lax.dot_general(qs, k, (((1,), (1,)), ((), ())), preferred_element_type=F32)
            s = s + (bias_first if qb == 0 else bias)
            m = jnp.max(s, axis=-1, keepdims=True)
            e = jnp.exp2(s - m).astype(BF16)
            pv = jnp.dot(e, jnp.concatenate([v, ones], axis=1),
                         preferred_element_type=F32)
            l_rep = pv[:, LANES:]
            o_ref[cls, p, rows, :] = jnp.where(v_head0, pv[:blk, :LANES], pv[blk:, :LANES])
            m_tile = jnp.where(lane == 2 * p, m[:blk], m_tile)
            m_tile = jnp.where(lane == 2 * p + 1, m[blk:], m_tile)
            l_tile = jnp.where(lane == 2 * p, l_rep[:blk], l_tile)
            l_tile = jnp.where(lane == 2 * p + 1, l_rep[blk:], l_tile)
        stat_ref[0, cls, rows, :] = m_tile
        stat_ref[1, cls, rows, :] = l_tile


def _attention_group(qkv, dilation, batch, seq):
    n = seq // dilation
    tq = min(ATTN_TQ, n)
    n_cls = ATTN_TQ // tq
    per = tq // ATTN_BLK
    pairs = ATTN_HEADS // 2

    def cur(which):
        return pl.BlockSpec((None, None, n_cls, tq, D_MODEL), lambda b, r, i: (which, b, r, i, 0))

    def prev(which):
        return pl.BlockSpec((None, None, n_cls, ATTN_BLK, D_MODEL),
                            lambda b, r, i: (which, b, r, jnp.maximum(i * per - 1, 0), 0))

    return pl.pallas_call(
        _attn_kernel,
        out_shape=(jax.ShapeDtypeStruct((batch, dilation, pairs, n, LANES), F32),
                   jax.ShapeDtypeStruct((2, batch, dilation, n, LANES), F32)),
        grid=(batch, dilation // n_cls, n // tq),
        in_specs=[cur(0), prev(1), cur(1), prev(2), cur(2)],
        out_specs=(pl.BlockSpec((None, n_cls, pairs, tq, LANES), lambda b, r, i: (b, r, 0, i, 0)),
                   pl.BlockSpec((2, None, n_cls, tq, LANES), lambda b, r, i: (0, b, r, i, 0))),
        compiler_params=pltpu.CompilerParams(
            dimension_semantics=("arbitrary", "arbitrary", "arbitrary"),
            vmem_limit_bytes=VMEM_LIMIT),
        name=f"dilated_attn_d{dilation}",
    )(qkv, qkv, qkv, qkv, qkv)


def _class_rows(ref, lead, dilation, r16, n16):
    step = MAX_DILATION // dilation
    if step == 1:
        return ref[(r16, *lead)]
    return ref[(r16 % dilation, *lead, pl.ds(r16 // dilation, n16, stride=step), slice(None))]


def _attn_out_kernel(layer, o0_ref, o1_ref, o2_ref, s0_ref, s1_ref, s2_ref, x_ref, w_hbm, ex_ref,
                     g_ref, b_ref, y_ref, proj_ref, o0s_ref, s0s_ref, w_ref, stage_ref, sem_ref):
    tm = x_ref.shape[0]

    @pl.when(_first_step(2))
    def _():
        _load_plain_weights(w_hbm, (layer,), w_ref, stage_ref, sem_ref)

    n16 = tm // MAX_DILATION
    n_chunks = D_MODEL // LANES
    pairs = ATTN_HEADS // 2
    classes = range(MAX_DILATION)
    mid = DILATED_PATTERNS[1][1]
    assert [d for _, d in DILATED_PATTERNS] == [1, mid, MAX_DILATION]

    for r in range(mid):
        for k in range(2):
            s0s_ref[k, r, :, :] = s0_ref[k, 0, pl.ds(r, tm // mid, stride=mid), :]
        for p in range(pairs):
            o0s_ref[r, p, :, :] = o0_ref[0, p, pl.ds(r, tm // mid, stride=mid), :]
    o_refs = ((o0s_ref, mid), (o1_ref, mid), (o2_ref, MAX_DILATION))
    s_refs = ((s0s_ref, mid), (s1_ref, mid), (s2_ref, MAX_DILATION))

    def stat(k):
        return [jnp.concatenate([_class_rows(ref.at[k], (), d, r, n16) for r in classes], axis=0)
                for ref, d in s_refs]

    ms, ls = stat(0), stat(1)
    mx = jnp.maximum(jnp.maximum(ms[0], ms[1]), ms[2])
    es = [jnp.exp2(v - mx) for v in ms]
    inv = 1.0 / (ls[0] * es[0] + ls[1] * es[1] + ls[2] * es[2])
    halves = []
    for e in es:
        w = e * inv
        hi = w.astype(BF16)
        halves.append(jnp.concatenate([hi, (w - hi.astype(F32)).astype(BF16)], axis=1))

    pieces = []
    for pp in range(ATTN_HEADS // 4):
        cols = slice(2 * pp * LANES, (2 * pp + 2) * LANES)
        mixed = jnp.zeros((tm, 2 * LANES), F32)
        for g, (ref, d) in enumerate(o_refs):
            w_wide = jnp.dot(halves[g], ex_ref[:, cols], preferred_element_type=F32)
            o_g = jnp.concatenate(
                [jnp.concatenate([_class_rows(ref, (p,), d, r, n16) for r in classes], axis=0)
                 for p in (2 * pp, 2 * pp + 1)], axis=1)
            mixed = mixed + w_wide * o_g
        pieces.append(mixed.astype(BF16))
    proj = jnp.dot(jnp.concatenate(pieces, axis=1), w_ref[...], preferred_element_type=F32)
    pitch = proj_ref.shape[1] // MAX_DILATION
    for c in range(n_chunks):
        for r in classes:
            proj_ref[c, r * pitch:r * pitch + n16, :] = proj[r * n16:(r + 1) * n16,
                                                             c * LANES:(c + 1) * LANES]
    g = g_ref[...]
    b = b_ref[...]
    for n in range(n16):
        tok = slice(n * MAX_DILATION, (n + 1) * MAX_DILATION)
        y = jnp.concatenate([proj_ref[c, pl.ds(n, MAX_DILATION, stride=pitch), :]
                             for c in range(n_chunks)], axis=1)
        y_ref[tok, :] = _layer_norm(DEEPNORM_ALPHA * x_ref[tok, :] + y, g, b)


def _head_expansion():
    e = (np.arange(D_MODEL)[None, :] // ATTN_HEAD_DIM == np.arange(LANES)[:, None])
    return jnp.asarray(np.concatenate([e, e], axis=0).astype(np.float32), dtype=BF16)


def _attn_out(os_, stats, x2d, w_out, layer, ln_g, ln_b, batch, seq):
    t = x2d.shape[0]
    tm = OUT_TM
    tiles = seq // tm
    pairs = ATTN_HEADS // 2
    dils = [d for _, d in DILATED_PATTERNS]
    o_spec = lambda d: pl.BlockSpec((None, d, pairs, tm // d, LANES), lambda b, i: (b, 0, 0, i, 0))
    s_spec = lambda d: pl.BlockSpec((2, None, d, tm // d, LANES), lambda b, i: (0, b, 0, i, 0))
    row = pl.BlockSpec((tm, D_MODEL), lambda b, i: (b * tiles + i, 0))
    return pl.pallas_call(
        functools.partial(_attn_out_kernel, layer),
        out_shape=jax.ShapeDtypeStruct((t, D_MODEL), F32),
        grid=(batch, tiles),
        in_specs=([o_spec(d) for d in dils] + [s_spec(d) for d in dils]
                  + [row, _HBM, _resident((2 * LANES, D_MODEL)),
                     _resident((1, D_MODEL)), _resident((1, D_MODEL))]),
        out_specs=row,
        scratch_shapes=[pltpu.VMEM((D_MODEL // LANES, tm + SUBLANES * MAX_DILATION, LANES), F32),
                        pltpu.VMEM((dils[1], pairs, tm // dils[1], LANES), F32),
                        pltpu.VMEM((2, dils[1], tm // dils[1], LANES), F32),
                        pltpu.VMEM((D_MODEL, D_MODEL), BF16)] + _weight_scratch(),
        compiler_params=pltpu.CompilerParams(
            dimension_semantics=("arbitrary", "arbitrary"), vmem_limit_bytes=VMEM_LIMIT),
        name="attn_out_ln",
    )(*os_, *stats, x2d, w_out, _head_expansion(), ln_g, ln_b)


def _ffn_kernel(layer, x_ref, wu_hbm, wd_hbm, g_ref, b_ref, y_ref, h_ref, wu_ref, wd_ref,
                stage_ref, sem_ref):
    @pl.when(_first_step(1))
    def _():
        _load_plain_weights(wu_hbm, (layer,), wu_ref, stage_ref, sem_ref)
        _load_plain_weights(wd_hbm, (layer,), wd_ref, stage_ref, sem_ref)

    for t in range(x_ref.shape[0] // FFN_SUB):
        rows = slice(t * FFN_SUB, (t + 1) * FFN_SUB)
        x = x_ref[rows, :]
        xb = x.astype(BF16)
        for c in range(D_FF // D_MODEL):
            sl = slice(c * D_MODEL, (c + 1) * D_MODEL)
            h = jnp.dot(xb, wu_ref[:, sl], preferred_element_type=F32)
            h_ref[rows, sl] = jnp.square(jnp.maximum(h, 0.0)).astype(BF16)
        y = jnp.dot(h_ref[rows, :], wd_ref[...], preferred_element_type=F32)
        y_ref[rows, :] = _layer_norm(DEEPNORM_ALPHA * x + y, g_ref[...], b_ref[...])


def _ffn(x2d, layer, w_up, w_down, ln_g, ln_b):
    t = x2d.shape[0]
    tm = FFN_TM
    return pl.pallas_call(
        functools.partial(_ffn_kernel, layer),
        out_shape=jax.ShapeDtypeStruct((t, D_MODEL), F32),
        grid=(t // tm,),
        in_specs=[pl.BlockSpec((tm, D_MODEL), lambda i: (i, 0)),
                  _HBM, _HBM, _resident((1, D_MODEL)), _resident((1, D_MODEL))],
        out_specs=pl.BlockSpec((tm, D_MODEL), lambda i: (i, 0)),
        scratch_shapes=[pltpu.VMEM((tm, D_FF), BF16), pltpu.VMEM((D_MODEL, D_FF), BF16),
                        pltpu.VMEM((D_FF, D_MODEL), BF16)] + _weight_scratch(),
        compiler_params=pltpu.CompilerParams(
            dimension_semantics=("arbitrary",), vmem_limit_bytes=VMEM_LIMIT),
        name="ffn_ln",
    )(x2d, w_up, w_down, ln_g, ln_b)


def _hgrn_pair_chain(xb, wi_ref, p, lb, tri, block_causal, ng, states, on_ref, out_rows, t, done):
    hk = HGRN_HEADS * HGRN_DK
    dk = HGRN_DK
    c_len = HGRN_CHUNK
    pw = 2 * dk
    rows = xb.shape[0]
    n_chunks = rows // c_len
    cols = slice(p * pw, (p + 1) * pw)
    contract_last = (((1,), (1,)), ((), ()))
    contract_rows = (((0,), (0,)), ((), ()))

    q_raw = jnp.dot(xb, wi_ref[:, p * pw:(p + 1) * pw], preferred_element_type=F32)
    z = jnp.dot(xb, wi_ref[:, hk + p * pw:hk + (p + 1) * pw], preferred_element_type=F32)
    v = jnp.dot(xb, wi_ref[:, 2 * hk + p * pw:2 * hk + (p + 1) * pw], preferred_element_type=F32)
    yield
    lb_p = lb[:, cols]
    key = (1.0 - lb_p) / (1.0 + jnp.exp(z))
    log_f = jnp.log(1.0 - key)
    q = q_raw / (1.0 + jnp.exp(-q_raw))
    v_b = v.astype(BF16)
    hi = log_f.astype(BF16)
    lo = (log_f - hi.astype(F32)).astype(BF16)
    yield
    bcum = (jnp.dot(tri, hi, preferred_element_type=F32)
            + jnp.dot(tri, lo, preferred_element_type=F32))
    yield
    last = [bcum[(c + 1) * c_len - 1:(c + 1) * c_len] for c in range(n_chunks)]
    b_last = jnp.concatenate([jnp.broadcast_to(r, (c_len, pw)) for r in last], axis=0)
    q_dec = (q * jnp.exp(bcum)).astype(BF16)
    k_dec = (key * jnp.exp(-bcum)).astype(BF16)
    k_end = (key * jnp.exp(b_last - bcum)).astype(BF16)
    yield
    scores = [lax.dot_general(q_dec[:, hh * dk:(hh + 1) * dk], k_dec[:, hh * dk:(hh + 1) * dk],
                              contract_last, preferred_element_type=F32) for hh in range(2)]
    yield
    intra = [jnp.dot(jnp.where(block_causal, scores[hh], 0.0).astype(BF16),
                     v_b[:, hh * dk:(hh + 1) * dk], preferred_element_type=F32) for hh in range(2)]
    yield
    zero_st = jnp.zeros((HGRN_DV, dk), BF16)
    zero_k = jnp.zeros((c_len, dk), BF16)
    kvs = []
    for c in range(n_chunks):
        rs = slice(c * c_len, (c + 1) * c_len)
        v_rows = jnp.concatenate([v_b[rs, :dk], v_b[rs, dk:]], axis=0)
        k_rows = jnp.concatenate(
            [jnp.concatenate([k_end[rs, :dk], zero_k], axis=1),
             jnp.concatenate([zero_k, k_end[rs, dk:]], axis=1)], axis=0)
        kvs.append(lax.dot_general(v_rows, k_rows, contract_rows,
                                   preferred_element_type=F32))
        if c % 2 == 1:
            yield
    while t > 0 and (t - 1, p) not in done:
        yield
    st0, st1 = states[2 * p], states[2 * p + 1]
    inter = []
    for c in range(n_chunks):
        rs = slice(c * c_len, (c + 1) * c_len)
        st_pair = jnp.concatenate(
            [jnp.concatenate([st0.astype(BF16), zero_st], axis=1),
             jnp.concatenate([zero_st, st1.astype(BF16)], axis=1)], axis=0)
        inter.append(lax.dot_general(q_dec[rs], st_pair, contract_last,
                                     preferred_element_type=F32))
        decay = jnp.exp(last[c])
        st0 = decay[:, :dk] * st0 + kvs[c][:, :dk]
        st1 = decay[:, dk:] * st1 + kvs[c][:, dk:]
        yield
    states[2 * p], states[2 * p + 1] = st0, st1
    inter = jnp.concatenate(inter, axis=0)
    outs = []
    for hh in range(2):
        ls = slice(hh * dk, (hh + 1) * dk)
        o = intra[hh] + inter[:, ls]
        o = o * lax.rsqrt(jnp.mean(o * o, axis=-1, keepdims=True) + RMS_EPS) * ng[:, cols][:, ls]
        outs.append(o.astype(BF16))
    on_ref[out_rows, cols] = jnp.concatenate(outs, axis=1)
    done.add((t, p))


def _hgrn_out_chain(x_ref, on_ref, wo_ref, g_ref, b_ref, y_ref, rows, t, done):
    while any((t, p) not in done for p in range(HGRN_HEADS // 2)):
        yield
    y = jnp.dot(on_ref[rows, :], wo_ref[...], preferred_element_type=F32)
    yield
    y_ref[rows, :] = _layer_norm(DEEPNORM_ALPHA * x_ref[rows, :] + y, g_ref[...], b_ref[...])


def _run_staggered(chains, stagger):
    active, pending, tick = [], list(chains), 0
    while active or pending:
        if pending and tick % stagger == 0:
            active.append(pending.pop(0))
        tick += 1
        for gen in list(active):
            try:
                next(gen)
            except StopIteration:
                active.remove(gen)


def _hgrn_kernel(layer, w_layer, sub, x_ref, wi_hbm, wo_hbm, lbl_ref, ng_ref, g_ref, b_ref, y_ref,
                 state_ref, on_ref, wi_ref, wo_ref, stage_ref, sem_ref):
    tm = x_ref.shape[0]
    c_len = HGRN_CHUNK

    @pl.when(_first_step(2))
    def _():
        _load_plain_weights(wi_hbm, (w_layer,), wi_ref, stage_ref, sem_ref)
        _load_plain_weights(wo_hbm, (w_layer,), wo_ref, stage_ref, sem_ref)

    @pl.when(pl.program_id(1) == 0)
    def _():
        state_ref[...] = jnp.zeros_like(state_ref)

    logits = lbl_ref[...]
    ex = jnp.exp(logits - jnp.max(logits, axis=0, keepdims=True))
    sm = ex / jnp.sum(ex, axis=0, keepdims=True)
    lb = jnp.sum(sm[1:layer + 1], axis=0, keepdims=True)

    ri = lax.broadcasted_iota(jnp.int32, (sub, sub), 0)
    ci = lax.broadcasted_iota(jnp.int32, (sub, sub), 1)
    block_causal = (ri // c_len == ci // c_len) & (ci <= ri)
    tri = block_causal.astype(F32).astype(BF16)
    ng = ng_ref[...]
    states = [state_ref[h] for h in range(HGRN_HEADS)]

    chains, done = [], set()
    for t in range(tm // sub):
        rows = slice(t * sub, (t + 1) * sub)
        xb = x_ref[rows, :].astype(BF16)
        for p in range(HGRN_HEADS // 2):
            chains.append(_hgrn_pair_chain(xb, wi_ref, p, lb, tri, block_causal, ng, states,
                                           on_ref, rows, t, done))
        chains.append(_hgrn_out_chain(x_ref, on_ref, wo_ref, g_ref, b_ref, y_ref, rows, t, done))
    _run_staggered(chains, HGRN_STAGGER)
    for h in range(HGRN_HEADS):
        state_ref[h] = states[h]


def _hgrn_mixer(layer, w_layer, x2d, w_in, w_out, lb_logits, norm_g, ln_g, ln_b, batch, seq):
    t = x2d.shape[0]
    tm = HGRN_TM
    tiles = seq // tm
    d_in = w_in.shape[-1]
    row = pl.BlockSpec((tm, D_MODEL), lambda b, i: (b * tiles + i, 0))
    return pl.pallas_call(
        functools.partial(_hgrn_kernel, layer, w_layer, HGRN_SUB),
        out_shape=jax.ShapeDtypeStruct((t, D_MODEL), F32),
        grid=(batch, tiles),
        in_specs=[row, _HBM, _HBM, _resident((DEPTH, D_MODEL)), _resident((1, D_MODEL)),
                  _resident((1, D_MODEL)), _resident((1, D_MODEL))],
        out_specs=row,
        scratch_shapes=[pltpu.VMEM((HGRN_HEADS, HGRN_DV, HGRN_DK), F32),
                        pltpu.VMEM((tm, D_MODEL), BF16), pltpu.VMEM((D_MODEL, d_in), BF16),
                        pltpu.VMEM((D_MODEL, D_MODEL), BF16)] + _weight_scratch(),
        compiler_params=pltpu.CompilerParams(
            dimension_semantics=("arbitrary", "arbitrary"), vmem_limit_bytes=VMEM_LIMIT),
        name="hgrn2_mixer_ln",
    )(x2d, w_in, w_out, lb_logits, norm_g, ln_g, ln_b)


def kernel(x, attn_w_in, attn_w_out, hgrn_w_in, hgrn_w_out, hgrn_norm_g, lb_logits,
           ln_mix_g, ln_mix_b, ln_ffn_g, ln_ffn_b, ffn_w_up, ffn_w_down):
    batch, seq, d = x.shape
    assert d == D_MODEL and lb_logits.shape[0] == DEPTH
    for window, dilation in DILATED_PATTERNS:
        assert window // dilation == ATTN_BLK and seq % window == 0
        assert QKV_TM % (dilation * BF16_ROWS) == 0 and OUT_TM % (dilation * SUBLANES) == 0
        assert dilation % max(1, ATTN_TQ // (seq // dilation)) == 0
    assert seq % QKV_TM == 0 and seq % OUT_TM == 0
    assert seq % HGRN_TM == 0 and HGRN_TM % HGRN_SUB == 0 and HGRN_SUB % HGRN_CHUNK == 0
    h = x.reshape(batch * seq, d)
    row = lambda a: a.reshape(1, -1)
    for i in range(DEPTH):
        j = i // 2
        if i % 2 == 0:
            os_, stats = [], []
            for g, (_, dil) in enumerate(DILATED_PATTERNS):
                qkv = _qkv_rope(h, attn_w_in, j, g, dil, batch, seq)
                o, st = _attention_group(qkv, dil, batch, seq)
                os_.append(o)
                stats.append(st)
            h = _attn_out(os_, stats, h, attn_w_out, j, row(ln_mix_g[i]), row(ln_mix_b[i]),
                          batch, seq)
        else:
            h = _hgrn_mixer(i, j, h, hgrn_w_in, hgrn_w_out, lb_logits, row(hgrn_norm_g[j]),
                            row(ln_mix_g[i]), row(ln_mix_b[i]), batch, seq)
        h = _ffn(h, i, ffn_w_up, ffn_w_down, row(ln_ffn_g[i]), row(ln_ffn_b[i]))
    return h.reshape(batch, seq, d)
```

```python
import functools
import math

import jax
import jax.numpy as jnp
from jax import lax
from jax.experimental import pallas as pl
from jax.experimental.pallas import tpu as pltpu
import numpy as np

F32 = jnp.float32
BF16 = jnp.bfloat16

D_MODEL = 1024
DEPTH = 2
ATTN_HEAD_DIM = 64
ATTN_HEADS = D_MODEL // ATTN_HEAD_DIM
DILATED_PATTERNS = ((128, 1), (512, 4), (2048, 16))
N_GROUPS = len(DILATED_PATTERNS)
MAX_DILATION = max(d for _, d in DILATED_PATTERNS)
ROPE_THETA = 10000.0
HGRN_HEADS = 8
HGRN_DK = 128
HGRN_DV = 128
HGRN_CHUNK = 64
D_FF = 4 * D_MODEL
LN_EPS = 1e-5
RMS_EPS = 1e-6
DEEPNORM_ALPHA = (2 * DEPTH) ** 0.25

LANES = 128
SUBLANES = 8
BF16_ROWS = 2 * SUBLANES
ATTN_BLK = 128
HALF = ATTN_HEAD_DIM // 2
MASK_VALUE = -1e30
LN2 = math.log(2.0)
Q_SCALE = ATTN_HEAD_DIM ** -0.5 / LN2
VMEM_LIMIT = 56 * 1024 * 1024
MAX_ROW_STRIDE = 4
W_CHUNK_ROWS = 1024
W_CHUNK_COLS = 256
W_SLOTS = 4

QKV_TM = 1024
QKV_SUB = 512
ATTN_TQ = 2048
OUT_TM = 512
FFN_TM = 1024
FFN_SUB = 256
HGRN_TM = 512
HGRN_SUB = 256
HGRN_STAGGER = 1


def _layer_norm(y, g, b):
    mu = jnp.mean(y, axis=-1, keepdims=True)
    d = y - mu
    var = jnp.mean(d * d, axis=-1, keepdims=True)
    return d * lax.rsqrt(var + LN_EPS) * g + b


def _resident(shape):
    nd = len(shape)
    return pl.BlockSpec(shape, lambda *_: (0,) * nd, pipeline_mode=pl.Buffered(1))


_HBM = pl.BlockSpec(memory_space=pl.ANY)


def _first_step(grid_rank):
    ids = [pl.program_id(a) == 0 for a in range(grid_rank)]
    return functools.reduce(jnp.logical_and, ids)


def _weight_scratch():
    return [pltpu.VMEM((W_SLOTS, W_CHUNK_ROWS, W_CHUNK_COLS), F32),
            pltpu.SemaphoreType.DMA((W_SLOTS,))]


def _load_weights(chunks, stage_ref, sem_ref, store):
    copies = [pltpu.make_async_copy(src, stage_ref.at[i % W_SLOTS], sem_ref.at[i % W_SLOTS])
              for i, src in enumerate(chunks)]
    ahead = W_SLOTS - 1
    for cp in copies[:ahead]:
        cp.start()
    for i, cp in enumerate(copies):
        if i + ahead < len(copies):
            copies[i + ahead].start()
        cp.wait()
        store(i, stage_ref[i % W_SLOTS])


def _weight_chunks(w_hbm, lead, n_rows, col0, n_cols):
    out = []
    for r in range(0, n_rows, W_CHUNK_ROWS):
        for c in range(0, n_cols, W_CHUNK_COLS):
            view = w_hbm.at[(*lead, pl.ds(r, W_CHUNK_ROWS), pl.ds(col0 + c, W_CHUNK_COLS))]
            out.append((view, r, c))
    return out


def _load_plain_weights(w_hbm, lead, w_ref, stage_ref, sem_ref):
    chunks = _weight_chunks(w_hbm, lead, w_ref.shape[0], 0, w_ref.shape[1])

    def store(i, val):
        _, r, c = chunks[i]
        w_ref[r:r + W_CHUNK_ROWS, c:c + W_CHUNK_COLS] = val.astype(BF16)

    _load_weights([v for v, _, _ in chunks], stage_ref, sem_ref, store)


def _qkv_rope_kernel(dilation, layer, g, x_ref, w_hbm, tab_ref, o_ref, xb_ref, xs_ref, w_ref,
                     stage_ref, sem_ref):
    tm = x_ref.shape[0]
    n_per = tm // dilation
    n_chunks = D_MODEL // LANES

    @pl.when(_first_step(2))
    def _():
        chunks = _weight_chunks(w_hbm, (layer,), D_MODEL, 3 * g * D_MODEL, 3 * D_MODEL)
        lane = lax.broadcasted_iota(jnp.int32, (1, LANES), 1)
        from_right = (lane >= HALF) & (lane < 2 * HALF)
        from_left = (lane >= 2 * HALF) & (lane < 3 * HALF)

        def store(i, val):
            _, _, c = chunks[i]
            if c < 2 * D_MODEL:
                parts = []
                for j in range(W_CHUNK_COLS // LANES):
                    a = val[:, j * LANES:(j + 1) * LANES]
                    parts.append(jnp.where(from_right, pltpu.roll(a, LANES - HALF, 1),
                                           jnp.where(from_left, pltpu.roll(a, HALF, 1), a)))
                val = jnp.concatenate(parts, axis=1)
            w_ref[:, c:c + W_CHUNK_COLS] = val.astype(BF16)

        _load_weights([v for v, _, _ in chunks], stage_ref, sem_ref, store)

    if dilation == 1:
        xb_ref[...] = x_ref[...].astype(BF16)
    else:
        for c in range(n_chunks):
            xs_ref[0, c] = x_ref[:, c * LANES:(c + 1) * LANES]
        passes, left = [], dilation
        while left > 1:
            passes.append(min(left, MAX_ROW_STRIDE))
            left //= passes[-1]
        src, blocks = 0, 1
        for i, st in enumerate(passes):
            rows_blk = tm // blocks
            for blk in range(blocks):
                for r in range(st):
                    lo = (blk + r * blocks) * (rows_blk // st)
                    dst = slice(lo, lo + rows_blk // st)
                    parts = [xs_ref[src, c, pl.ds(blk * rows_blk + r, rows_blk // st, stride=st), :]
                             for c in range(n_chunks)]
                    if i == len(passes) - 1:
                        xb_ref[dst, :] = jnp.concatenate([v.astype(BF16) for v in parts], axis=1)
                    else:
                        for c in range(n_chunks):
                            xs_ref[1 - src, c, dst, :] = parts[c]
            src, blocks = 1 - src, blocks * st

    def store(kind, s, val):
        if n_per >= QKV_SUB:
            start = s * QKV_SUB
            o_ref[kind, start // n_per, start % n_per:start % n_per + QKV_SUB, :] = val
        else:
            per = QKV_SUB // n_per
            for c in range(per):
                o_ref[kind, s * per + c] = val[c * n_per:(c + 1) * n_per]

    for kind in range(3):
        cols = slice(kind * D_MODEL, (kind + 1) * D_MODEL)
        for s in range(tm // QKV_SUB):
            rows = slice(s * QKV_SUB, (s + 1) * QKV_SUB)
            acc = jnp.dot(xb_ref[rows], w_ref[:, cols], preferred_element_type=F32)
            if kind == 2:
                store(kind, s, acc.astype(BF16))
                continue
            cos = tab_ref[0, rows, :]
            sin = tab_ref[1, rows, :]
            if kind == 0:
                cos = cos * Q_SCALE
                sin = sin * Q_SCALE
            pieces = []
            for c in range(n_chunks):
                a = acc[:, c * LANES:(c + 1) * LANES]
                pieces.append((a * cos + pltpu.roll(a, LANES // 2, 1) * sin).astype(BF16))
            store(kind, s, jnp.concatenate(pieces, axis=1))


def _rope_table(seq, dilation, tm):
    inv = ROPE_THETA ** (-np.arange(HALF, dtype=np.float64) * (2.0 / ATTN_HEAD_DIM))
    ang = np.arange(seq, dtype=np.float64)[:, None] * inv[None, :]
    cos = np.tile(np.cos(ang), (1, LANES // HALF))
    sin = np.tile(np.sin(ang), (1, LANES // HALF))
    sign = np.where(np.arange(LANES) < LANES // 2, -1.0, 1.0)
    tab = np.stack([cos, sin * sign])
    tab = tab.reshape(2, seq // tm, tm // dilation, dilation, LANES)
    tab = tab.transpose(0, 1, 3, 2, 4).reshape(2, seq, LANES)
    return jnp.asarray(tab.astype(np.float32))


def _qkv_rope(x2d, w_in, layer, g, dilation, batch, seq):
    tm = QKV_TM
    tiles = seq // tm
    n_per = tm // dilation
    tab = _rope_table(seq, dilation, tm)
    return pl.pallas_call(
        functools.partial(_qkv_rope_kernel, dilation, layer, g),
        out_shape=jax.ShapeDtypeStruct((3, batch, dilation, seq // dilation, D_MODEL), BF16),
        grid=(batch, tiles),
        in_specs=[
            pl.BlockSpec((tm, D_MODEL), lambda b, i: (b * tiles + i, 0)),
            _HBM,
            pl.BlockSpec((2, tm, LANES), lambda b, i: (0, i, 0)),
        ],
        out_specs=pl.BlockSpec((3, None, dilation, n_per, D_MODEL), lambda b, i: (0, b, 0, i, 0)),
        scratch_shapes=[pltpu.VMEM((tm, D_MODEL), BF16),
                        pltpu.VMEM((2, D_MODEL // LANES, tm, LANES), F32),
                        pltpu.VMEM((D_MODEL, 3 * D_MODEL), BF16)] + _weight_scratch(),
        compiler_params=pltpu.CompilerParams(
            dimension_semantics=("arbitrary", "arbitrary"), vmem_limit_bytes=VMEM_LIMIT),
        name=f"qkv_rope_d{dilation}",
    )(x2d, w_in, tab)


def _attn_kernel(q_ref, kp_ref, kc_ref, vp_ref, vc_ref, o_ref, stat_ref):
    i = pl.program_id(2)
    blk = ATTN_BLK
    n_cls, tq = q_ref.shape[0], q_ref.shape[1]
    row = lax.broadcasted_iota(jnp.int32, (2 * blk, 2 * blk), 0) % blk
    col = lax.broadcasted_iota(jnp.int32, (2 * blk, 2 * blk), 1)
    valid = (col >= row) & (col <= row + blk)
    bias = jnp.where(valid, 0.0, MASK_VALUE).astype(F32)
    bias_first = jnp.where(valid & ((col >= blk) | (i > 0)), 0.0, MASK_VALUE).astype(F32)
    lane = lax.broadcasted_iota(jnp.int32, (blk, LANES), 1)
    qk_head0 = ((lane // HALF) % 2 == 0).astype(F32).astype(BF16)
    qk_head1 = ((lane // HALF) % 2 == 1).astype(F32).astype(BF16)
    v_head0 = lane < ATTN_HEAD_DIM
    ones = jnp.ones((2 * blk, LANES), BF16)
    for cls, qb in [(c, b) for c in range(n_cls) for b in range(tq // blk)]:
        rows = slice(qb * blk, (qb + 1) * blk)
        prev_rows = slice((qb - 1) * blk, qb * blk)
        m_tile = jnp.zeros((blk, LANES), F32)
        l_tile = jnp.ones((blk, LANES), F32)
        for p in range(ATTN_HEADS // 2):
            sl = slice(p * LANES, (p + 1) * LANES)
            q = q_ref[cls, rows, sl]
            qs = jnp.concatenate([q * qk_head0, q * qk_head1], axis=0)
            k_prev = kp_ref[cls, :, sl] if qb == 0 else kc_ref[cls, prev_rows, sl]
            v_prev = vp_ref[cls, :, sl] if qb == 0 else vc_ref[cls, prev_rows, sl]
            k = jnp.concatenate([k_prev, kc_ref[cls, rows, sl]], axis=0)
            v = jnp.concatenate([v_prev, vc_ref[cls, rows, sl]], axis=0)
            s = lax.dot_general(qs, k, (((1,), (1,)), ((), ())), preferred_element_type=F32)
            s = s + (bias_first if qb == 0 else bias)
            m = jnp.max(s, axis=-1, keepdims=True)
            e = jnp.exp2(s - m).astype(BF16)
            pv = jnp.dot(e, jnp.concatenate([v, ones], axis=1),
                         preferred_element_type=F32)
            l_rep = pv[:, LANES:]
            o_ref[cls, p, rows, :] = jnp.where(v_head0, pv[:blk, :LANES], pv[blk:, :LANES])
            m_tile = jnp.where(lane == 2 * p, m[:blk], m_tile)
            m_tile = jnp.where(lane == 2 * p + 1, m[blk:], m_tile)
            l_tile = jnp.where(lane == 2 * p, l_rep[:blk], l_tile)
            l_tile = jnp.where(lane == 2 * p + 1, l_rep[blk:], l_tile)
        stat_ref[0, cls, rows, :] = m_tile
        stat_ref[1, cls, rows, :] = l_tile


def _attention_group(qkv, dilation, batch, seq):
    n = seq // dilation
    tq = min(ATTN_TQ, n)
    n_cls = ATTN_TQ // tq
    per = tq // ATTN_BLK
    pairs = ATTN_HEADS // 2

    def cur(which):
        return pl.BlockSpec((None, None, n_cls, tq, D_MODEL), lambda b, r, i: (which, b, r, i, 0))

    def prev(which):
        return pl.BlockSpec((None, None, n_cls, ATTN_BLK, D_MODEL),
                            lambda b, r, i: (which, b, r, jnp.maximum(i * per - 1, 0), 0))

    return pl.pallas_call(
        _attn_kernel,
        out_shape=(jax.ShapeDtypeStruct((batch, dilation, pairs, n, LANES), F32),
                   jax.ShapeDtypeStruct((2, batch, dilation, n, LANES), F32)),
        grid=(batch, dilation // n_cls, n // tq),
        in_specs=[cur(0), prev(1), cur(1), prev(2), cur(2)],
        out_specs=(pl.BlockSpec((None, n_cls, pairs, tq, LANES), lambda b, r, i: (b, r, 0, i, 0)),
                   pl.BlockSpec((2, None, n_cls, tq, LANES), lambda b, r, i: (0, b, r, i, 0))),
        compiler_params=pltpu.CompilerParams(
            dimension_semantics=("arbitrary", "arbitrary", "arbitrary"),
            vmem_limit_bytes=VMEM_LIMIT),
        name=f"dilated_attn_d{dilation}",
    )(qkv, qkv, qkv, qkv, qkv)


def _class_rows(ref, lead, dilation, r16, n16):
    step = MAX_DILATION // dilation
    if step == 1:
        return ref[(r16, *lead)]
    return ref[(r16 % dilation, *lead, pl.ds(r16 // dilation, n16, stride=step), slice(None))]


def _attn_out_kernel(layer, o0_ref, o1_ref, o2_ref, s0_ref, s1_ref, s2_ref, x_ref, w_hbm, ex_ref,
                     g_ref, b_ref, y_ref, proj_ref, o0s_ref, s0s_ref, w_ref, stage_ref, sem_ref):
    tm = x_ref.shape[0]

    @pl.when(_first_step(2))
    def _():
        _load_plain_weights(w_hbm, (layer,), w_ref, stage_ref, sem_ref)

    n16 = tm // MAX_DILATION
    n_chunks = D_MODEL // LANES
    pairs = ATTN_HEADS // 2
    classes = range(MAX_DILATION)
    mid = DILATED_PATTERNS[1][1]
    assert [d for _, d in DILATED_PATTERNS] == [1, mid, MAX_DILATION]

    for r in range(mid):
        for k in range(2):
            s0s_ref[k, r, :, :] = s0_ref[k, 0, pl.ds(r, tm // mid, stride=mid), :]
        for p in range(pairs):
            o0s_ref[r, p, :, :] = o0_ref[0, p, pl.ds(r, tm // mid, stride=mid), :]
    o_refs = ((o0s_ref, mid), (o1_ref, mid), (o2_ref, MAX_DILATION))
    s_refs = ((s0s_ref, mid), (s1_ref, mid), (s2_ref, MAX_DILATION))

    def stat(k):
        return [jnp.concatenate([_class_rows(ref.at[k], (), d, r, n16) for r in classes], axis=0)
                for ref, d in s_refs]

    ms, ls = stat(0), stat(1)
    mx = jnp.maximum(jnp.maximum(ms[0], ms[1]), ms[2])
    es = [jnp.exp2(v - mx) for v in ms]
    inv = 1.0 / (ls[0] * es[0] + ls[1] * es[1] + ls[2] * es[2])
    halves = []
    for e in es:
        w = e * inv
        hi = w.astype(BF16)
        halves.append(jnp.concatenate([hi, (w - hi.astype(F32)).astype(BF16)], axis=1))

    pieces = []
    for pp in range(ATTN_HEADS // 4):
        cols = slice(2 * pp * LANES, (2 * pp + 2) * LANES)
        mixed = jnp.zeros((tm, 2 * LANES), F32)
        for g, (ref, d) in enumerate(o_refs):
            w_wide = jnp.dot(halves[g], ex_ref[:, cols], preferred_element_type=F32)
            o_g = jnp.concatenate(
                [jnp.concatenate([_class_rows(ref, (p,), d, r, n16) for r in classes], axis=0)
                 for p in (2 * pp, 2 * pp + 1)], axis=1)
            mixed = mixed + w_wide * o_g
        pieces.append(mixed.astype(BF16))
    proj = jnp.dot(jnp.concatenate(pieces, axis=1), w_ref[...], preferred_element_type=F32)
    pitch = proj_ref.shape[1] // MAX_DILATION
    for c in range(n_chunks):
        for r in classes:
            proj_ref[c, r * pitch:r * pitch + n16, :] = proj[r * n16:(r + 1) * n16,
                                                             c * LANES:(c + 1) * LANES]
    g = g_ref[...]
    b = b_ref[...]
    for n in range(n16):
        tok = slice(n * MAX_DILATION, (n + 1) * MAX_DILATION)
        y = jnp.concatenate([proj_ref[c, pl.ds(n, MAX_DILATION, stride=pitch), :]
                             for c in range(n_chunks)], axis=1)
        y_ref[tok, :] = _layer_norm(DEEPNORM_ALPHA * x_ref[tok, :] + y, g, b)


def _head_expansion():
    e = (np.arange(D_MODEL)[None, :] // ATTN_HEAD_DIM == np.arange(LANES)[:, None])
    return jnp.asarray(np.concatenate([e, e], axis=0).astype(np.float32), dtype=BF16)


def _attn_out(os_, stats, x2d, w_out, layer, ln_g, ln_b, batch, seq):
    t = x2d.shape[0]
    tm = OUT_TM
    tiles = seq // tm
    pairs = ATTN_HEADS // 2
    dils = [d for _, d in DILATED_PATTERNS]
    o_spec = lambda d: pl.BlockSpec((None, d, pairs, tm // d, LANES), lambda b, i: (b, 0, 0, i, 0))
    s_spec = lambda d: pl.BlockSpec((2, None, d, tm // d, LANES), lambda b, i: (0, b, 0, i, 0))
    row = pl.BlockSpec((tm, D_MODEL), lambda b, i: (b * tiles + i, 0))
    return pl.pallas_call(
        functools.partial(_attn_out_kernel, layer),
        out_shape=jax.ShapeDtypeStruct((t, D_MODEL), F32),
        grid=(batch, tiles),
        in_specs=([o_spec(d) for d in dils] + [s_spec(d) for d in dils]
                  + [row, _HBM, _resident((2 * LANES, D_MODEL)),
                     _resident((1, D_MODEL)), _resident((1, D_MODEL))]),
        out_specs=row,
        scratch_shapes=[pltpu.VMEM((D_MODEL // LANES, tm + SUBLANES * MAX_DILATION, LANES), F32),
                        pltpu.VMEM((dils[1], pairs, tm // dils[1], LANES), F32),
                        pltpu.VMEM((2, dils[1], tm // dils[1], LANES), F32),
                        pltpu.VMEM((D_MODEL, D_MODEL), BF16)] + _weight_scratch(),
        compiler_params=pltpu.CompilerParams(
            dimension_semantics=("arbitrary", "arbitrary"), vmem_limit_bytes=VMEM_LIMIT),
        name="attn_out_ln",
    )(*os_, *stats, x2d, w_out, _head_expansion(), ln_g, ln_b)


def _ffn_kernel(layer, x_ref, wu_hbm, wd_hbm, g_ref, b_ref, y_ref, h_ref, wu_ref, wd_ref,
                stage_ref, sem_ref):
    @pl.when(_first_step(1))
    def _():
        _load_plain_weights(wu_hbm, (layer,), wu_ref, stage_ref, sem_ref)
        _load_plain_weights(wd_hbm, (layer,), wd_ref, stage_ref, sem_ref)

    for t in range(x_ref.shape[0] // FFN_SUB):
        rows = slice(t * FFN_SUB, (t + 1) * FFN_SUB)
        x = x_ref[rows, :]
        xb = x.astype(BF16)
        for c in range(D_FF // D_MODEL):
            sl = slice(c * D_MODEL, (c + 1) * D_MODEL)
            h = jnp.dot(xb, wu_ref[:, sl], preferred_element_type=F32)
            h_ref[rows, sl] = jnp.square(jnp.maximum(h, 0.0)).astype(BF16)
        y = jnp.dot(h_ref[rows, :], wd_ref[...], preferred_element_type=F32)
        y_ref[rows, :] = _layer_norm(DEEPNORM_ALPHA * x + y, g_ref[...], b_ref[...])


def _ffn(x2d, layer, w_up, w_down, ln_g, ln_b):
    t = x2d.shape[0]
    tm = FFN_TM
    return pl.pallas_call(
        functools.partial(_ffn_kernel, layer),
        out_shape=jax.ShapeDtypeStruct((t, D_MODEL), F32),
        grid=(t // tm,),
        in_specs=[pl.BlockSpec((tm, D_MODEL), lambda i: (i, 0)),
                  _HBM, _HBM, _resident((1, D_MODEL)), _resident((1, D_MODEL))],
        out_specs=pl.BlockSpec((tm, D_MODEL), lambda i: (i, 0)),
        scratch_shapes=[pltpu.VMEM((tm, D_FF), BF16), pltpu.VMEM((D_MODEL, D_FF), BF16),
                        pltpu.VMEM((D_FF, D_MODEL), BF16)] + _weight_scratch(),
        compiler_params=pltpu.CompilerParams(
            dimension_semantics=("arbitrary",), vmem_limit_bytes=VMEM_LIMIT),
        name="ffn_ln",
    )(x2d, w_up, w_down, ln_g, ln_b)


def _hgrn_pair_chain(xb, wi_ref, p, lb, tri, block_causal, ng, states, on_ref, out_rows, t, done):
    hk = HGRN_HEADS * HGRN_DK
    dk = HGRN_DK
    c_len = HGRN_CHUNK
    pw = 2 * dk
    rows = xb.shape[0]
    n_chunks = rows // c_len
    cols = slice(p * pw, (p + 1) * pw)
    contract_last = (((1,), (1,)), ((), ()))
    contract_rows = (((0,), (0,)), ((), ()))

    q_raw = jnp.dot(xb, wi_ref[:, p * pw:(p + 1) * pw], preferred_element_type=F32)
    z = jnp.dot(xb, wi_ref[:, hk + p * pw:hk + (p + 1) * pw], preferred_element_type=F32)
    v = jnp.dot(xb, wi_ref[:, 2 * hk + p * pw:2 * hk + (p + 1) * pw], preferred_element_type=F32)
    yield
    lb_p = lb[:, cols]
    key = (1.0 - lb_p) / (1.0 + jnp.exp(z))
    log_f = jnp.log(1.0 - key)
    q = q_raw / (1.0 + jnp.exp(-q_raw))
    v_b = v.astype(BF16)
    hi = log_f.astype(BF16)
    lo = (log_f - hi.astype(F32)).astype(BF16)
    yield
    bcum = (jnp.dot(tri, hi, preferred_element_type=F32)
            + jnp.dot(tri, lo, preferred_element_type=F32))
    yield
    last = [bcum[(c + 1) * c_len - 1:(c + 1) * c_len] for c in range(n_chunks)]
    b_last = jnp.concatenate([jnp.broadcast_to(r, (c_len, pw)) for r in last], axis=0)
    q_dec = (q * jnp.exp(bcum)).astype(BF16)
    k_dec = (key * jnp.exp(-bcum)).astype(BF16)
    k_end = (key * jnp.exp(b_last - bcum)).astype(BF16)
    yield
    scores = [lax.dot_general(q_dec[:, hh * dk:(hh + 1) * dk], k_dec[:, hh * dk:(hh + 1) * dk],
                              contract_last, preferred_element_type=F32) for hh in range(2)]
    yield
    intra = [jnp.dot(jnp.where(block_causal, scores[hh], 0.0).astype(BF16),
                     v_b[:, hh * dk:(hh + 1) * dk], preferred_element_type=F32) for hh in range(2)]
    yield
    zero_st = jnp.zeros((HGRN_DV, dk), BF16)
    zero_k = jnp.zeros((c_len, dk), BF16)
    kvs = []
    for c in range(n_chunks):
        rs = slice(c * c_len, (c + 1) * c_len)
        v_rows = jnp.concatenate([v_b[rs, :dk], v_b[rs, dk:]], axis=0)
        k_rows = jnp.concatenate(
            [jnp.concatenate([k_end[rs, :dk], zero_k], axis=1),
             jnp.concatenate([zero_k, k_end[rs, dk:]], axis=1)], axis=0)
        kvs.append(lax.dot_general(v_rows, k_rows, contract_rows,
                                   preferred_element_type=F32))
        if c % 2 == 1:
            yield
    while t > 0 and (t - 1, p) not in done:
        yield
    st0, st1 = states[2 * p], states[2 * p + 1]
    inter = []
    for c in range(n_chunks):
        rs = slice(c * c_len, (c + 1) * c_len)
        st_pair = jnp.concatenate(
            [jnp.concatenate([st0.astype(BF16), zero_st], axis=1),
             jnp.concatenate([zero_st, st1.astype(BF16)], axis=1)], axis=0)
        inter.append(lax.dot_general(q_dec[rs], st_pair, contract_last,
                                     preferred_element_type=F32))
        decay = jnp.exp(last[c])
        st0 = decay[:, :dk] * st0 + kvs[c][:, :dk]
        st1 = decay[:, dk:] * st1 + kvs[c][:, dk:]
        yield
    states[2 * p], states[2 * p + 1] = st0, st1
    inter = jnp.concatenate(inter, axis=0)
    outs = []
    for hh in range(2):
        ls = slice(hh * dk, (hh + 1) * dk)
        o = intra[hh] + inter[:, ls]
        o = o * lax.rsqrt(jnp.mean(o * o, axis=-1, keepdims=True) + RMS_EPS) * ng[:, cols][:, ls]
        outs.append(o.astype(BF16))
    on_ref[out_rows, cols] = jnp.concatenate(outs, axis=1)
    done.add((t, p))


def _hgrn_out_chain(x_ref, on_ref, wo_ref, g_ref, b_ref, y_ref, rows, t, done):
    while any((t, p) not in done for p in range(HGRN_HEADS // 2)):
        yield
    y = jnp.dot(on_ref[rows, :], wo_ref[...], preferred_element_type=F32)
    yield
    y_ref[rows, :] = _layer_norm(DEEPNORM_ALPHA * x_ref[rows, :] + y, g_ref[...], b_ref[...])


def _run_staggered(chains, stagger):
    active, pending, tick = [], list(chains), 0
    while active or pending:
        if pending and tick % stagger == 0:
            active.append(pending.pop(0))
        tick += 1
        for gen in list(active):
            try:
                next(gen)
            except StopIteration:
                active.remove(gen)


def _hgrn_kernel(layer, w_layer, sub, x_ref, wi_hbm, wo_hbm, lbl_ref, ng_ref, g_ref, b_ref, y_ref,
                 state_ref, on_ref, wi_ref, wo_ref, stage_ref, sem_ref):
    tm = x_ref.shape[0]
    c_len = HGRN_CHUNK

    @pl.when(_first_step(2))
    def _():
        _load_plain_weights(wi_hbm, (w_layer,), wi_ref, stage_ref, sem_ref)
        _load_plain_weights(wo_hbm, (w_layer,), wo_ref, stage_ref, sem_ref)

    @pl.when(pl.program_id(1) == 0)
    def _():
        state_ref[...] = jnp.zeros_like(state_ref)

    logits = lbl_ref[...]
    ex = jnp.exp(logits - jnp.max(logits, axis=0, keepdims=True))
    sm = ex / jnp.sum(ex, axis=0, keepdims=True)
    lb = jnp.sum(sm[1:layer + 1], axis=0, keepdims=True)

    ri = lax.broadcasted_iota(jnp.int32, (sub, sub), 0)
    ci = lax.broadcasted_iota(jnp.int32, (sub, sub), 1)
    block_causal = (ri // c_len == ci // c_len) & (ci <= ri)
    tri = block_causal.astype(F32).astype(BF16)
    ng = ng_ref[...]
    states = [state_ref[h] for h in range(HGRN_HEADS)]

    chains, done = [], set()
    for t in range(tm // sub):
        rows = slice(t * sub, (t + 1) * sub)
        xb = x_ref[rows, :].astype(BF16)
        for p in range(HGRN_HEADS // 2):
            chains.append(_hgrn_pair_chain(xb, wi_ref, p, lb, tri, block_causal, ng, states,
                                           on_ref, rows, t, done))
        chains.append(_hgrn_out_chain(x_ref, on_ref, wo_ref, g_ref, b_ref, y_ref, rows, t, done))
    _run_staggered(chains, HGRN_STAGGER)
    for h in range(HGRN_HEADS):
        state_ref[h] = states[h]


def _hgrn_mixer(layer, w_layer, x2d, w_in, w_out, lb_logits, norm_g, ln_g, ln_b, batch, seq):
    t = x2d.shape[0]
    tm = HGRN_TM
    tiles = seq // tm
    d_in = w_in.shape[-1]
    row = pl.BlockSpec((tm, D_MODEL), lambda b, i: (b * tiles + i, 0))
    return pl.pallas_call(
        functools.partial(_hgrn_kernel, layer, w_layer, HGRN_SUB),
        out_shape=jax.ShapeDtypeStruct((t, D_MODEL), F32),
        grid=(batch, tiles),
        in_specs=[row, _HBM, _HBM, _resident((DEPTH, D_MODEL)), _resident((1, D_MODEL)),
                  _resident((1, D_MODEL)), _resident((1, D_MODEL))],
        out_specs=row,
        scratch_shapes=[pltpu.VMEM((HGRN_HEADS, HGRN_DV, HGRN_DK), F32),
                        pltpu.VMEM((tm, D_MODEL), BF16), pltpu.VMEM((D_MODEL, d_in), BF16),
                        pltpu.VMEM((D_MODEL, D_MODEL), BF16)] + _weight_scratch(),
        compiler_params=pltpu.CompilerParams(
            dimension_semantics=("arbitrary", "arbitrary"), vmem_limit_bytes=VMEM_LIMIT),
        name="hgrn2_mixer_ln",
    )(x2d, w_in, w_out, lb_logits, norm_g, ln_g, ln_b)


def kernel(x, attn_w_in, attn_w_out, hgrn_w_in, hgrn_w_out, hgrn_norm_g, lb_logits,
           ln_mix_g, ln_mix_b, ln_ffn_g, ln_ffn_b, ffn_w_up, ffn_w_down):
    batch, seq, d = x.shape
    assert d == D_MODEL and lb_logits.shape[0] == DEPTH
    for window, dilation in DILATED_PATTERNS:
        assert window // dilation == ATTN_BLK and seq % window == 0
        assert QKV_TM % (dilation * BF16_ROWS) == 0 and OUT_TM % (dilation * SUBLANES) == 0
        assert dilation % max(1, ATTN_TQ // (seq // dilation)) == 0
    assert seq % QKV_TM == 0 and seq % OUT_TM == 0
    assert seq % HGRN_TM == 0 and HGRN_TM % HGRN_SUB == 0 and HGRN_SUB % HGRN_CHUNK == 0
    h = x.reshape(batch * seq, d)
    row = lambda a: a.reshape(1, -1)
    for i in range(DEPTH):
        j = i // 2
        if i % 2 == 0:
            os_, stats = [], []
            for g, (_, dil) in enumerate(DILATED_PATTERNS):
                qkv = _qkv_rope(h, attn_w_in, j, g, dil, batch, seq)
                o, st = _attention_group(qkv, dil, batch, seq)
                os_.append(o)
                stats.append(st)
            h = _attn_out(os_, stats, h, attn_w_out, j, row(ln_mix_g[i]), row(ln_mix_b[i]),
                          batch, seq)
        else:
            h = _hgrn_mixer(i, j, h, hgrn_w_in, hgrn_w_out, lb_logits, row(hgrn_norm_g[j]),
                            row(ln_mix_g[i]), row(ln_mix_b[i]), batch, seq)
        h = _ffn(h, i, ffn_w_up, ffn_w_down, row(ln_ffn_g[i]), row(ln_ffn_b[i]))
    return h.reshape(batch, seq, d)
```

```python
import functools
import math

import jax
import jax.numpy as jnp
from jax import lax
from jax.experimental import pallas as pl
from jax.experimental.pallas import tpu as pltpu
import numpy as np

F32 = jnp.float32
BF16 = jnp.bfloat16

D_MODEL = 1024
DEPTH = 2
ATTN_HEAD_DIM = 64
ATTN_HEADS = D_MODEL // ATTN_HEAD_DIM
DILATED_PATTERNS = ((128, 1), (512, 4), (2048, 16))
N_GROUPS = len(DILATED_PATTERNS)
MAX_DILATION = max(d for _, d in DILATED_PATTERNS)
ROPE_THETA = 10000.0
HGRN_HEADS = 8
HGRN_DK = 128
HGRN_DV = 128
HGRN_CHUNK = 64
D_FF = 4 * D_MODEL
LN_EPS = 1e-5
RMS_EPS = 1e-6
DEEPNORM_ALPHA = (2 * DEPTH) ** 0.25

LANES = 128
SUBLANES = 8
BF16_ROWS = 2 * SUBLANES
ATTN_BLK = 128
HALF = ATTN_HEAD_DIM // 2
MASK_VALUE = -1e30
LN2 = math.log(2.0)
Q_SCALE = ATTN_HEAD_DIM ** -0.5 / LN2
VMEM_LIMIT = 56 * 1024 * 1024
MAX_ROW_STRIDE = 4
W_CHUNK_ROWS = 1024
W_CHUNK_COLS = 256
W_SLOTS = 4

QKV_TM = 1024
QKV_SUB = 512
ATTN_TQ = 1024
OUT_TM = 512
OUT_SUB = 256
FFN_TM = 1024
FFN_SUB = 256
HGRN_TM = 512
HGRN_SUB = 256
HGRN_STAGGER = 1


def _layer_norm(y, g, b):
    mu = jnp.mean(y, axis=-1, keepdims=True)
    d = y - mu
    var = jnp.mean(d * d, axis=-1, keepdims=True)
    return d * lax.rsqrt(var + LN_EPS) * g + b


def _resident(shape):
    nd = len(shape)
    return pl.BlockSpec(shape, lambda *_: (0,) * nd, pipeline_mode=pl.Buffered(1))


_HBM = pl.BlockSpec(memory_space=pl.ANY)


def _first_step(grid_rank):
    ids = [pl.program_id(a) == 0 for a in range(grid_rank)]
    return functools.reduce(jnp.logical_and, ids)


def _run_staggered(chains, stagger):
    active, pending, tick = [], list(chains), 0
    while active or pending:
        if pending and tick % stagger == 0:
            active.append(pending.pop(0))
        tick += 1
        for gen in list(active):
            try:
                next(gen)
            except StopIteration:
                active.remove(gen)


def _weight_scratch():
    return [pltpu.VMEM((W_SLOTS, W_CHUNK_ROWS, W_CHUNK_COLS), F32),
            pltpu.SemaphoreType.DMA((W_SLOTS,))]


def _load_weights(chunks, stage_ref, sem_ref, store):
    copies = [pltpu.make_async_copy(src, stage_ref.at[i % W_SLOTS], sem_ref.at[i % W_SLOTS])
              for i, src in enumerate(chunks)]
    ahead = W_SLOTS - 1
    for cp in copies[:ahead]:
        cp.start()
    for i, cp in enumerate(copies):
        if i + ahead < len(copies):
            copies[i + ahead].start()
        cp.wait()
        store(i, stage_ref[i % W_SLOTS])


def _weight_chunks(w_hbm, lead, n_rows, col0, n_cols):
    out = []
    for r in range(0, n_rows, W_CHUNK_ROWS):
        for c in range(0, n_cols, W_CHUNK_COLS):
            view = w_hbm.at[(*lead, pl.ds(r, W_CHUNK_ROWS), pl.ds(col0 + c, W_CHUNK_COLS))]
            out.append((view, r, c))
    return out


def _load_plain_weights(w_hbm, lead, w_ref, stage_ref, sem_ref):
    chunks = _weight_chunks(w_hbm, lead, w_ref.shape[0], 0, w_ref.shape[1])

    def store(i, val):
        _, r, c = chunks[i]
        w_ref[r:r + W_CHUNK_ROWS, c:c + W_CHUNK_COLS] = val.astype(BF16)

    _load_weights([v for v, _, _ in chunks], stage_ref, sem_ref, store)


def _qkv_rope_kernel(dilation, layer, g, x_ref, w_hbm, tab_ref, o_ref, xb_ref, xs_ref, w_ref,
                     stage_ref, sem_ref):
    tm = x_ref.shape[0]
    n_per = tm // dilation
    n_chunks = D_MODEL // LANES

    @pl.when(_first_step(2))
    def _():
        chunks = _weight_chunks(w_hbm, (layer,), D_MODEL, 3 * g * D_MODEL, 3 * D_MODEL)
        lane = lax.broadcasted_iota(jnp.int32, (1, LANES), 1)
        from_right = (lane >= HALF) & (lane < 2 * HALF)
        from_left = (lane >= 2 * HALF) & (lane < 3 * HALF)

        def store(i, val):
            _, _, c = chunks[i]
            if c < 2 * D_MODEL:
                parts = []
                for j in range(W_CHUNK_COLS // LANES):
                    a = val[:, j * LANES:(j + 1) * LANES]
                    parts.append(jnp.where(from_right, pltpu.roll(a, LANES - HALF, 1),
                                           jnp.where(from_left, pltpu.roll(a, HALF, 1), a)))
                val = jnp.concatenate(parts, axis=1)
            w_ref[:, c:c + W_CHUNK_COLS] = val.astype(BF16)

        _load_weights([v for v, _, _ in chunks], stage_ref, sem_ref, store)

    if dilation == 1:
        xb_ref[...] = x_ref[...].astype(BF16)
    else:
        for c in range(n_chunks):
            xs_ref[0, c] = x_ref[:, c * LANES:(c + 1) * LANES]
        passes, left = [], dilation
        while left > 1:
            passes.append(min(left, MAX_ROW_STRIDE))
            left //= passes[-1]
        src, blocks = 0, 1
        for i, st in enumerate(passes):
            rows_blk = tm // blocks
            for blk in range(blocks):
                for r in range(st):
                    lo = (blk + r * blocks) * (rows_blk // st)
                    dst = slice(lo, lo + rows_blk // st)
                    parts = [xs_ref[src, c, pl.ds(blk * rows_blk + r, rows_blk // st, stride=st), :]
                             for c in range(n_chunks)]
                    if i == len(passes) - 1:
                        xb_ref[dst, :] = jnp.concatenate([v.astype(BF16) for v in parts], axis=1)
                    else:
                        for c in range(n_chunks):
                            xs_ref[1 - src, c, dst, :] = parts[c]
            src, blocks = 1 - src, blocks * st

    def store(kind, s, val):
        if n_per >= QKV_SUB:
            start = s * QKV_SUB
            o_ref[kind, start // n_per, start % n_per:start % n_per + QKV_SUB, :] = val
        else:
            per = QKV_SUB // n_per
            for c in range(per):
                o_ref[kind, s * per + c] = val[c * n_per:(c + 1) * n_per]

    for kind in range(3):
        cols = slice(kind * D_MODEL, (kind + 1) * D_MODEL)
        for s in range(tm // QKV_SUB):
            rows = slice(s * QKV_SUB, (s + 1) * QKV_SUB)
            acc = jnp.dot(xb_ref[rows], w_ref[:, cols], preferred_element_type=F32)
            if kind == 2:
                store(kind, s, acc.astype(BF16))
                continue
            cos = tab_ref[0, rows, :]
            sin = tab_ref[1, rows, :]
            if kind == 0:
                cos = cos * Q_SCALE
                sin = sin * Q_SCALE
            pieces = []
            for c in range(n_chunks):
                a = acc[:, c * LANES:(c + 1) * LANES]
                pieces.append((a * cos + pltpu.roll(a, LANES // 2, 1) * sin).astype(BF16))
            store(kind, s, jnp.concatenate(pieces, axis=1))


def _rope_table(seq, dilation, tm):
    inv = ROPE_THETA ** (-np.arange(HALF, dtype=np.float64) * (2.0 / ATTN_HEAD_DIM))
    ang = np.arange(seq, dtype=np.float64)[:, None] * inv[None, :]
    cos = np.tile(np.cos(ang), (1, LANES // HALF))
    sin = np.tile(np.sin(ang), (1, LANES // HALF))
    sign = np.where(np.arange(LANES) < LANES // 2, -1.0, 1.0)
    tab = np.stack([cos, sin * sign])
    tab = tab.reshape(2, seq // tm, tm // dilation, dilation, LANES)
    tab = tab.transpose(0, 1, 3, 2, 4).reshape(2, seq, LANES)
    return jnp.asarray(tab.astype(np.float32))


def _qkv_rope(x2d, w_in, layer, g, dilation, batch, seq):
    tm = QKV_TM
    tiles = seq // tm
    n_per = tm // dilation
    tab = _rope_table(seq, dilation, tm)
    return pl.pallas_call(
        functools.partial(_qkv_rope_kernel, dilation, layer, g),
        out_shape=jax.ShapeDtypeStruct((3, batch, dilation, seq // dilation, D_MODEL), BF16),
        grid=(batch, tiles),
        in_specs=[
            pl.BlockSpec((tm, D_MODEL), lambda b, i: (b * tiles + i, 0)),
            _HBM,
            pl.BlockSpec((2, tm, LANES), lambda b, i: (0, i, 0)),
        ],
        out_specs=pl.BlockSpec((3, None, dilation, n_per, D_MODEL), lambda b, i: (0, b, 0, i, 0)),
        scratch_shapes=[pltpu.VMEM((tm, D_MODEL), BF16),
                        pltpu.VMEM((2, D_MODEL // LANES, tm, LANES), F32),
                        pltpu.VMEM((D_MODEL, 3 * D_MODEL), BF16)] + _weight_scratch(),
        compiler_params=pltpu.CompilerParams(
            dimension_semantics=("arbitrary", "arbitrary"), vmem_limit_bytes=VMEM_LIMIT),
        name=f"qkv_rope_d{dilation}",
    )(x2d, w_in, tab)


def _attn_kernel(q_ref, kp_ref, kc_ref, vp_ref, vc_ref, o_ref, stat_ref):
    i = pl.program_id(2)
    blk = ATTN_BLK
    n_cls, tq = q_ref.shape[0], q_ref.shape[1]
    row = lax.broadcasted_iota(jnp.int32, (2 * blk, 2 * blk), 0) % blk
    col = lax.broadcasted_iota(jnp.int32, (2 * blk, 2 * blk), 1)
    valid = (col >= row) & (col <= row + blk)
    bias = jnp.where(valid, 0.0, MASK_VALUE).astype(F32)
    bias_first = jnp.where(valid & ((col >= blk) | (i > 0)), 0.0, MASK_VALUE).astype(F32)
    lane = lax.broadcasted_iota(jnp.int32, (blk, LANES), 1)
    qk_head0 = ((lane // HALF) % 2 == 0).astype(F32).astype(BF16)
    qk_head1 = ((lane // HALF) % 2 == 1).astype(F32).astype(BF16)
    v_head0 = lane < ATTN_HEAD_DIM
    ones = jnp.ones((2 * blk, LANES), BF16)
    for cls, qb in [(c, b) for c in range(n_cls) for b in range(tq // blk)]:
        rows = slice(qb * blk, (qb + 1) * blk)
        prev_rows = slice((qb - 1) * blk, qb * blk)
        stat = jnp.zeros((blk, LANES), F32)
        for p in range(ATTN_HEADS // 2):
            sl = slice(p * LANES, (p + 1) * LANES)
            q = q_ref[cls, rows, sl]
            qs = jnp.concatenate([q * qk_head0, q * qk_head1], axis=0)
            k_prev = kp_ref[cls, :, sl] if qb == 0 else kc_ref[cls, prev_rows, sl]
            v_prev = vp_ref[cls, :, sl] if qb == 0 else vc_ref[cls, prev_rows, sl]
            k = jnp.concatenate([k_prev, kc_ref[cls, rows, sl]], axis=0)
            v = jnp.concatenate([v_prev, vc_ref[cls, rows, sl]], axis=0)
            s = lax.dot_general(qs, k, (((1,), (1,)), ((), ())), preferred_element_type=F32)
            s = s + (bias_first if qb == 0 else bias)
            m = jnp.max(s, axis=-1, keepdims=True)
            e = jnp.exp2(s - m).astype(BF16)
            pv = jnp.dot(e, jnp.concatenate([v, ones], axis=1),
                         preferred_element_type=F32)
            l_rep = pv[:, LANES:]
            o_ref[cls, p, rows, :] = jnp.where(v_head0, pv[:blk, :LANES], pv[blk:, :LANES])
            stat = jnp.where(lane == 2 * p, m[:blk], stat)
            stat = jnp.where(lane == 2 * p + 1, m[blk:], stat)
            stat = jnp.where(lane == ATTN_HEADS + 2 * p, l_rep[:blk], stat)
            stat = jnp.where(lane == ATTN_HEADS + 2 * p + 1, l_rep[blk:], stat)
        stat_ref[cls, rows, :] = stat


def _attention_group(qkv, dilation, batch, seq):
    n = seq // dilation
    tq = min(ATTN_TQ, n)
    n_cls = ATTN_TQ // tq
    per = tq // ATTN_BLK
    pairs = ATTN_HEADS // 2

    def cur(which):
        return pl.BlockSpec((None, None, n_cls, tq, D_MODEL), lambda b, r, i: (which, b, r, i, 0))

    def prev(which):
        return pl.BlockSpec((None, None, n_cls, ATTN_BLK, D_MODEL),
                            lambda b, r, i: (which, b, r, jnp.maximum(i * per - 1, 0), 0))

    return pl.pallas_call(
        _attn_kernel,
        out_shape=(jax.ShapeDtypeStruct((batch, dilation, pairs, n, LANES), F32),
                   jax.ShapeDtypeStruct((batch, dilation, n, LANES), F32)),
        grid=(batch, dilation // n_cls, n // tq),
        in_specs=[cur(0), prev(1), cur(1), prev(2), cur(2)],
        out_specs=(pl.BlockSpec((None, n_cls, pairs, tq, LANES), lambda b, r, i: (b, r, 0, i, 0)),
                   pl.BlockSpec((None, n_cls, tq, LANES), lambda b, r, i: (b, r, i, 0))),
        compiler_params=pltpu.CompilerParams(
            dimension_semantics=("arbitrary", "arbitrary", "arbitrary"),
            vmem_limit_bytes=VMEM_LIMIT),
        name=f"dilated_attn_d{dilation}",
    )(qkv, qkv, qkv, qkv, qkv)


def _class_rows(ref, lead, dilation, r16, sub, n_sub):
    step = MAX_DILATION // dilation
    if step == 1:
        return ref[(r16, *lead, slice(sub * n_sub, (sub + 1) * n_sub))]
    start = sub * n_sub * step + r16 // dilation
    return ref[(r16 % dilation, *lead, pl.ds(start, n_sub, stride=step), slice(None))]


def _attn_out_chain(sub, o_refs, s_refs, x_ref, w_ref, ex_ref, g_ref, b_ref, y_ref, proj_ref):
    n_sub = OUT_SUB // MAX_DILATION
    n_chunks = D_MODEL // LANES
    classes = range(MAX_DILATION)

    ms = [jnp.concatenate([_class_rows(ref, (), d, r, sub, n_sub) for r in classes], axis=0)
          for ref, d in s_refs]
    ls = [pltpu.roll(v, LANES - ATTN_HEADS, 1) for v in ms]
    mx = jnp.maximum(jnp.maximum(ms[0], ms[1]), ms[2])
    es = [jnp.exp2(v - mx) for v in ms]
    inv = 1.0 / (ls[0] * es[0] + ls[1] * es[1] + ls[2] * es[2])
    head_lane = lax.broadcasted_iota(jnp.int32, (OUT_SUB, LANES), 1) < ATTN_HEADS
    halves = []
    for e in es:
        w = jnp.where(head_lane, e * inv, 0.0)
        hi = w.astype(BF16)
        halves.append(jnp.concatenate([hi, (w - hi.astype(F32)).astype(BF16)], axis=1))
    pieces = []
    for pp in range(ATTN_HEADS // 4):
        cols = slice(2 * pp * LANES, (2 * pp + 2) * LANES)
        mixed = jnp.zeros((OUT_SUB, 2 * LANES), F32)
        for g, (ref, d) in enumerate(o_refs):
            w_wide = jnp.dot(halves[g], ex_ref[:, cols], preferred_element_type=F32)
            o_g = jnp.concatenate(
                [jnp.concatenate([_class_rows(ref, (p,), d, r, sub, n_sub) for r in classes], axis=0)
                 for p in (2 * pp, 2 * pp + 1)], axis=1)
            mixed = mixed + w_wide * o_g
        pieces.append(mixed.astype(BF16))
    mix = jnp.concatenate(pieces, axis=1)
    yield
    proj = jnp.dot(mix, w_ref[...], preferred_element_type=F32)
    pitch = proj_ref.shape[2] // MAX_DILATION
    for c in range(n_chunks):
        for r in classes:
            proj_ref[sub, c, r * pitch:r * pitch + n_sub, :] = proj[r * n_sub:(r + 1) * n_sub,
                                                                    c * LANES:(c + 1) * LANES]
    yield
    g = g_ref[...]
    b = b_ref[...]
    for n in range(n_sub):
        lo = sub * OUT_SUB + n * MAX_DILATION
        tok = slice(lo, lo + MAX_DILATION)
        y = jnp.concatenate([proj_ref[sub, c, pl.ds(n, MAX_DILATION, stride=pitch), :]
                             for c in range(n_chunks)], axis=1)
        y_ref[tok, :] = _layer_norm(DEEPNORM_ALPHA * x_ref[tok, :] + y, g, b)


def _attn_out_kernel(layer, o0_ref, o1_ref, o2_ref, s0_ref, s1_ref, s2_ref, x_ref, w_hbm, ex_ref,
                     g_ref, b_ref, y_ref, proj_ref, o0s_ref, s0s_ref, w_ref, stage_ref, sem_ref):
    tm = x_ref.shape[0]

    @pl.when(_first_step(2))
    def _():
        _load_plain_weights(w_hbm, (layer,), w_ref, stage_ref, sem_ref)

    pairs = ATTN_HEADS // 2
    mid = DILATED_PATTERNS[1][1]
    assert [d for _, d in DILATED_PATTERNS] == [1, mid, MAX_DILATION]

    for r in range(mid):
        s0s_ref[r, :, :] = s0_ref[0, pl.ds(r, tm // mid, stride=mid), :]
        for p in range(pairs):
            o0s_ref[r, p, :, :] = o0_ref[0, p, pl.ds(r, tm // mid, stride=mid), :]
    o_refs = ((o0s_ref, mid), (o1_ref, mid), (o2_ref, MAX_DILATION))
    s_refs = ((s0s_ref, mid), (s1_ref, mid), (s2_ref, MAX_DILATION))
    _run_staggered([_attn_out_chain(sub, o_refs, s_refs, x_ref, w_ref, ex_ref, g_ref, b_ref, y_ref,
                                    proj_ref) for sub in range(tm // OUT_SUB)], 1)


def _head_expansion():
    e = (np.arange(D_MODEL)[None, :] // ATTN_HEAD_DIM == np.arange(LANES)[:, None])
    return jnp.asarray(np.concatenate([e, e], axis=0).astype(np.float32), dtype=BF16)


def _attn_out(os_, stats, x2d, w_out, layer, ln_g, ln_b, batch, seq):
    t = x2d.shape[0]
    tm = OUT_TM
    tiles = seq // tm
    pairs = ATTN_HEADS // 2
    dils = [d for _, d in DILATED_PATTERNS]
    o_spec = lambda d: pl.BlockSpec((None, d, pairs, tm // d, LANES), lambda b, i: (b, 0, 0, i, 0))
    s_spec = lambda d: pl.BlockSpec((None, d, tm // d, LANES), lambda b, i: (b, 0, i, 0))
    row = pl.BlockSpec((tm, D_MODEL), lambda b, i: (b * tiles + i, 0))
    return pl.pallas_call(
        functools.partial(_attn_out_kernel, layer),
        out_shape=jax.ShapeDtypeStruct((t, D_MODEL), F32),
        grid=(batch, tiles),
        in_specs=([o_spec(d) for d in dils] + [s_spec(d) for d in dils]
                  + [row, _HBM, _resident((2 * LANES, D_MODEL)),
                     _resident((1, D_MODEL)), _resident((1, D_MODEL))]),
        out_specs=row,
        scratch_shapes=[pltpu.VMEM((tm // OUT_SUB, D_MODEL // LANES,
                                    OUT_SUB + SUBLANES * MAX_DILATION, LANES), F32),
                        pltpu.VMEM((dils[1], pairs, tm // dils[1], LANES), F32),
                        pltpu.VMEM((dils[1], tm // dils[1], LANES), F32),
                        pltpu.VMEM((D_MODEL, D_MODEL), BF16)] + _weight_scratch(),
        compiler_params=pltpu.CompilerParams(
            dimension_semantics=("arbitrary", "arbitrary"), vmem_limit_bytes=VMEM_LIMIT),
        name="attn_out_ln",
    )(*os_, *stats, x2d, w_out, _head_expansion(), ln_g, ln_b)


def _ffn_kernel(layer, x_ref, wu_hbm, wd_hbm, g_ref, b_ref, y_ref, h_ref, wu_ref, wd_ref,
                stage_ref, sem_ref):
    @pl.when(_first_step(1))
    def _():
        _load_plain_weights(wu_hbm, (layer,), wu_ref, stage_ref, sem_ref)
        _load_plain_weights(wd_hbm, (layer,), wd_ref, stage_ref, sem_ref)

    for t in range(x_ref.shape[0] // FFN_SUB):
        rows = slice(t * FFN_SUB, (t + 1) * FFN_SUB)
        x = x_ref[rows, :]
        xb = x.astype(BF16)
        for c in range(D_FF // D_MODEL):
            sl = slice(c * D_MODEL, (c + 1) * D_MODEL)
            h = jnp.dot(xb, wu_ref[:, sl], preferred_element_type=F32)
            h_ref[rows, sl] = jnp.square(jnp.maximum(h, 0.0)).astype(BF16)
        y = jnp.dot(h_ref[rows, :], wd_ref[...], preferred_element_type=F32)
        y_ref[rows, :] = _layer_norm(DEEPNORM_ALPHA * x + y, g_ref[...], b_ref[...])


def _ffn(x2d, layer, w_up, w_down, ln_g, ln_b):
    t = x2d.shape[0]
    tm = FFN_TM
    return pl.pallas_call(
        functools.partial(_ffn_kernel, layer),
        out_shape=jax.ShapeDtypeStruct((t, D_MODEL), F32),
        grid=(t // tm,),
        in_specs=[pl.BlockSpec((tm, D_MODEL), lambda i: (i, 0)),
                  _HBM, _HBM, _resident((1, D_MODEL)), _resident((1, D_MODEL))],
        out_specs=pl.BlockSpec((tm, D_MODEL), lambda i: (i, 0)),
        scratch_shapes=[pltpu.VMEM((tm, D_FF), BF16), pltpu.VMEM((D_MODEL, D_FF), BF16),
                        pltpu.VMEM((D_FF, D_MODEL), BF16)] + _weight_scratch(),
        compiler_params=pltpu.CompilerParams(
            dimension_semantics=("arbitrary",), vmem_limit_bytes=VMEM_LIMIT),
        name="ffn_ln",
    )(x2d, w_up, w_down, ln_g, ln_b)


def _hgrn_pair_chain(xb, wi_ref, p, lb, tri, block_causal, ng, states, on_ref, out_rows, t, done):
    hk = HGRN_HEADS * HGRN_DK
    dk = HGRN_DK
    c_len = HGRN_CHUNK
    pw = 2 * dk
    rows = xb.shape[0]
    n_chunks = rows // c_len
    cols = slice(p * pw, (p + 1) * pw)
    contract_last = (((1,), (1,)), ((), ()))
    contract_rows = (((0,), (0,)), ((), ()))

    q_raw = jnp.dot(xb, wi_ref[:, p * pw:(p + 1) * pw], preferred_element_type=F32)
    z = jnp.dot(xb, wi_ref[:, hk + p * pw:hk + (p + 1) * pw], preferred_element_type=F32)
    v = jnp.dot(xb, wi_ref[:, 2 * hk + p * pw:2 * hk + (p + 1) * pw], preferred_element_type=F32)
    yield
    lb_p = lb[:, cols]
    key = (1.0 - lb_p) / (1.0 + jnp.exp(z))
    log_f = jnp.log(1.0 - key)
    q = q_raw / (1.0 + jnp.exp(-q_raw))
    v_b = v.astype(BF16)
    hi = log_f.astype(BF16)
    lo = (log_f - hi.astype(F32)).astype(BF16)
    yield
    bcum = (jnp.dot(tri, hi, preferred_element_type=F32)
            + jnp.dot(tri, lo, preferred_element_type=F32))
    yield
    last = [bcum[(c + 1) * c_len - 1:(c + 1) * c_len] for c in range(n_chunks)]
    b_last = jnp.concatenate([jnp.broadcast_to(r, (c_len, pw)) for r in last], axis=0)
    q_dec = (q * jnp.exp(bcum)).astype(BF16)
    k_dec = (key * jnp.exp(-bcum)).astype(BF16)
    k_end = (key * jnp.exp(b_last - bcum)).astype(BF16)
    yield
    scores = [lax.dot_general(q_dec[:, hh * dk:(hh + 1) * dk], k_dec[:, hh * dk:(hh + 1) * dk],
                              contract_last, preferred_element_type=F32) for hh in range(2)]
    yield
    intra = [jnp.dot(jnp.where(block_causal, scores[hh], 0.0).astype(BF16),
                     v_b[:, hh * dk:(hh + 1) * dk], preferred_element_type=F32) for hh in range(2)]
    yield
    zero_st = jnp.zeros((HGRN_DV, dk), BF16)
    zero_k = jnp.zeros((c_len, dk), BF16)
    kvs = []
    for c in range(n_chunks):
        rs = slice(c * c_len, (c + 1) * c_len)
        v_rows = jnp.concatenate([v_b[rs, :dk], v_b[rs, dk:]], axis=0)
        k_rows = jnp.concatenate(
            [jnp.concatenate([k_end[rs, :dk], zero_k], axis=1),
             jnp.concatenate([zero_k, k_end[rs, dk:]], axis=1)], axis=0)
        kvs.append(lax.dot_general(v_rows, k_rows, contract_rows,
                                   preferred_element_type=F32))
        if c % 2 == 1:
            yield
    while t > 0 and (t - 1, p) not in done:
        yield
    st0, st1 = states[2 * p], states[2 * p + 1]
    inter = []
    for c in range(n_chunks):
        rs = slice(c * c_len, (c + 1) * c_len)
        st_pair = jnp.concatenate(
            [jnp.concatenate([st0.astype(BF16), zero_st], axis=1),
             jnp.concatenate([zero_st, st1.astype(BF16)], axis=1)], axis=0)
        inter.append(lax.dot_general(q_dec[rs], st_pair, contract_last,
                                     preferred_element_type=F32))
        decay = jnp.exp(last[c])
        st0 = decay[:, :dk] * st0 + kvs[c][:, :dk]
        st1 = decay[:, dk:] * st1 + kvs[c][:, dk:]
        yield
    states[2 * p], states[2 * p + 1] = st0, st1
    inter = jnp.concatenate(inter, axis=0)
    outs = []
    for hh in range(2):
        ls = slice(hh * dk, (hh + 1) * dk)
        o = intra[hh] + inter[:, ls]
        o = o * lax.rsqrt(jnp.mean(o * o, axis=-1, keepdims=True) + RMS_EPS) * ng[:, cols][:, ls]
        outs.append(o.astype(BF16))
    on_ref[out_rows, cols] = jnp.concatenate(outs, axis=1)
    done.add((t, p))


def _hgrn_out_chain(x_ref, on_ref, wo_ref, g_ref, b_ref, y_ref, rows, t, done):
    while any((t, p) not in done for p in range(HGRN_HEADS // 2)):
        yield
    y = jnp.dot(on_ref[rows, :], wo_ref[...], preferred_element_type=F32)
    yield
    y_ref[rows, :] = _layer_norm(DEEPNORM_ALPHA * x_ref[rows, :] + y, g_ref[...], b_ref[...])


def _hgrn_kernel(layer, w_layer, sub, x_ref, wi_hbm, wo_hbm, lbl_ref, ng_ref, g_ref, b_ref, y_ref,
                 state_ref, on_ref, wi_ref, wo_ref, stage_ref, sem_ref):
    tm = x_ref.shape[0]
    c_len = HGRN_CHUNK

    @pl.when(_first_step(2))
    def _():
        _load_plain_weights(wi_hbm, (w_layer,), wi_ref, stage_ref, sem_ref)
        _load_plain_weights(wo_hbm, (w_layer,), wo_ref, stage_ref, sem_ref)

    @pl.when(pl.program_id(1) == 0)
    def _():
        state_ref[...] = jnp.zeros_like(state_ref)

    logits = lbl_ref[...]
    ex = jnp.exp(logits - jnp.max(logits, axis=0, keepdims=True))
    sm = ex / jnp.sum(ex, axis=0, keepdims=True)
    lb = jnp.sum(sm[1:layer + 1], axis=0, keepdims=True)

    ri = lax.broadcasted_iota(jnp.int32, (sub, sub), 0)
    ci = lax.broadcasted_iota(jnp.int32, (sub, sub), 1)
    block_causal = (ri // c_len == ci // c_len) & (ci <= ri)
    tri = block_causal.astype(F32).astype(BF16)
    ng = ng_ref[...]
    states = [state_ref[h] for h in range(HGRN_HEADS)]

    chains, done = [], set()
    for t in range(tm // sub):
        rows = slice(t * sub, (t + 1) * sub)
        xb = x_ref[rows, :].astype(BF16)
        for p in range(HGRN_HEADS // 2):
            chains.append(_hgrn_pair_chain(xb, wi_ref, p, lb, tri, block_causal, ng, states,
                                           on_ref, rows, t, done))
        chains.append(_hgrn_out_chain(x_ref, on_ref, wo_ref, g_ref, b_ref, y_ref, rows, t, done))
    _run_staggered(chains, HGRN_STAGGER)
    for h in range(HGRN_HEADS):
        state_ref[h] = states[h]


def _hgrn_mixer(layer, w_layer, x2d, w_in, w_out, lb_logits, norm_g, ln_g, ln_b, batch, seq):
    t = x2d.shape[0]
    tm = HGRN_TM
    tiles = seq // tm
    d_in = w_in.shape[-1]
    row = pl.BlockSpec((tm, D_MODEL), lambda b, i: (b * tiles + i, 0))
    return pl.pallas_call(
        functools.partial(_hgrn_kernel, layer, w_layer, HGRN_SUB),
        out_shape=jax.ShapeDtypeStruct((t, D_MODEL), F32),
        grid=(batch, tiles),
        in_specs=[row, _HBM, _HBM, _resident((DEPTH, D_MODEL)), _resident((1, D_MODEL)),
                  _resident((1, D_MODEL)), _resident((1, D_MODEL))],
        out_specs=row,
        scratch_shapes=[pltpu.VMEM((HGRN_HEADS, HGRN_DV, HGRN_DK), F32),
                        pltpu.VMEM((tm, D_MODEL), BF16), pltpu.VMEM((D_MODEL, d_in), BF16),
                        pltpu.VMEM((D_MODEL, D_MODEL), BF16)] + _weight_scratch(),
        compiler_params=pltpu.CompilerParams(
            dimension_semantics=("arbitrary", "arbitrary"), vmem_limit_bytes=VMEM_LIMIT),
        name="hgrn2_mixer_ln",
    )(x2d, w_in, w_out, lb_logits, norm_g, ln_g, ln_b)


def kernel(x, attn_w_in, attn_w_out, hgrn_w_in, hgrn_w_out, hgrn_norm_g, lb_logits,
           ln_mix_g, ln_mix_b, ln_ffn_g, ln_ffn_b, ffn_w_up, ffn_w_down):
    batch, seq, d = x.shape
    assert d == D_MODEL and lb_logits.shape[0] == DEPTH
    for window, dilation in DILATED_PATTERNS:
        assert window // dilation == ATTN_BLK and seq % window == 0
        assert QKV_TM % (dilation * BF16_ROWS) == 0 and OUT_TM % (dilation * SUBLANES) == 0
        assert dilation % max(1, ATTN_TQ // (seq // dilation)) == 0
    assert seq % QKV_TM == 0 and seq % OUT_TM == 0 and OUT_TM % OUT_SUB == 0
    assert OUT_SUB % (MAX_DILATION * SUBLANES) == 0
    assert seq % HGRN_TM == 0 and HGRN_TM % HGRN_SUB == 0 and HGRN_SUB % HGRN_CHUNK == 0
    h = x.reshape(batch * seq, d)
    row = lambda a: a.reshape(1, -1)
    for i in range(DEPTH):
        j = i // 2
        if i % 2 == 0:
            os_, stats = [], []
            for g, (_, dil) in enumerate(DILATED_PATTERNS):
                qkv = _qkv_rope(h, attn_w_in, j, g, dil, batch, seq)
                o, st = _attention_group(qkv, dil, batch, seq)
                os_.append(o)
                stats.append(st)
            h = _attn_out(os_, stats, h, attn_w_out, j, row(ln_mix_g[i]), row(ln_mix_b[i]),
                          batch, seq)
        else:
            h = _hgrn_mixer(i, j, h, hgrn_w_in, hgrn_w_out, lb_logits, row(hgrn_norm_g[j]),
                            row(ln_mix_g[i]), row(ln_mix_b[i]), batch, seq)
        h = _ffn(h, i, ffn_w_up, ffn_w_down, row(ln_ffn_g[i]), row(ln_ffn_b[i]))
    return h.reshape(batch, seq, d)
```

```python
import functools
import math

import jax
import jax.numpy as jnp
from jax import lax
from jax.experimental import pallas as pl
from jax.experimental.pallas import tpu as pltpu
import numpy as np

F32 = jnp.float32
BF16 = jnp.bfloat16

D_MODEL = 1024
DEPTH = 2
ATTN_HEAD_DIM = 64
ATTN_HEADS = D_MODEL // ATTN_HEAD_DIM
DILATED_PATTERNS = ((128, 1), (512, 4), (2048, 16))
MAX_DILATION = max(d for _, d in DILATED_PATTERNS)
ROPE_THETA = 10000.0
HGRN_HEADS = 8
HGRN_DK = 128
HGRN_DV = 128
HGRN_CHUNK = 64
D_FF = 4 * D_MODEL
LN_EPS = 1e-5
RMS_EPS = 1e-6
DEEPNORM_ALPHA = (2 * DEPTH) ** 0.25

LANES = 128
SUBLANES = 8
BF16_ROWS = 2 * SUBLANES
ATTN_BLK = 128
HALF = ATTN_HEAD_DIM // 2
MASK_VALUE = -1e30
LN2 = math.log(2.0)
Q_SCALE = ATTN_HEAD_DIM ** -0.5 / LN2
VMEM_LIMIT = 56 * 1024 * 1024
MAX_ROW_STRIDE = 4
W_CHUNK_ROWS = 1024
W_CHUNK_COLS = 256
W_SLOTS = 4

QKV_TM = 1024
QKV_SUB = 512
ATTN_TQ = 1024
OUT_TM = 512
OUT_SUB = 256
FFN_TM = 1024
FFN_SUB = 256
HGRN_TM = 512
HGRN_SUB = 256
HGRN_STAGGER = 1


def _layer_norm(y, g, b):
    mu = jnp.mean(y, axis=-1, keepdims=True)
    d = y - mu
    var = jnp.mean(d * d, axis=-1, keepdims=True)
    return d * lax.rsqrt(var + LN_EPS) * g + b


def _resident(shape):
    nd = len(shape)
    return pl.BlockSpec(shape, lambda *_: (0,) * nd, pipeline_mode=pl.Buffered(1))


_HBM = pl.BlockSpec(memory_space=pl.ANY)


def _first_step(grid_rank):
    ids = [pl.program_id(a) == 0 for a in range(grid_rank)]
    return functools.reduce(jnp.logical_and, ids)


def _run_staggered(chains, stagger):
    active, pending, tick = [], list(chains), 0
    while active or pending:
        if pending and tick % stagger == 0:
            active.append(pending.pop(0))
        tick += 1
        for gen in list(active):
            try:
                next(gen)
            except StopIteration:
                active.remove(gen)


def _weight_scratch():
    return [pltpu.VMEM((W_SLOTS, W_CHUNK_ROWS, W_CHUNK_COLS), F32),
            pltpu.SemaphoreType.DMA((W_SLOTS,))]


def _load_weights(chunks, stage_ref, sem_ref, store):
    copies = [pltpu.make_async_copy(src, stage_ref.at[i % W_SLOTS], sem_ref.at[i % W_SLOTS])
              for i, src in enumerate(chunks)]
    ahead = W_SLOTS - 1
    for cp in copies[:ahead]:
        cp.start()
    for i, cp in enumerate(copies):
        if i + ahead < len(copies):
            copies[i + ahead].start()
        cp.wait()
        store(i, stage_ref[i % W_SLOTS])


def _weight_chunks(w_hbm, lead, n_rows, col0, n_cols):
    out = []
    for r in range(0, n_rows, W_CHUNK_ROWS):
        for c in range(0, n_cols, W_CHUNK_COLS):
            view = w_hbm.at[(*lead, pl.ds(r, W_CHUNK_ROWS), pl.ds(col0 + c, W_CHUNK_COLS))]
            out.append((view, r, c))
    return out


def _load_plain_weights(w_hbm, lead, w_ref, stage_ref, sem_ref):
    chunks = _weight_chunks(w_hbm, lead, w_ref.shape[0], 0, w_ref.shape[1])

    def store(i, val):
        _, r, c = chunks[i]
        w_ref[r:r + W_CHUNK_ROWS, c:c + W_CHUNK_COLS] = val.astype(BF16)

    _load_weights([v for v, _, _ in chunks], stage_ref, sem_ref, store)


def _row_passes(dilation):
    passes, left = [], dilation
    while left > 1:
        passes.append(min(left, MAX_ROW_STRIDE))
        left //= passes[-1]
    return passes


def _qkv_rope_kernel(dilation, layer, g, *refs):
    n_chunks = D_MODEL // LANES
    n_x = 1 if dilation == 1 else n_chunks
    x_refs = refs[:n_x]
    w_hbm, tab_ref, o_ref, xb_ref, xs_ref, w_ref, stage_ref, sem_ref = refs[n_x:]
    tm = x_refs[0].shape[0]
    n_per = tm // dilation

    @pl.when(_first_step(2))
    def _():
        chunks = _weight_chunks(w_hbm, (layer,), D_MODEL, 3 * g * D_MODEL, 3 * D_MODEL)
        lane = lax.broadcasted_iota(jnp.int32, (1, LANES), 1)
        from_right = (lane >= HALF) & (lane < 2 * HALF)
        from_left = (lane >= 2 * HALF) & (lane < 3 * HALF)

        def store(i, val):
            _, _, c = chunks[i]
            if c < 2 * D_MODEL:
                parts = []
                for j in range(W_CHUNK_COLS // LANES):
                    a = val[:, j * LANES:(j + 1) * LANES]
                    parts.append(jnp.where(from_right, pltpu.roll(a, LANES - HALF, 1),
                                           jnp.where(from_left, pltpu.roll(a, HALF, 1), a)))
                val = jnp.concatenate(parts, axis=1)
            w_ref[:, c:c + W_CHUNK_COLS] = val.astype(BF16)

        _load_weights([v for v, _, _ in chunks], stage_ref, sem_ref, store)

    if dilation == 1:
        xb_ref[...] = x_refs[0][...].astype(BF16)
    else:
        passes = _row_passes(dilation)
        n_slabs = xs_ref.shape[0]
        blocks = 1
        for i, st in enumerate(passes):
            rows_blk = tm // blocks
            for blk in range(blocks):
                for r in range(st):
                    lo = (blk + r * blocks) * (rows_blk // st)
                    dst = slice(lo, lo + rows_blk // st)
                    rows = pl.ds(blk * rows_blk + r, rows_blk // st, stride=st)
                    parts = [x_refs[c][rows, :] if i == 0 else xs_ref[(i - 1) % n_slabs, c, rows, :]
                             for c in range(n_chunks)]
                    if i == len(passes) - 1:
                        xb_ref[dst, :] = jnp.concatenate([v.astype(BF16) for v in parts], axis=1)
                    else:
                        for c in range(n_chunks):
                            xs_ref[i % n_slabs, c, dst, :] = parts[c]
            blocks *= st

    def store(kind, s, val):
        if n_per >= QKV_SUB:
            start = s * QKV_SUB
            o_ref[kind, start // n_per, start % n_per:start % n_per + QKV_SUB, :] = val
        else:
            per = QKV_SUB // n_per
            for c in range(per):
                o_ref[kind, s * per + c] = val[c * n_per:(c + 1) * n_per]

    for kind in range(3):
        cols = slice(kind * D_MODEL, (kind + 1) * D_MODEL)
        for s in range(tm // QKV_SUB):
            rows = slice(s * QKV_SUB, (s + 1) * QKV_SUB)
            acc = jnp.dot(xb_ref[rows], w_ref[:, cols], preferred_element_type=F32)
            if kind == 2:
                store(kind, s, acc.astype(BF16))
                continue
            cos = tab_ref[0, rows, :]
            sin = tab_ref[1, rows, :]
            if kind == 0:
                cos = cos * Q_SCALE
                sin = sin * Q_SCALE
            pieces = []
            for c in range(n_chunks):
                a = acc[:, c * LANES:(c + 1) * LANES]
                pieces.append((a * cos + pltpu.roll(a, LANES // 2, 1) * sin).astype(BF16))
            store(kind, s, jnp.concatenate(pieces, axis=1))


def _rope_table(seq, dilation, tm):
    inv = ROPE_THETA ** (-np.arange(HALF, dtype=np.float64) * (2.0 / ATTN_HEAD_DIM))
    ang = np.arange(seq, dtype=np.float64)[:, None] * inv[None, :]
    cos = np.tile(np.cos(ang), (1, LANES // HALF))
    sin = np.tile(np.sin(ang), (1, LANES // HALF))
    sign = np.where(np.arange(LANES) < LANES // 2, -1.0, 1.0)
    tab = np.stack([cos, sin * sign])
    tab = tab.reshape(2, seq // tm, tm // dilation, dilation, LANES)
    tab = tab.transpose(0, 1, 3, 2, 4).reshape(2, seq, LANES)
    return jnp.asarray(tab.astype(np.float32))


def _qkv_rope(x2d, w_in, layer, g, dilation, batch, seq):
    tm = QKV_TM
    tiles = seq // tm
    n_per = tm // dilation
    tab = _rope_table(seq, dilation, tm)
    if dilation == 1:
        x_specs = [pl.BlockSpec((tm, D_MODEL), lambda b, i: (b * tiles + i, 0))]
    else:
        x_specs = [pl.BlockSpec((tm, LANES), functools.partial(lambda b, i, c: (b * tiles + i, c), c=c))
                   for c in range(D_MODEL // LANES)]
    n_slabs = max(1, len(_row_passes(dilation)) - 1)
    return pl.pallas_call(
        functools.partial(_qkv_rope_kernel, dilation, layer, g),
        out_shape=jax.ShapeDtypeStruct((3, batch, dilation, seq // dilation, D_MODEL), BF16),
        grid=(batch, tiles),
        in_specs=x_specs + [_HBM, pl.BlockSpec((2, tm, LANES), lambda b, i: (0, i, 0))],
        out_specs=pl.BlockSpec((3, None, dilation, n_per, D_MODEL), lambda b, i: (0, b, 0, i, 0)),
        scratch_shapes=[pltpu.VMEM((tm, D_MODEL), BF16),
                        pltpu.VMEM((n_slabs, D_MODEL // LANES, tm, LANES), F32),
                        pltpu.VMEM((D_MODEL, 3 * D_MODEL), BF16)] + _weight_scratch(),
        compiler_params=pltpu.CompilerParams(
            dimension_semantics=("arbitrary", "arbitrary"), vmem_limit_bytes=VMEM_LIMIT),
        name=f"qkv_rope_d{dilation}",
    )(*([x2d] * len(x_specs)), w_in, tab)


def _attn_kernel(q_ref, kp_ref, kc_ref, vp_ref, vc_ref, o_ref, stat_ref):
    i = pl.program_id(2)
    blk = ATTN_BLK
    n_cls, tq = q_ref.shape[0], q_ref.shape[1]
    row = lax.broadcasted_iota(jnp.int32, (2 * blk, 2 * blk), 0) % blk
    col = lax.broadcasted_iota(jnp.int32, (2 * blk, 2 * blk), 1)
    valid = (col >= row) & (col <= row + blk)
    bias = jnp.where(valid, 0.0, MASK_VALUE).astype(F32)
    bias_first = jnp.where(valid & ((col >= blk) | (i > 0)), 0.0, MASK_VALUE).astype(F32)
    lane = lax.broadcasted_iota(jnp.int32, (blk, LANES), 1)
    qk_head0 = ((lane // HALF) % 2 == 0).astype(F32).astype(BF16)
    qk_head1 = ((lane // HALF) % 2 == 1).astype(F32).astype(BF16)
    v_head0 = lane < ATTN_HEAD_DIM
    ones = jnp.ones((2 * blk, LANES), BF16)
    for cls, qb in [(c, b) for c in range(n_cls) for b in range(tq // blk)]:
        rows = slice(qb * blk, (qb + 1) * blk)
        prev_rows = slice((qb - 1) * blk, qb * blk)
        stat = jnp.zeros((blk, LANES), F32)
        for p in range(ATTN_HEADS // 2):
            sl = slice(p * LANES, (p + 1) * LANES)
            q = q_ref[cls, rows, sl]
            qs = jnp.concatenate([q * qk_head0, q * qk_head1], axis=0)
            k_prev = kp_ref[cls, :, sl] if qb == 0 else kc_ref[cls, prev_rows, sl]
            v_prev = vp_ref[cls, :, sl] if qb == 0 else vc_ref[cls, prev_rows, sl]
            k = jnp.concatenate([k_prev, kc_ref[cls, rows, sl]], axis=0)
            v = jnp.concatenate([v_prev, vc_ref[cls, rows, sl]], axis=0)
            s = lax.dot_general(qs, k, (((1,), (1,)), ((), ())), preferred_element_type=F32)
            s = s + (bias_first if qb == 0 else bias)
            m = jnp.max(s, axis=-1, keepdims=True)
            e = jnp.exp2(s - m).astype(BF16)
            pv = jnp.dot(e, jnp.concatenate([v, ones], axis=1),
                         preferred_element_type=F32)
            l_rep = pv[:, LANES:]
            o_ref[cls, p, rows, :] = jnp.where(v_head0, pv[:blk, :LANES], pv[blk:, :LANES])
            stat = jnp.where(lane == 2 * p, m[:blk], stat)
            stat = jnp.where(lane == 2 * p + 1, m[blk:], stat)
            stat = jnp.where(lane == ATTN_HEADS + 2 * p, l_rep[:blk], stat)
            stat = jnp.where(lane == ATTN_HEADS + 2 * p + 1, l_rep[blk:], stat)
        stat_ref[cls, rows, :] = stat


def _attention_group(qkv, dilation, batch, seq):
    n = seq // dilation
    tq = min(ATTN_TQ, n)
    n_cls = ATTN_TQ // tq
    per = tq // ATTN_BLK
    pairs = ATTN_HEADS // 2

    def cur(which):
        return pl.BlockSpec((None, None, n_cls, tq, D_MODEL), lambda b, r, i: (which, b, r, i, 0))

    def prev(which):
        return pl.BlockSpec((None, None, n_cls, ATTN_BLK, D_MODEL),
                            lambda b, r, i: (which, b, r, jnp.maximum(i * per - 1, 0), 0))

    return pl.pallas_call(
        _attn_kernel,
        out_shape=(jax.ShapeDtypeStruct((batch, dilation, pairs, n, LANES), F32),
                   jax.ShapeDtypeStruct((batch, dilation, n, LANES), F32)),
        grid=(batch, dilation // n_cls, n // tq),
        in_specs=[cur(0), prev(1), cur(1), prev(2), cur(2)],
        out_specs=(pl.BlockSpec((None, n_cls, pairs, tq, LANES), lambda b, r, i: (b, r, 0, i, 0)),
                   pl.BlockSpec((None, n_cls, tq, LANES), lambda b, r, i: (b, r, i, 0))),
        compiler_params=pltpu.CompilerParams(
            dimension_semantics=("arbitrary", "arbitrary", "arbitrary"),
            vmem_limit_bytes=VMEM_LIMIT),
        name=f"dilated_attn_d{dilation}",
    )(qkv, qkv, qkv, qkv, qkv)


def _class_rows(ref, lead, dilation, r16, sub, n_sub):
    step = MAX_DILATION // dilation
    if step == 1:
        return ref[(r16, *lead, slice(sub * n_sub, (sub + 1) * n_sub))]
    start = sub * n_sub * step + r16 // dilation
    return ref[(r16 % dilation, *lead, pl.ds(start, n_sub, stride=step), slice(None))]


def _attn_out_chain(sub, o_refs, s_refs, x_ref, w_ref, ex_ref, g_ref, b_ref, y_ref, proj_ref):
    n_sub = OUT_SUB // MAX_DILATION
    n_chunks = D_MODEL // LANES
    classes = range(MAX_DILATION)

    ms = [jnp.concatenate([_class_rows(ref, (), d, r, sub, n_sub) for r in classes], axis=0)
          for ref, d in s_refs]
    ls = [pltpu.roll(v, LANES - ATTN_HEADS, 1) for v in ms]
    mx = jnp.maximum(jnp.maximum(ms[0], ms[1]), ms[2])
    es = [jnp.exp2(v - mx) for v in ms]
    inv = 1.0 / (ls[0] * es[0] + ls[1] * es[1] + ls[2] * es[2])
    head_lane = lax.broadcasted_iota(jnp.int32, (OUT_SUB, LANES), 1) < ATTN_HEADS
    halves = []
    for e in es:
        w = jnp.where(head_lane, e * inv, 0.0)
        hi = w.astype(BF16)
        halves.append(jnp.concatenate([hi, (w - hi.astype(F32)).astype(BF16)], axis=1))
    pieces = []
    for pp in range(ATTN_HEADS // 4):
        cols = slice(2 * pp * LANES, (2 * pp + 2) * LANES)
        mixed = jnp.zeros((OUT_SUB, 2 * LANES), F32)
        for g, (ref, d) in enumerate(o_refs):
            w_wide = jnp.dot(halves[g], ex_ref[:, cols], preferred_element_type=F32)
            o_g = jnp.concatenate(
                [jnp.concatenate([_class_rows(ref, (p,), d, r, sub, n_sub) for r in classes], axis=0)
                 for p in (2 * pp, 2 * pp + 1)], axis=1)
            mixed = mixed + w_wide * o_g
        pieces.append(mixed.astype(BF16))
    mix = jnp.concatenate(pieces, axis=1)
    yield
    proj = jnp.dot(mix, w_ref[...], preferred_element_type=F32)
    pitch = proj_ref.shape[2] // MAX_DILATION
    for c in range(n_chunks):
        for r in classes:
            proj_ref[sub, c, r * pitch:r * pitch + n_sub, :] = proj[r * n_sub:(r + 1) * n_sub,
                                                                    c * LANES:(c + 1) * LANES]
    yield
    g = g_ref[...]
    b = b_ref[...]
    for n in range(n_sub):
        lo = sub * OUT_SUB + n * MAX_DILATION
        tok = slice(lo, lo + MAX_DILATION)
        y = jnp.concatenate([proj_ref[sub, c, pl.ds(n, MAX_DILATION, stride=pitch), :]
                             for c in range(n_chunks)], axis=1)
        y_ref[tok, :] = _layer_norm(DEEPNORM_ALPHA * x_ref[tok, :] + y, g, b)


def _attn_out_kernel(layer, o0_ref, o1_ref, o2_ref, s0_ref, s1_ref, s2_ref, x_ref, w_hbm, ex_ref,
                     g_ref, b_ref, y_ref, proj_ref, o0s_ref, s0s_ref, w_ref, stage_ref, sem_ref):
    tm = x_ref.shape[0]

    @pl.when(_first_step(2))
    def _():
        _load_plain_weights(w_hbm, (layer,), w_ref, stage_ref, sem_ref)

    pairs = ATTN_HEADS // 2
    mid = DILATED_PATTERNS[1][1]
    assert [d for _, d in DILATED_PATTERNS] == [1, mid, MAX_DILATION]

    for r in range(mid):
        s0s_ref[r, :, :] = s0_ref[0, pl.ds(r, tm // mid, stride=mid), :]
        for p in range(pairs):
            o0s_ref[r, p, :, :] = o0_ref[0, p, pl.ds(r, tm // mid, stride=mid), :]
    o_refs = ((o0s_ref, mid), (o1_ref, mid), (o2_ref, MAX_DILATION))
    s_refs = ((s0s_ref, mid), (s1_ref, mid), (s2_ref, MAX_DILATION))
    _run_staggered([_attn_out_chain(sub, o_refs, s_refs, x_ref, w_ref, ex_ref, g_ref, b_ref, y_ref,
                                    proj_ref) for sub in range(tm // OUT_SUB)], 1)


def _head_expansion():
    e = (np.arange(D_MODEL)[None, :] // ATTN_HEAD_DIM == np.arange(LANES)[:, None])
    return jnp.asarray(np.concatenate([e, e], axis=0).astype(np.float32), dtype=BF16)


def _attn_out(os_, stats, x2d, w_out, layer, ln_g, ln_b, batch, seq):
    t = x2d.shape[0]
    tm = OUT_TM
    tiles = seq // tm
    pairs = ATTN_HEADS // 2
    dils = [d for _, d in DILATED_PATTERNS]
    o_spec = lambda d: pl.BlockSpec((None, d, pairs, tm // d, LANES), lambda b, i: (b, 0, 0, i, 0))
    s_spec = lambda d: pl.BlockSpec((None, d, tm // d, LANES), lambda b, i: (b, 0, i, 0))
    row = pl.BlockSpec((tm, D_MODEL), lambda b, i: (b * tiles + i, 0))
    return pl.pallas_call(
        functools.partial(_attn_out_kernel, layer),
        out_shape=jax.ShapeDtypeStruct((t, D_MODEL), F32),
        grid=(batch, tiles),
        in_specs=([o_spec(d) for d in dils] + [s_spec(d) for d in dils]
                  + [row, _HBM, _resident((2 * LANES, D_MODEL)),
                     _resident((1, D_MODEL)), _resident((1, D_MODEL))]),
        out_specs=row,
        scratch_shapes=[pltpu.VMEM((tm // OUT_SUB, D_MODEL // LANES,
                                    OUT_SUB + SUBLANES * MAX_DILATION, LANES), F32),
                        pltpu.VMEM((dils[1], pairs, tm // dils[1], LANES), F32),
                        pltpu.VMEM((dils[1], tm // dils[1], LANES), F32),
                        pltpu.VMEM((D_MODEL, D_MODEL), BF16)] + _weight_scratch(),
        compiler_params=pltpu.CompilerParams(
            dimension_semantics=("arbitrary", "arbitrary"), vmem_limit_bytes=VMEM_LIMIT),
        name="attn_out_ln",
    )(*os_, *stats, x2d, w_out, _head_expansion(), ln_g, ln_b)


def _ffn_kernel(layer, x_ref, wu_hbm, wd_hbm, g_ref, b_ref, y_ref, h_ref, wu_ref, wd_ref,
                stage_ref, sem_ref):
    @pl.when(_first_step(1))
    def _():
        _load_plain_weights(wu_hbm, (layer,), wu_ref, stage_ref, sem_ref)
        _load_plain_weights(wd_hbm, (layer,), wd_ref, stage_ref, sem_ref)

    for t in range(x_ref.shape[0] // FFN_SUB):
        rows = slice(t * FFN_SUB, (t + 1) * FFN_SUB)
        x = x_ref[rows, :]
        xb = x.astype(BF16)
        for c in range(D_FF // D_MODEL):
            sl = slice(c * D_MODEL, (c + 1) * D_MODEL)
            h = jnp.dot(xb, wu_ref[:, sl], preferred_element_type=F32)
            h_ref[rows, sl] = jnp.square(jnp.maximum(h, 0.0)).astype(BF16)
        y = jnp.dot(h_ref[rows, :], wd_ref[...], preferred_element_type=F32)
        y_ref[rows, :] = _layer_norm(DEEPNORM_ALPHA * x + y, g_ref[...], b_ref[...])


def _ffn(x2d, layer, w_up, w_down, ln_g, ln_b):
    t = x2d.shape[0]
    tm = FFN_TM
    return pl.pallas_call(
        functools.partial(_ffn_kernel, layer),
        out_shape=jax.ShapeDtypeStruct((t, D_MODEL), F32),
        grid=(t // tm,),
        in_specs=[pl.BlockSpec((tm, D_MODEL), lambda i: (i, 0)),
                  _HBM, _HBM, _resident((1, D_MODEL)), _resident((1, D_MODEL))],
        out_specs=pl.BlockSpec((tm, D_MODEL), lambda i: (i, 0)),
        scratch_shapes=[pltpu.VMEM((tm, D_FF), BF16), pltpu.VMEM((D_MODEL, D_FF), BF16),
                        pltpu.VMEM((D_FF, D_MODEL), BF16)] + _weight_scratch(),
        compiler_params=pltpu.CompilerParams(
            dimension_semantics=("arbitrary",), vmem_limit_bytes=VMEM_LIMIT),
        name="ffn_ln",
    )(x2d, w_up, w_down, ln_g, ln_b)


def _hgrn_pair_chain(xb, wi_ref, p, lb, tri, block_causal, ng, states, on_ref, out_rows, t, done):
    hk = HGRN_HEADS * HGRN_DK
    dk = HGRN_DK
    c_len = HGRN_CHUNK
    pw = 2 * dk
    rows = xb.shape[0]
    n_chunks = rows // c_len
    cols = slice(p * pw, (p + 1) * pw)
    contract_last = (((1,), (1,)), ((), ()))
    contract_rows = (((0,), (0,)), ((), ()))

    q_raw = jnp.dot(xb, wi_ref[:, p * pw:(p + 1) * pw], preferred_element_type=F32)
    z = jnp.dot(xb, wi_ref[:, hk + p * pw:hk + (p + 1) * pw], preferred_element_type=F32)
    v = jnp.dot(xb, wi_ref[:, 2 * hk + p * pw:2 * hk + (p + 1) * pw], preferred_element_type=F32)
    yield
    lb_p = lb[:, cols]
    key = (1.0 - lb_p) / (1.0 + jnp.exp(z))
    log_f = jnp.log(1.0 - key)
    q = q_raw / (1.0 + jnp.exp(-q_raw))
    v_b = v.astype(BF16)
    hi = log_f.astype(BF16)
    lo = (log_f - hi.astype(F32)).astype(BF16)
    yield
    bcum = (jnp.dot(tri, hi, preferred_element_type=F32)
            + jnp.dot(tri, lo, preferred_element_type=F32))
    yield
    last = [bcum[(c + 1) * c_len - 1:(c + 1) * c_len] for c in range(n_chunks)]
    b_last = jnp.concatenate([jnp.broadcast_to(r, (c_len, pw)) for r in last], axis=0)
    q_dec = (q * jnp.exp(bcum)).astype(BF16)
    k_dec = (key * jnp.exp(-bcum)).astype(BF16)
    k_end = (key * jnp.exp(b_last - bcum)).astype(BF16)
    yield
    scores = [lax.dot_general(q_dec[:, hh * dk:(hh + 1) * dk], k_dec[:, hh * dk:(hh + 1) * dk],
                              contract_last, preferred_element_type=F32) for hh in range(2)]
    yield
    intra = [jnp.dot(jnp.where(block_causal, scores[hh], 0.0).astype(BF16),
                     v_b[:, hh * dk:(hh + 1) * dk], preferred_element_type=F32) for hh in range(2)]
    yield
    zero_st = jnp.zeros((HGRN_DV, dk), BF16)
    zero_k = jnp.zeros((c_len, dk), BF16)
    kvs = []
    for c in range(n_chunks):
        rs = slice(c * c_len, (c + 1) * c_len)
        v_rows = jnp.concatenate([v_b[rs, :dk], v_b[rs, dk:]], axis=0)
        k_rows = jnp.concatenate(
            [jnp.concatenate([k_end[rs, :dk], zero_k], axis=1),
             jnp.concatenate([zero_k, k_end[rs, dk:]], axis=1)], axis=0)
        kvs.append(lax.dot_general(v_rows, k_rows, contract_rows,
                                   preferred_element_type=F32))
        if c % 2 == 1:
            yield
    while t > 0 and (t - 1, p) not in done:
        yield
    st0, st1 = states[2 * p], states[2 * p + 1]
    inter = []
    for c in range(n_chunks):
        rs = slice(c * c_len, (c + 1) * c_len)
        st_pair = jnp.concatenate(
            [jnp.concatenate([st0.astype(BF16), zero_st], axis=1),
             jnp.concatenate([zero_st, st1.astype(BF16)], axis=1)], axis=0)
        inter.append(lax.dot_general(q_dec[rs], st_pair, contract_last,
                                     preferred_element_type=F32))
        decay = jnp.exp(last[c])
        st0 = decay[:, :dk] * st0 + kvs[c][:, :dk]
        st1 = decay[:, dk:] * st1 + kvs[c][:, dk:]
        yield
    states[2 * p], states[2 * p + 1] = st0, st1
    inter = jnp.concatenate(inter, axis=0)
    outs = []
    for hh in range(2):
        ls = slice(hh * dk, (hh + 1) * dk)
        o = intra[hh] + inter[:, ls]
        o = o * lax.rsqrt(jnp.mean(o * o, axis=-1, keepdims=True) + RMS_EPS) * ng[:, cols][:, ls]
        outs.append(o.astype(BF16))
    on_ref[out_rows, cols] = jnp.concatenate(outs, axis=1)
    done.add((t, p))


def _hgrn_out_chain(x_ref, on_ref, wo_ref, g_ref, b_ref, y_ref, rows, t, done):
    while any((t, p) not in done for p in range(HGRN_HEADS // 2)):
        yield
    y = jnp.dot(on_ref[rows, :], wo_ref[...], preferred_element_type=F32)
    yield
    y_ref[rows, :] = _layer_norm(DEEPNORM_ALPHA * x_ref[rows, :] + y, g_ref[...], b_ref[...])


def _hgrn_kernel(layer, w_layer, sub, x_ref, wi_hbm, wo_hbm, lbl_ref, ng_ref, g_ref, b_ref, y_ref,
                 state_ref, on_ref, wi_ref, wo_ref, stage_ref, sem_ref):
    tm = x_ref.shape[0]
    c_len = HGRN_CHUNK

    @pl.when(_first_step(2))
    def _():
        _load_plain_weights(wi_hbm, (w_layer,), wi_ref, stage_ref, sem_ref)
        _load_plain_weights(wo_hbm, (w_layer,), wo_ref, stage_ref, sem_ref)

    @pl.when(pl.program_id(1) == 0)
    def _():
        state_ref[...] = jnp.zeros_like(state_ref)

    logits = lbl_ref[...]
    ex = jnp.exp(logits - jnp.max(logits, axis=0, keepdims=True))
    sm = ex / jnp.sum(ex, axis=0, keepdims=True)
    lb = jnp.sum(sm[1:layer + 1], axis=0, keepdims=True)

    ri = lax.broadcasted_iota(jnp.int32, (sub, sub), 0)
    ci = lax.broadcasted_iota(jnp.int32, (sub, sub), 1)
    block_causal = (ri // c_len == ci // c_len) & (ci <= ri)
    tri = block_causal.astype(F32).astype(BF16)
    ng = ng_ref[...]
    states = [state_ref[h] for h in range(HGRN_HEADS)]

    chains, done = [], set()
    for t in range(tm // sub):
        rows = slice(t * sub, (t + 1) * sub)
        xb = x_ref[rows, :].astype(BF16)
        for p in range(HGRN_HEADS // 2):
            chains.append(_hgrn_pair_chain(xb, wi_ref, p, lb, tri, block_causal, ng, states,
                                           on_ref, rows, t, done))
        chains.append(_hgrn_out_chain(x_ref, on_ref, wo_ref, g_ref, b_ref, y_ref, rows, t, done))
    _run_staggered(chains, HGRN_STAGGER)
    for h in range(HGRN_HEADS):
        state_ref[h] = states[h]


def _hgrn_mixer(layer, w_layer, x2d, w_in, w_out, lb_logits, norm_g, ln_g, ln_b, batch, seq):
    t = x2d.shape[0]
    tm = HGRN_TM
    tiles = seq // tm
    d_in = w_in.shape[-1]
    row = pl.BlockSpec((tm, D_MODEL), lambda b, i: (b * tiles + i, 0))
    return pl.pallas_call(
        functools.partial(_hgrn_kernel, layer, w_layer, HGRN_SUB),
        out_shape=jax.ShapeDtypeStruct((t, D_MODEL), F32),
        grid=(batch, tiles),
        in_specs=[row, _HBM, _HBM, _resident((DEPTH, D_MODEL)), _resident((1, D_MODEL)),
                  _resident((1, D_MODEL)), _resident((1, D_MODEL))],
        out_specs=row,
        scratch_shapes=[pltpu.VMEM((HGRN_HEADS, HGRN_DV, HGRN_DK), F32),
                        pltpu.VMEM((tm, D_MODEL), BF16), pltpu.VMEM((D_MODEL, d_in), BF16),
                        pltpu.VMEM((D_MODEL, D_MODEL), BF16)] + _weight_scratch(),
        compiler_params=pltpu.CompilerParams(
            dimension_semantics=("arbitrary", "arbitrary"), vmem_limit_bytes=VMEM_LIMIT),
        name="hgrn2_mixer_ln",
    )(x2d, w_in, w_out, lb_logits, norm_g, ln_g, ln_b)


def kernel(x, attn_w_in, attn_w_out, hgrn_w_in, hgrn_w_out, hgrn_norm_g, lb_logits,
           ln_mix_g, ln_mix_b, ln_ffn_g, ln_ffn_b, ffn_w_up, ffn_w_down):
    batch, seq, d = x.shape
    assert d == D_MODEL and lb_logits.shape[0] == DEPTH
    for window, dilation in DILATED_PATTERNS:
        assert window // dilation == ATTN_BLK and seq % window == 0
        assert QKV_TM % (dilation * BF16_ROWS) == 0 and OUT_TM % (dilation * SUBLANES) == 0
        assert dilation % max(1, ATTN_TQ // (seq // dilation)) == 0
    assert seq % QKV_TM == 0 and seq % OUT_TM == 0 and OUT_TM % OUT_SUB == 0
    assert OUT_SUB % (MAX_DILATION * SUBLANES) == 0
    assert seq % HGRN_TM == 0 and HGRN_TM % HGRN_SUB == 0 and HGRN_SUB % HGRN_CHUNK == 0
    h = x.reshape(batch * seq, d)
    row = lambda a: a.reshape(1, -1)
    for i in range(DEPTH):
        j = i // 2
        if i % 2 == 0:
            os_, stats = [], []
            for g, (_, dil) in enumerate(DILATED_PATTERNS):
                qkv = _qkv_rope(h, attn_w_in, j, g, dil, batch, seq)
                o, st = _attention_group(qkv, dil, batch, seq)
                os_.append(o)
                stats.append(st)
            h = _attn_out(os_, stats, h, attn_w_out, j, row(ln_mix_g[i]), row(ln_mix_b[i]),
                          batch, seq)
        else:
            h = _hgrn_mixer(i, j, h, hgrn_w_in, hgrn_w_out, lb_logits, row(hgrn_norm_g[j]),
                            row(ln_mix_g[i]), row(ln_mix_b[i]), batch, seq)
        h = _ffn(h, i, ffn_w_up, ffn_w_down, row(ln_ffn_g[i]), row(ln_ffn_b[i]))
    return h.reshape(batch, seq, d)
```

```python
import functools
import math

import jax
import jax.numpy as jnp
from jax import lax
from jax.experimental import pallas as pl
from jax.experimental.pallas import tpu as pltpu
import numpy as np

F32 = jnp.float32
BF16 = jnp.bfloat16

D_MODEL = 1024
DEPTH = 2
ATTN_HEAD_DIM = 64
ATTN_HEADS = D_MODEL // ATTN_HEAD_DIM
DILATED_PATTERNS = ((128, 1), (512, 4), (2048, 16))
MAX_DILATION = max(d for _, d in DILATED_PATTERNS)
ROPE_THETA = 10000.0
HGRN_HEADS = 8
HGRN_DK = 128
HGRN_DV = 128
HGRN_CHUNK = 64
D_FF = 4 * D_MODEL
LN_EPS = 1e-5
RMS_EPS = 1e-6
DEEPNORM_ALPHA = (2 * DEPTH) ** 0.25

LANES = 128
SUBLANES = 8
BF16_ROWS = 2 * SUBLANES
ATTN_BLK = 128
HALF = ATTN_HEAD_DIM // 2
MASK_VALUE = -1e30
LN2 = math.log(2.0)
Q_SCALE = ATTN_HEAD_DIM ** -0.5 / LN2
VMEM_LIMIT = 56 * 1024 * 1024
MAX_ROW_STRIDE = 4
W_CHUNK_ROWS = 1024
W_CHUNK_COLS = 256
W_SLOTS = 4

QKV_TM = 1024
QKV_SUB = 512
ATTN_TQ = 1024
OUT_TM = 512
OUT_SUB = 256
FFN_TM = 1024
FFN_SUB = 256
HGRN_TM = 512
HGRN_SUB = 256
HGRN_STAGGER = 1


def _layer_norm(y, g, b):
    mu = jnp.mean(y, axis=-1, keepdims=True)
    d = y - mu
    var = jnp.mean(d * d, axis=-1, keepdims=True)
    return d * lax.rsqrt(var + LN_EPS) * g + b


def _resident(shape):
    nd = len(shape)
    return pl.BlockSpec(shape, lambda *_: (0,) * nd, pipeline_mode=pl.Buffered(1))


_HBM = pl.BlockSpec(memory_space=pl.ANY)


def _first_step(grid_rank):
    ids = [pl.program_id(a) == 0 for a in range(grid_rank)]
    return functools.reduce(jnp.logical_and, ids)


def _run_staggered(chains, stagger):
    active, pending, tick = [], list(chains), 0
    while active or pending:
        if pending and tick % stagger == 0:
            active.append(pending.pop(0))
        tick += 1
        for gen in list(active):
            try:
                next(gen)
            except StopIteration:
                active.remove(gen)


def _weight_scratch():
    return [pltpu.VMEM((W_SLOTS, W_CHUNK_ROWS, W_CHUNK_COLS), F32),
            pltpu.SemaphoreType.DMA((W_SLOTS,))]


def _load_weights(chunks, stage_ref, sem_ref, store):
    copies = [pltpu.make_async_copy(src, stage_ref.at[i % W_SLOTS], sem_ref.at[i % W_SLOTS])
              for i, src in enumerate(chunks)]
    ahead = W_SLOTS - 1
    for cp in copies[:ahead]:
        cp.start()
    for i, cp in enumerate(copies):
        if i + ahead < len(copies):
            copies[i + ahead].start()
        cp.wait()
        store(i, stage_ref[i % W_SLOTS])


def _weight_chunks(w_hbm, lead, n_rows, col0, n_cols):
    out = []
    for r in range(0, n_rows, W_CHUNK_ROWS):
        for c in range(0, n_cols, W_CHUNK_COLS):
            view = w_hbm.at[(*lead, pl.ds(r, W_CHUNK_ROWS), pl.ds(col0 + c, W_CHUNK_COLS))]
            out.append((view, r, c))
    return out


def _load_plain_weights(w_hbm, lead, w_ref, stage_ref, sem_ref):
    chunks = _weight_chunks(w_hbm, lead, w_ref.shape[0], 0, w_ref.shape[1])

    def store(i, val):
        _, r, c = chunks[i]
        w_ref[r:r + W_CHUNK_ROWS, c:c + W_CHUNK_COLS] = val.astype(BF16)

    _load_weights([v for v, _, _ in chunks], stage_ref, sem_ref, store)


def _row_passes(dilation):
    passes, left = [], dilation
    while left > 1:
        passes.append(min(left, MAX_ROW_STRIDE))
        left //= passes[-1]
    return passes


def _qkv_rope_kernel(dilation, layer, g, *refs):
    n_chunks = D_MODEL // LANES
    n_x = 1 if dilation == 1 else n_chunks
    x_refs = refs[:n_x]
    w_hbm, tab_ref, o_ref, xb_ref, xs_ref, w_ref, stage_ref, sem_ref = refs[n_x:]
    tm = x_refs[0].shape[0]
    n_per = tm // dilation

    @pl.when(_first_step(2))
    def _():
        chunks = _weight_chunks(w_hbm, (layer,), D_MODEL, 3 * g * D_MODEL, 3 * D_MODEL)
        lane = lax.broadcasted_iota(jnp.int32, (1, LANES), 1)
        from_right = (lane >= HALF) & (lane < 2 * HALF)
        from_left = (lane >= 2 * HALF) & (lane < 3 * HALF)

        def store(i, val):
            _, _, c = chunks[i]
            if c < 2 * D_MODEL:
                parts = []
                for j in range(W_CHUNK_COLS // LANES):
                    a = val[:, j * LANES:(j + 1) * LANES]
                    parts.append(jnp.where(from_right, pltpu.roll(a, LANES - HALF, 1),
                                           jnp.where(from_left, pltpu.roll(a, HALF, 1), a)))
                val = jnp.concatenate(parts, axis=1)
            w_ref[:, c:c + W_CHUNK_COLS] = val.astype(BF16)

        _load_weights([v for v, _, _ in chunks], stage_ref, sem_ref, store)

    if dilation == 1:
        xb_ref[...] = x_refs[0][...].astype(BF16)
    else:
        passes = _row_passes(dilation)
        n_slabs = xs_ref.shape[0]
        blocks = 1
        for i, st in enumerate(passes):
            rows_blk = tm // blocks
            for blk in range(blocks):
                for r in range(st):
                    lo = (blk + r * blocks) * (rows_blk // st)
                    dst = slice(lo, lo + rows_blk // st)
                    rows = pl.ds(blk * rows_blk + r, rows_blk // st, stride=st)
                    parts = [x_refs[c][rows, :] if i == 0 else xs_ref[(i - 1) % n_slabs, c, rows, :]
                             for c in range(n_chunks)]
                    if i == len(passes) - 1:
                        xb_ref[dst, :] = jnp.concatenate([v.astype(BF16) for v in parts], axis=1)
                    else:
                        for c in range(n_chunks):
                            xs_ref[i % n_slabs, c, dst, :] = parts[c]
            blocks *= st

    def store(kind, s, val):
        if n_per >= QKV_SUB:
            start = s * QKV_SUB
            o_ref[kind, start // n_per, start % n_per:start % n_per + QKV_SUB, :] = val
        else:
            per = QKV_SUB // n_per
            for c in range(per):
                o_ref[kind, s * per + c] = val[c * n_per:(c + 1) * n_per]

    for kind in range(3):
        cols = slice(kind * D_MODEL, (kind + 1) * D_MODEL)
        for s in range(tm // QKV_SUB):
            rows = slice(s * QKV_SUB, (s + 1) * QKV_SUB)
            acc = jnp.dot(xb_ref[rows], w_ref[:, cols], preferred_element_type=F32)
            if kind == 2:
                store(kind, s, acc.astype(BF16))
                continue
            cos = tab_ref[0, rows, :]
            sin = tab_ref[1, rows, :]
            if kind == 0:
                cos = cos * Q_SCALE
                sin = sin * Q_SCALE
            pieces = []
            for c in range(n_chunks):
                a = acc[:, c * LANES:(c + 1) * LANES]
                pieces.append((a * cos + pltpu.roll(a, LANES // 2, 1) * sin).astype(BF16))
            store(kind, s, jnp.concatenate(pieces, axis=1))


def _rope_table(seq, dilation, tm):
    inv = ROPE_THETA ** (-np.arange(HALF, dtype=np.float64) * (2.0 / ATTN_HEAD_DIM))
    ang = np.arange(seq, dtype=np.float64)[:, None] * inv[None, :]
    cos = np.tile(np.cos(ang), (1, LANES // HALF))
    sin = np.tile(np.sin(ang), (1, LANES // HALF))
    sign = np.where(np.arange(LANES) < LANES // 2, -1.0, 1.0)
    tab = np.stack([cos, sin * sign])
    tab = tab.reshape(2, seq // tm, tm // dilation, dilation, LANES)
    tab = tab.transpose(0, 1, 3, 2, 4).reshape(2, seq, LANES)
    return jnp.asarray(tab.astype(np.float32))


def _qkv_rope(x2d, w_in, layer, g, dilation, batch, seq):
    tm = QKV_TM
    tiles = seq // tm
    n_per = tm // dilation
    tab = _rope_table(seq, dilation, tm)
    if dilation == 1:
        x_specs = [pl.BlockSpec((tm, D_MODEL), lambda b, i: (b * tiles + i, 0))]
    else:
        x_specs = [pl.BlockSpec((tm, LANES), functools.partial(lambda b, i, c: (b * tiles + i, c), c=c))
                   for c in range(D_MODEL // LANES)]
    n_slabs = max(1, len(_row_passes(dilation)) - 1)
    return pl.pallas_call(
        functools.partial(_qkv_rope_kernel, dilation, layer, g),
        out_shape=jax.ShapeDtypeStruct((3, batch, dilation, seq // dilation, D_MODEL), BF16),
        grid=(batch, tiles),
        in_specs=x_specs + [_HBM, pl.BlockSpec((2, tm, LANES), lambda b, i: (0, i, 0))],
        out_specs=pl.BlockSpec((3, None, dilation, n_per, D_MODEL), lambda b, i: (0, b, 0, i, 0)),
        scratch_shapes=[pltpu.VMEM((tm, D_MODEL), BF16),
                        pltpu.VMEM((n_slabs, D_MODEL // LANES, tm, LANES), F32),
                        pltpu.VMEM((D_MODEL, 3 * D_MODEL), BF16)] + _weight_scratch(),
        compiler_params=pltpu.CompilerParams(
            dimension_semantics=("arbitrary", "arbitrary"), vmem_limit_bytes=VMEM_LIMIT),
        name=f"qkv_rope_d{dilation}",
    )(*([x2d] * len(x_specs)), w_in, tab)


def _attn_kernel(q_ref, kp_ref, kc_ref, vp_ref, vc_ref, o_ref, stat_ref):
    i = pl.program_id(2)
    blk = ATTN_BLK
    n_cls, tq = q_ref.shape[0], q_ref.shape[1]
    row = lax.broadcasted_iota(jnp.int32, (2 * blk, 2 * blk), 0) % blk
    col = lax.broadcasted_iota(jnp.int32, (2 * blk, 2 * blk), 1)
    valid = (col >= row) & (col <= row + blk)
    bias = jnp.where(valid, 0.0, MASK_VALUE).astype(F32)
    bias_first = jnp.where(valid & ((col >= blk) | (i > 0)), 0.0, MASK_VALUE).astype(F32)
    lane = lax.broadcasted_iota(jnp.int32, (blk, LANES), 1)
    qk_head0 = ((lane // HALF) % 2 == 0).astype(F32).astype(BF16)
    qk_head1 = ((lane // HALF) % 2 == 1).astype(F32).astype(BF16)
    v_head0 = lane < ATTN_HEAD_DIM
    ones = jnp.ones((2 * blk, LANES), BF16)
    for cls, qb in [(c, b) for c in range(n_cls) for b in range(tq // blk)]:
        rows = slice(qb * blk, (qb + 1) * blk)
        prev_rows = slice((qb - 1) * blk, qb * blk)
        stat = jnp.zeros((blk, LANES), F32)
        for p in range(ATTN_HEADS // 2):
            sl = slice(p * LANES, (p + 1) * LANES)
            q = q_ref[cls, rows, sl]
            qs = jnp.concatenate([q * qk_head0, q * qk_head1], axis=0)
            k_prev = kp_ref[cls, :, sl] if qb == 0 else kc_ref[cls, prev_rows, sl]
            v_prev = vp_ref[cls, :, sl] if qb == 0 else vc_ref[cls, prev_rows, sl]
            k = jnp.concatenate([k_prev, kc_ref[cls, rows, sl]], axis=0)
            v = jnp.concatenate([v_prev, vc_ref[cls, rows, sl]], axis=0)
            s = lax.dot_general(qs, k, (((1,), (1,)), ((), ())), preferred_element_type=F32)
            s = s + (bias_first if qb == 0 else bias)
            m = jnp.max(s, axis=-1, keepdims=True)
            e = jnp.exp2(s - m).astype(BF16)
            pv = jnp.dot(e, jnp.concatenate([v, ones], axis=1),
                         preferred_element_type=F32)
            l_rep = pv[:, LANES:]
            o_ref[cls, p, rows, :] = jnp.where(v_head0, pv[:blk, :LANES], pv[blk:, :LANES])
            stat = jnp.where(lane == 2 * p, m[:blk], stat)
            stat = jnp.where(lane == 2 * p + 1, m[blk:], stat)
            stat = jnp.where(lane == ATTN_HEADS + 2 * p, l_rep[:blk], stat)
            stat = jnp.where(lane == ATTN_HEADS + 2 * p + 1, l_rep[blk:], stat)
        stat_ref[cls, rows, :] = stat


def _attention_group(qkv, dilation, batch, seq):
    n = seq // dilation
    tq = min(ATTN_TQ, n)
    n_cls = ATTN_TQ // tq
    per = tq // ATTN_BLK
    pairs = ATTN_HEADS // 2

    def cur(which):
        return pl.BlockSpec((None, None, n_cls, tq, D_MODEL), lambda b, r, i: (which, b, r, i, 0))

    def prev(which):
        return pl.BlockSpec((None, None, n_cls, ATTN_BLK, D_MODEL),
                            lambda b, r, i: (which, b, r, jnp.maximum(i * per - 1, 0), 0))

    return pl.pallas_call(
        _attn_kernel,
        out_shape=(jax.ShapeDtypeStruct((batch, dilation, pairs, n, LANES), F32),
                   jax.ShapeDtypeStruct((batch, dilation, n, LANES), F32)),
        grid=(batch, dilation // n_cls, n // tq),
        in_specs=[cur(0), prev(1), cur(1), prev(2), cur(2)],
        out_specs=(pl.BlockSpec((None, n_cls, pairs, tq, LANES), lambda b, r, i: (b, r, 0, i, 0)),
                   pl.BlockSpec((None, n_cls, tq, LANES), lambda b, r, i: (b, r, i, 0))),
        compiler_params=pltpu.CompilerParams(
            dimension_semantics=("arbitrary", "arbitrary", "arbitrary"),
            vmem_limit_bytes=VMEM_LIMIT),
        name=f"dilated_attn_d{dilation}",
    )(qkv, qkv, qkv, qkv, qkv)


def _class_rows(ref, lead, dilation, r16, sub, n_sub):
    step = MAX_DILATION // dilation
    if step == 1:
        return ref[(r16, *lead, slice(sub * n_sub, (sub + 1) * n_sub))]
    start = sub * n_sub * step + r16 // dilation
    return ref[(r16 % dilation, *lead, pl.ds(start, n_sub, stride=step), slice(None))]


def _attn_out_chain(sub, o_refs, s_refs, x_ref, w_ref, ex_ref, g_ref, b_ref, y_ref, proj_ref):
    n_sub = OUT_SUB // MAX_DILATION
    n_chunks = D_MODEL // LANES
    classes = range(MAX_DILATION)

    ms = [jnp.concatenate([_class_rows(ref, (), d, r, sub, n_sub) for r in classes], axis=0)
          for ref, d in s_refs]
    ls = [pltpu.roll(v, LANES - ATTN_HEADS, 1) for v in ms]
    mx = jnp.maximum(jnp.maximum(ms[0], ms[1]), ms[2])
    es = [jnp.exp2(v - mx) for v in ms]
    inv = 1.0 / (ls[0] * es[0] + ls[1] * es[1] + ls[2] * es[2])
    head_lane = lax.broadcasted_iota(jnp.int32, (OUT_SUB, LANES), 1) < ATTN_HEADS
    halves = []
    for e in es:
        w = jnp.where(head_lane, e * inv, 0.0)
        hi = w.astype(BF16)
        halves.append(jnp.concatenate([hi, (w - hi.astype(F32)).astype(BF16)], axis=1))
    pieces = []
    for pp in range(ATTN_HEADS // 4):
        cols = slice(2 * pp * LANES, (2 * pp + 2) * LANES)
        mixed = jnp.zeros((OUT_SUB, 2 * LANES), F32)
        for g, (ref, d) in enumerate(o_refs):
            w_wide = jnp.dot(halves[g], ex_ref[:, cols], preferred_element_type=F32)
            o_g = jnp.concatenate(
                [jnp.concatenate([_class_rows(ref, (p,), d, r, sub, n_sub) for r in classes], axis=0)
                 for p in (2 * pp, 2 * pp + 1)], axis=1)
            mixed = mixed + w_wide * o_g
        pieces.append(mixed.astype(BF16))
    mix = jnp.concatenate(pieces, axis=1)
    yield
    proj = jnp.dot(mix, w_ref[...], preferred_element_type=F32)
    pitch = proj_ref.shape[2] // MAX_DILATION
    for c in range(n_chunks):
        for r in classes:
            proj_ref[sub, c, r * pitch:r * pitch + n_sub, :] = proj[r * n_sub:(r + 1) * n_sub,
                                                                    c * LANES:(c + 1) * LANES]
    yield
    g = g_ref[...]
    b = b_ref[...]
    for n in range(n_sub):
        lo = sub * OUT_SUB + n * MAX_DILATION
        tok = slice(lo, lo + MAX_DILATION)
        y = jnp.concatenate([proj_ref[sub, c, pl.ds(n, MAX_DILATION, stride=pitch), :]
                             for c in range(n_chunks)], axis=1)
        y_ref[tok, :] = _layer_norm(DEEPNORM_ALPHA * x_ref[tok, :] + y, g, b)


def _attn_out_kernel(layer, o0_ref, o1_ref, o2_ref, s0_ref, s1_ref, s2_ref, x_ref, w_hbm, ex_ref,
                     g_ref, b_ref, y_ref, proj_ref, o0s_ref, s0s_ref, w_ref, stage_ref, sem_ref):
    tm = x_ref.shape[0]

    @pl.when(_first_step(2))
    def _():
        _load_plain_weights(w_hbm, (layer,), w_ref, stage_ref, sem_ref)

    pairs = ATTN_HEADS // 2
    mid = DILATED_PATTERNS[1][1]
    assert [d for _, d in DILATED_PATTERNS] == [1, mid, MAX_DILATION]

    for r in range(mid):
        s0s_ref[r, :, :] = s0_ref[0, pl.ds(r, tm // mid, stride=mid), :]
        for p in range(pairs):
            o0s_ref[r, p, :, :] = o0_ref[0, p, pl.ds(r, tm // mid, stride=mid), :]
    o_refs = ((o0s_ref, mid), (o1_ref, mid), (o2_ref, MAX_DILATION))
    s_refs = ((s0s_ref, mid), (s1_ref, mid), (s2_ref, MAX_DILATION))
    _run_staggered([_attn_out_chain(sub, o_refs, s_refs, x_ref, w_ref, ex_ref, g_ref, b_ref, y_ref,
                                    proj_ref) for sub in range(tm // OUT_SUB)], 1)


def _head_expansion():
    e = (np.arange(D_MODEL)[None, :] // ATTN_HEAD_DIM == np.arange(LANES)[:, None])
    return jnp.asarray(np.concatenate([e, e], axis=0).astype(np.float32), dtype=BF16)


def _attn_out(os_, stats, x2d, w_out, layer, ln_g, ln_b, batch, seq):
    t = x2d.shape[0]
    tm = OUT_TM
    tiles = seq // tm
    pairs = ATTN_HEADS // 2
    dils = [d for _, d in DILATED_PATTERNS]
    o_spec = lambda d: pl.BlockSpec((None, d, pairs, tm // d, LANES), lambda b, i: (b, 0, 0, i, 0))
    s_spec = lambda d: pl.BlockSpec((None, d, tm // d, LANES), lambda b, i: (b, 0, i, 0))
    row = pl.BlockSpec((tm, D_MODEL), lambda b, i: (b * tiles + i, 0))
    return pl.pallas_call(
        functools.partial(_attn_out_kernel, layer),
        out_shape=jax.ShapeDtypeStruct((t, D_MODEL), F32),
        grid=(batch, tiles),
        in_specs=([o_spec(d) for d in dils] + [s_spec(d) for d in dils]
                  + [row, _HBM, _resident((2 * LANES, D_MODEL)),
                     _resident((1, D_MODEL)), _resident((1, D_MODEL))]),
        out_specs=row,
        scratch_shapes=[pltpu.VMEM((tm // OUT_SUB, D_MODEL // LANES,
                                    OUT_SUB + SUBLANES * MAX_DILATION, LANES), F32),
                        pltpu.VMEM((dils[1], pairs, tm // dils[1], LANES), F32),
                        pltpu.VMEM((dils[1], tm // dils[1], LANES), F32),
                        pltpu.VMEM((D_MODEL, D_MODEL), BF16)] + _weight_scratch(),
        compiler_params=pltpu.CompilerParams(
            dimension_semantics=("arbitrary", "arbitrary"), vmem_limit_bytes=VMEM_LIMIT),
        name="attn_out_ln",
    )(*os_, *stats, x2d, w_out, _head_expansion(), ln_g, ln_b)


def _ffn_kernel(layer, x_ref, wu_hbm, wd_hbm, g_ref, b_ref, y_ref, h_ref, wu_ref, wd_ref,
                stage_ref, sem_ref):
    @pl.when(_first_step(1))
    def _():
        _load_plain_weights(wu_hbm, (layer,), wu_ref, stage_ref, sem_ref)
        _load_plain_weights(wd_hbm, (layer,), wd_ref, stage_ref, sem_ref)

    for t in range(x_ref.shape[0] // FFN_SUB):
        rows = slice(t * FFN_SUB, (t + 1) * FFN_SUB)
        x = x_ref[rows, :]
        xb = x.astype(BF16)
        for c in range(D_FF // D_MODEL):
            sl = slice(c * D_MODEL, (c + 1) * D_MODEL)
            h = jnp.dot(xb, wu_ref[:, sl], preferred_element_type=F32)
            h_ref[rows, sl] = jnp.square(jnp.maximum(h, 0.0)).astype(BF16)
        y = jnp.dot(h_ref[rows, :], wd_ref[...], preferred_element_type=F32)
        y_ref[rows, :] = _layer_norm(DEEPNORM_ALPHA * x + y, g_ref[...], b_ref[...])


def _ffn(x2d, layer, w_up, w_down, ln_g, ln_b):
    t = x2d.shape[0]
    tm = FFN_TM
    return pl.pallas_call(
        functools.partial(_ffn_kernel, layer),
        out_shape=jax.ShapeDtypeStruct((t, D_MODEL), F32),
        grid=(t // tm,),
        in_specs=[pl.BlockSpec((tm, D_MODEL), lambda i: (i, 0)),
                  _HBM, _HBM, _resident((1, D_MODEL)), _resident((1, D_MODEL))],
        out_specs=pl.BlockSpec((tm, D_MODEL), lambda i: (i, 0)),
        scratch_shapes=[pltpu.VMEM((tm, D_FF), BF16), pltpu.VMEM((D_MODEL, D_FF), BF16),
                        pltpu.VMEM((D_FF, D_MODEL), BF16)] + _weight_scratch(),
        compiler_params=pltpu.CompilerParams(
            dimension_semantics=("arbitrary",), vmem_limit_bytes=VMEM_LIMIT),
        name="ffn_ln",
    )(x2d, w_up, w_down, ln_g, ln_b)


def _hgrn_pair_chain(xb, wi_ref, p, lb, tri, block_causal, ng, states, on_ref, out_rows, t, done):
    hk = HGRN_HEADS * HGRN_DK
    dk = HGRN_DK
    c_len = HGRN_CHUNK
    pw = 2 * dk
    rows = xb.shape[0]
    n_chunks = rows // c_len
    cols = slice(p * pw, (p + 1) * pw)
    contract_last = (((1,), (1,)), ((), ()))
    contract_rows = (((0,), (0,)), ((), ()))

    q_raw = jnp.dot(xb, wi_ref[:, p * pw:(p + 1) * pw], preferred_element_type=F32)
    z = jnp.dot(xb, wi_ref[:, hk + p * pw:hk + (p + 1) * pw], preferred_element_type=F32)
    v = jnp.dot(xb, wi_ref[:, 2 * hk + p * pw:2 * hk + (p + 1) * pw], preferred_element_type=F32)
    yield
    lb_p = lb[:, cols]
    e_neg = jnp.exp(-jnp.abs(z))
    r = 1.0 / (1.0 + e_neg)
    pos = z >= 0.0
    key = (1.0 - lb_p) * jnp.where(pos, e_neg * r, r)
    log_f = jnp.log(lb_p + (1.0 - lb_p) * jnp.where(pos, r, e_neg * r))
    q = q_raw / (1.0 + jnp.exp(-q_raw))
    v_b = v.astype(BF16)
    hi = log_f.astype(BF16)
    lo = (log_f - hi.astype(F32)).astype(BF16)
    yield
    bcum = (jnp.dot(tri, hi, preferred_element_type=F32)
            + jnp.dot(tri, lo, preferred_element_type=F32))
    yield
    last = [bcum[(c + 1) * c_len - 1:(c + 1) * c_len] for c in range(n_chunks)]
    b_last = jnp.concatenate([jnp.broadcast_to(r, (c_len, pw)) for r in last], axis=0)
    q_dec = (q * jnp.exp(bcum)).astype(BF16)
    k_dec = (key * jnp.exp(-bcum)).astype(BF16)
    k_end = (key * jnp.exp(b_last - bcum)).astype(BF16)
    yield
    scores = [lax.dot_general(q_dec[:, hh * dk:(hh + 1) * dk], k_dec[:, hh * dk:(hh + 1) * dk],
                              contract_last, preferred_element_type=F32) for hh in range(2)]
    yield
    intra = [jnp.dot(jnp.where(block_causal, scores[hh], 0.0).astype(BF16),
                     v_b[:, hh * dk:(hh + 1) * dk], preferred_element_type=F32) for hh in range(2)]
    yield
    zero_st = jnp.zeros((HGRN_DV, dk), BF16)
    zero_k = jnp.zeros((c_len, dk), BF16)
    kvs = []
    for c in range(n_chunks):
        rs = slice(c * c_len, (c + 1) * c_len)
        v_rows = jnp.concatenate([v_b[rs, :dk], v_b[rs, dk:]], axis=0)
        k_rows = jnp.concatenate(
            [jnp.concatenate([k_end[rs, :dk], zero_k], axis=1),
             jnp.concatenate([zero_k, k_end[rs, dk:]], axis=1)], axis=0)
        kvs.append(lax.dot_general(v_rows, k_rows, contract_rows,
                                   preferred_element_type=F32))
        if c % 2 == 1:
            yield
    while t > 0 and (t - 1, p) not in done:
        yield
    st0, st1 = states[2 * p], states[2 * p + 1]
    inter = []
    for c in range(n_chunks):
        rs = slice(c * c_len, (c + 1) * c_len)
        st_pair = jnp.concatenate(
            [jnp.concatenate([st0.astype(BF16), zero_st], axis=1),
             jnp.concatenate([zero_st, st1.astype(BF16)], axis=1)], axis=0)
        inter.append(lax.dot_general(q_dec[rs], st_pair, contract_last,
                                     preferred_element_type=F32))
        decay = jnp.exp(last[c])
        st0 = decay[:, :dk] * st0 + kvs[c][:, :dk]
        st1 = decay[:, dk:] * st1 + kvs[c][:, dk:]
        yield
    states[2 * p], states[2 * p + 1] = st0, st1
    inter = jnp.concatenate(inter, axis=0)
    outs = []
    for hh in range(2):
        ls = slice(hh * dk, (hh + 1) * dk)
        o = intra[hh] + inter[:, ls]
        o = o * lax.rsqrt(jnp.mean(o * o, axis=-1, keepdims=True) + RMS_EPS) * ng[:, cols][:, ls]
        outs.append(o.astype(BF16))
    on_ref[out_rows, cols] = jnp.concatenate(outs, axis=1)
    done.add((t, p))


def _hgrn_out_chain(x_ref, on_ref, wo_ref, g_ref, b_ref, y_ref, rows, t, done):
    while any((t, p) not in done for p in range(HGRN_HEADS // 2)):
        yield
    y = jnp.dot(on_ref[rows, :], wo_ref[...], preferred_element_type=F32)
    yield
    y_ref[rows, :] = _layer_norm(DEEPNORM_ALPHA * x_ref[rows, :] + y, g_ref[...], b_ref[...])


def _hgrn_kernel(layer, w_layer, sub, x_ref, wi_hbm, wo_hbm, lbl_ref, ng_ref, g_ref, b_ref, y_ref,
                 state_ref, on_ref, wi_ref, wo_ref, stage_ref, sem_ref):
    tm = x_ref.shape[0]
    c_len = HGRN_CHUNK

    @pl.when(_first_step(2))
    def _():
        _load_plain_weights(wi_hbm, (w_layer,), wi_ref, stage_ref, sem_ref)
        _load_plain_weights(wo_hbm, (w_layer,), wo_ref, stage_ref, sem_ref)

    @pl.when(pl.program_id(1) == 0)
    def _():
        state_ref[...] = jnp.zeros_like(state_ref)

    logits = lbl_ref[...]
    ex = jnp.exp(logits - jnp.max(logits, axis=0, keepdims=True))
    sm = ex / jnp.sum(ex, axis=0, keepdims=True)
    lb = jnp.sum(sm[1:layer + 1], axis=0, keepdims=True)

    ri = lax.broadcasted_iota(jnp.int32, (sub, sub), 0)
    ci = lax.broadcasted_iota(jnp.int32, (sub, sub), 1)
    block_causal = (ri // c_len == ci // c_len) & (ci <= ri)
    tri = block_causal.astype(F32).astype(BF16)
    ng = ng_ref[...]
    states = [state_ref[h] for h in range(HGRN_HEADS)]

    chains, done = [], set()
    for t in range(tm // sub):
        rows = slice(t * sub, (t + 1) * sub)
        xb = x_ref[rows, :].astype(BF16)
        for p in range(HGRN_HEADS // 2):
            chains.append(_hgrn_pair_chain(xb, wi_ref, p, lb, tri, block_causal, ng, states,
                                           on_ref, rows, t, done))
        chains.append(_hgrn_out_chain(x_ref, on_ref, wo_ref, g_ref, b_ref, y_ref, rows, t, done))
    _run_staggered(chains, HGRN_STAGGER)
    for h in range(HGRN_HEADS):
        state_ref[h] = states[h]


def _hgrn_mixer(layer, w_layer, x2d, w_in, w_out, lb_logits, norm_g, ln_g, ln_b, batch, seq):
    t = x2d.shape[0]
    tm = HGRN_TM
    tiles = seq // tm
    d_in = w_in.shape[-1]
    row = pl.BlockSpec((tm, D_MODEL), lambda b, i: (b * tiles + i, 0))
    return pl.pallas_call(
        functools.partial(_hgrn_kernel, layer, w_layer, HGRN_SUB),
        out_shape=jax.ShapeDtypeStruct((t, D_MODEL), F32),
        grid=(batch, tiles),
        in_specs=[row, _HBM, _HBM, _resident((DEPTH, D_MODEL)), _resident((1, D_MODEL)),
                  _resident((1, D_MODEL)), _resident((1, D_MODEL))],
        out_specs=row,
        scratch_shapes=[pltpu.VMEM((HGRN_HEADS, HGRN_DV, HGRN_DK), F32),
                        pltpu.VMEM((tm, D_MODEL), BF16), pltpu.VMEM((D_MODEL, d_in), BF16),
                        pltpu.VMEM((D_MODEL, D_MODEL), BF16)] + _weight_scratch(),
        compiler_params=pltpu.CompilerParams(
            dimension_semantics=("arbitrary", "arbitrary"), vmem_limit_bytes=VMEM_LIMIT),
        name="hgrn2_mixer_ln",
    )(x2d, w_in, w_out, lb_logits, norm_g, ln_g, ln_b)


def kernel(x, attn_w_in, attn_w_out, hgrn_w_in, hgrn_w_out, hgrn_norm_g, lb_logits,
           ln_mix_g, ln_mix_b, ln_ffn_g, ln_ffn_b, ffn_w_up, ffn_w_down):
    batch, seq, d = x.shape
    assert d == D_MODEL and lb_logits.shape[0] == DEPTH
    for window, dilation in DILATED_PATTERNS:
        assert window // dilation == ATTN_BLK and seq % window == 0
        assert QKV_TM % (dilation * BF16_ROWS) == 0 and OUT_TM % (dilation * SUBLANES) == 0
        assert dilation % max(1, ATTN_TQ // (seq // dilation)) == 0
    assert seq % QKV_TM == 0 and seq % OUT_TM == 0 and OUT_TM % OUT_SUB == 0
    assert OUT_SUB % (MAX_DILATION * SUBLANES) == 0
    assert seq % HGRN_TM == 0 and HGRN_TM % HGRN_SUB == 0 and HGRN_SUB % HGRN_CHUNK == 0
    h = x.reshape(batch * seq, d)
    row = lambda a: a.reshape(1, -1)
    for i in range(DEPTH):
        j = i // 2
        if i % 2 == 0:
            os_, stats = [], []
            for g, (_, dil) in enumerate(DILATED_PATTERNS):
                qkv = _qkv_rope(h, attn_w_in, j, g, dil, batch, seq)
                o, st = _attention_group(qkv, dil, batch, seq)
                os_.append(o)
                stats.append(st)
            h = _attn_out(os_, stats, h, attn_w_out, j, row(ln_mix_g[i]), row(ln_mix_b[i]),
                          batch, seq)
        else:
            h = _hgrn_mixer(i, j, h, hgrn_w_in, hgrn_w_out, lb_logits, row(hgrn_norm_g[j]),
                            row(ln_mix_g[i]), row(ln_mix_b[i]), batch, seq)
        h = _ffn(h, i, ffn_w_up, ffn_w_down, row(ln_ffn_g[i]), row(ln_ffn_b[i]))
    return h.reshape(batch, seq, d)
```

```python
import functools
import math

import jax
import jax.numpy as jnp
from jax import lax
from jax.experimental import pallas as pl
from jax.experimental.pallas import tpu as pltpu
import numpy as np

F32 = jnp.float32
BF16 = jnp.bfloat16

D_MODEL = 1024
DEPTH = 2
ATTN_HEAD_DIM = 64
ATTN_HEADS = D_MODEL // ATTN_HEAD_DIM
DILATED_PATTERNS = ((128, 1), (512, 4), (2048, 16))
MAX_DILATION = max(d for _, d in DILATED_PATTERNS)
ROPE_THETA = 10000.0
HGRN_HEADS = 8
HGRN_DK = 128
HGRN_DV = 128
HGRN_CHUNK = 64
D_FF = 4 * D_MODEL
LN_EPS = 1e-5
RMS_EPS = 1e-6
DEEPNORM_ALPHA = (2 * DEPTH) ** 0.25

LANES = 128
SUBLANES = 8
BF16_ROWS = 2 * SUBLANES
ATTN_BLK = 128
HALF = ATTN_HEAD_DIM // 2
MASK_VALUE = -1e30
LN2 = math.log(2.0)
Q_SCALE = ATTN_HEAD_DIM ** -0.5 / LN2
VMEM_LIMIT = 56 * 1024 * 1024
MAX_ROW_STRIDE = 4
W_CHUNK_ROWS = 1024
W_CHUNK_COLS = 256
W_SLOTS = 4

QKV_TM = 1024
QKV_SUB = 512
ATTN_TQ = 1024
OUT_TM = 512
OUT_SUB = 256
FFN_TM = 1024
FFN_SUB = 256
HGRN_TM = 512
HGRN_SUB = 256
HGRN_STAGGER = 1


def _layer_norm(y, g, b):
    mu = jnp.mean(y, axis=-1, keepdims=True)
    d = y - mu
    var = jnp.mean(d * d, axis=-1, keepdims=True)
    return d * lax.rsqrt(var + LN_EPS) * g + b


def _resident(shape):
    nd = len(shape)
    return pl.BlockSpec(shape, lambda *_: (0,) * nd, pipeline_mode=pl.Buffered(1))


_HBM = pl.BlockSpec(memory_space=pl.ANY)


def _first_step(grid_rank):
    ids = [pl.program_id(a) == 0 for a in range(grid_rank)]
    return functools.reduce(jnp.logical_and, ids)


def _run_staggered(chains, stagger):
    active, pending, tick = [], list(chains), 0
    while active or pending:
        if pending and tick % stagger == 0:
            active.append(pending.pop(0))
        tick += 1
        for gen in list(active):
            try:
                next(gen)
            except StopIteration:
                active.remove(gen)


def _weight_scratch():
    return [pltpu.VMEM((W_SLOTS, W_CHUNK_ROWS, W_CHUNK_COLS), F32),
            pltpu.SemaphoreType.DMA((W_SLOTS,))]


def _load_weights(chunks, stage_ref, sem_ref, store):
    copies = [pltpu.make_async_copy(src, stage_ref.at[i % W_SLOTS], sem_ref.at[i % W_SLOTS])
              for i, src in enumerate(chunks)]
    ahead = W_SLOTS - 1
    for cp in copies[:ahead]:
        cp.start()
    for i, cp in enumerate(copies):
        if i + ahead < len(copies):
            copies[i + ahead].start()
        cp.wait()
        store(i, stage_ref[i % W_SLOTS])


def _weight_chunks(w_hbm, lead, n_rows, col0, n_cols):
    out = []
    for r in range(0, n_rows, W_CHUNK_ROWS):
        for c in range(0, n_cols, W_CHUNK_COLS):
            view = w_hbm.at[(*lead, pl.ds(r, W_CHUNK_ROWS), pl.ds(col0 + c, W_CHUNK_COLS))]
            out.append((view, r, c))
    return out


def _load_plain_weights(w_hbm, lead, w_ref, stage_ref, sem_ref):
    chunks = _weight_chunks(w_hbm, lead, w_ref.shape[0], 0, w_ref.shape[1])

    def store(i, val):
        _, r, c = chunks[i]
        w_ref[r:r + W_CHUNK_ROWS, c:c + W_CHUNK_COLS] = val.astype(BF16)

    _load_weights([v for v, _, _ in chunks], stage_ref, sem_ref, store)


def _row_passes(dilation):
    passes, left = [], dilation
    while left > 1:
        passes.append(min(left, MAX_ROW_STRIDE))
        left //= passes[-1]
    return passes


def _qkv_rope_kernel(dilation, layer, g, *refs):
    n_chunks = D_MODEL // LANES
    n_x = 1 if dilation == 1 else n_chunks
    x_refs = refs[:n_x]
    w_hbm, tab_ref, o_ref, xb_ref, xs_ref, w_ref, stage_ref, sem_ref = refs[n_x:]
    tm = x_refs[0].shape[0]
    n_per = tm // dilation

    @pl.when(_first_step(2))
    def _():
        chunks = _weight_chunks(w_hbm, (layer,), D_MODEL, 3 * g * D_MODEL, 3 * D_MODEL)
        lane = lax.broadcasted_iota(jnp.int32, (1, LANES), 1)
        from_right = (lane >= HALF) & (lane < 2 * HALF)
        from_left = (lane >= 2 * HALF) & (lane < 3 * HALF)

        def store(i, val):
            _, _, c = chunks[i]
            if c < 2 * D_MODEL:
                parts = []
                for j in range(W_CHUNK_COLS // LANES):
                    a = val[:, j * LANES:(j + 1) * LANES]
                    parts.append(jnp.where(from_right, pltpu.roll(a, LANES - HALF, 1),
                                           jnp.where(from_left, pltpu.roll(a, HALF, 1), a)))
                val = jnp.concatenate(parts, axis=1)
            w_ref[:, c:c + W_CHUNK_COLS] = val.astype(BF16)

        _load_weights([v for v, _, _ in chunks], stage_ref, sem_ref, store)

    if dilation == 1:
        xb_ref[...] = x_refs[0][...].astype(BF16)
    else:
        passes = _row_passes(dilation)
        n_slabs = xs_ref.shape[0]
        blocks = 1
        for i, st in enumerate(passes):
            rows_blk = tm // blocks
            for blk in range(blocks):
                for r in range(st):
                    lo = (blk + r * blocks) * (rows_blk // st)
                    dst = slice(lo, lo + rows_blk // st)
                    rows = pl.ds(blk * rows_blk + r, rows_blk // st, stride=st)
                    parts = [x_refs[c][rows, :] if i == 0 else xs_ref[(i - 1) % n_slabs, c, rows, :]
                             for c in range(n_chunks)]
                    if i == len(passes) - 1:
                        xb_ref[dst, :] = jnp.concatenate([v.astype(BF16) for v in parts], axis=1)
                    else:
                        for c in range(n_chunks):
                            xs_ref[i % n_slabs, c, dst, :] = parts[c]
            blocks *= st

    def store(kind, s, val):
        if n_per >= QKV_SUB:
            start = s * QKV_SUB
            o_ref[kind, start // n_per, start % n_per:start % n_per + QKV_SUB, :] = val
        else:
            per = QKV_SUB // n_per
            for c in range(per):
                o_ref[kind, s * per + c] = val[c * n_per:(c + 1) * n_per]

    for kind in range(3):
        cols = slice(kind * D_MODEL, (kind + 1) * D_MODEL)
        for s in range(tm // QKV_SUB):
            rows = slice(s * QKV_SUB, (s + 1) * QKV_SUB)
            acc = jnp.dot(xb_ref[rows], w_ref[:, cols], preferred_element_type=F32)
            if kind == 2:
                store(kind, s, acc.astype(BF16))
                continue
            cos = tab_ref[0, rows, :]
            sin = tab_ref[1, rows, :]
            if kind == 0:
                cos = cos * Q_SCALE
                sin = sin * Q_SCALE
            pieces = []
            for c in range(n_chunks):
                a = acc[:, c * LANES:(c + 1) * LANES]
                pieces.append((a * cos + pltpu.roll(a, LANES // 2, 1) * sin).astype(BF16))
            store(kind, s, jnp.concatenate(pieces, axis=1))


def _rope_table(seq, dilation, tm):
    inv = ROPE_THETA ** (-np.arange(HALF, dtype=np.float64) * (2.0 / ATTN_HEAD_DIM))
    ang = np.arange(seq, dtype=np.float64)[:, None] * inv[None, :]
    cos = np.tile(np.cos(ang), (1, LANES // HALF))
    sin = np.tile(np.sin(ang), (1, LANES // HALF))
    sign = np.where(np.arange(LANES) < LANES // 2, -1.0, 1.0)
    tab = np.stack([cos, sin * sign])
    tab = tab.reshape(2, seq // tm, tm // dilation, dilation, LANES)
    tab = tab.transpose(0, 1, 3, 2, 4).reshape(2, seq, LANES)
    return jnp.asarray(tab.astype(np.float32))


def _qkv_rope(x2d, w_in, layer, g, dilation, batch, seq):
    tm = QKV_TM
    tiles = seq // tm
    n_per = tm // dilation
    tab = _rope_table(seq, dilation, tm)
    if dilation == 1:
        x_specs = [pl.BlockSpec((tm, D_MODEL), lambda b, i: (b * tiles + i, 0))]
    else:
        x_specs = [pl.BlockSpec((tm, LANES), functools.partial(lambda b, i, c: (b * tiles + i, c), c=c))
                   for c in range(D_MODEL // LANES)]
    n_slabs = max(1, len(_row_passes(dilation)) - 1)
    return pl.pallas_call(
        functools.partial(_qkv_rope_kernel, dilation, layer, g),
        out_shape=jax.ShapeDtypeStruct((3, batch, dilation, seq // dilation, D_MODEL), BF16),
        grid=(batch, tiles),
        in_specs=x_specs + [_HBM, pl.BlockSpec((2, tm, LANES), lambda b, i: (0, i, 0))],
        out_specs=pl.BlockSpec((3, None, dilation, n_per, D_MODEL), lambda b, i: (0, b, 0, i, 0)),
        scratch_shapes=[pltpu.VMEM((tm, D_MODEL), BF16),
                        pltpu.VMEM((n_slabs, D_MODEL // LANES, tm, LANES), F32),
                        pltpu.VMEM((D_MODEL, 3 * D_MODEL), BF16)] + _weight_scratch(),
        compiler_params=pltpu.CompilerParams(
            dimension_semantics=("arbitrary", "arbitrary"), vmem_limit_bytes=VMEM_LIMIT),
        name=f"qkv_rope_d{dilation}",
    )(*([x2d] * len(x_specs)), w_in, tab)


def _attn_kernel(q_ref, kp_ref, kc_ref, vp_ref, vc_ref, o_ref, stat_ref):
    i = pl.program_id(2)
    blk = ATTN_BLK
    n_cls, tq = q_ref.shape[0], q_ref.shape[1]
    row = lax.broadcasted_iota(jnp.int32, (2 * blk, 2 * blk), 0) % blk
    col = lax.broadcasted_iota(jnp.int32, (2 * blk, 2 * blk), 1)
    valid = (col >= row) & (col <= row + blk)
    bias = jnp.where(valid, 0.0, MASK_VALUE).astype(F32)
    bias_first = jnp.where(valid & ((col >= blk) | (i > 0)), 0.0, MASK_VALUE).astype(F32)
    lane = lax.broadcasted_iota(jnp.int32, (blk, LANES), 1)
    qk_head0 = ((lane // HALF) % 2 == 0).astype(F32).astype(BF16)
    qk_head1 = ((lane // HALF) % 2 == 1).astype(F32).astype(BF16)
    v_head0 = lane < ATTN_HEAD_DIM
    ones = jnp.ones((2 * blk, LANES), BF16)
    for cls, qb in [(c, b) for c in range(n_cls) for b in range(tq // blk)]:
        rows = slice(qb * blk, (qb + 1) * blk)
        prev_rows = slice((qb - 1) * blk, qb * blk)
        stat = jnp.zeros((blk, LANES), F32)
        for p in range(ATTN_HEADS // 2):
            sl = slice(p * LANES, (p + 1) * LANES)
            q = q_ref[cls, rows, sl]
            qs = jnp.concatenate([q * qk_head0, q * qk_head1], axis=0)
            k_prev = kp_ref[cls, :, sl] if qb == 0 else kc_ref[cls, prev_rows, sl]
            v_prev = vp_ref[cls, :, sl] if qb == 0 else vc_ref[cls, prev_rows, sl]
            k = jnp.concatenate([k_prev, kc_ref[cls, rows, sl]], axis=0)
            v = jnp.concatenate([v_prev, vc_ref[cls, rows, sl]], axis=0)
            s = lax.dot_general(qs, k, (((1,), (1,)), ((), ())), preferred_element_type=F32)
            s = s + (bias_first if qb == 0 else bias)
            m = jnp.max(s, axis=-1, keepdims=True)
            e = jnp.exp2(s - m).astype(BF16)
            pv = jnp.dot(e, jnp.concatenate([v, ones], axis=1),
                         preferred_element_type=F32)
            l_rep = pv[:, LANES:]
            o_ref[cls, p, rows, :] = jnp.where(v_head0, pv[:blk, :LANES], pv[blk:, :LANES])
            stat = jnp.where(lane == 2 * p, m[:blk], stat)
            stat = jnp.where(lane == 2 * p + 1, m[blk:], stat)
            stat = jnp.where(lane == ATTN_HEADS + 2 * p, l_rep[:blk], stat)
            stat = jnp.where(lane == ATTN_HEADS + 2 * p + 1, l_rep[blk:], stat)
        stat_ref[cls, rows, :] = stat


def _attention_group(qkv, dilation, batch, seq):
    n = seq // dilation
    tq = min(ATTN_TQ, n)
    n_cls = ATTN_TQ // tq
    per = tq // ATTN_BLK
    pairs = ATTN_HEADS // 2

    def cur(which):
        return pl.BlockSpec((None, None, n_cls, tq, D_MODEL), lambda b, r, i: (which, b, r, i, 0))

    def prev(which):
        return pl.BlockSpec((None, None, n_cls, ATTN_BLK, D_MODEL),
                            lambda b, r, i: (which, b, r, jnp.maximum(i * per - 1, 0), 0))

    return pl.pallas_call(
        _attn_kernel,
        out_shape=(jax.ShapeDtypeStruct((batch, dilation, pairs, n, LANES), F32),
                   jax.ShapeDtypeStruct((batch, dilation, n, LANES), F32)),
        grid=(batch, dilation // n_cls, n // tq),
        in_specs=[cur(0), prev(1), cur(1), prev(2), cur(2)],
        out_specs=(pl.BlockSpec((None, n_cls, pairs, tq, LANES), lambda b, r, i: (b, r, 0, i, 0)),
                   pl.BlockSpec((None, n_cls, tq, LANES), lambda b, r, i: (b, r, i, 0))),
        compiler_params=pltpu.CompilerParams(
            dimension_semantics=("arbitrary", "arbitrary", "arbitrary"),
            vmem_limit_bytes=VMEM_LIMIT),
        name=f"dilated_attn_d{dilation}",
    )(qkv, qkv, qkv, qkv, qkv)


def _class_rows(ref, lead, dilation, r16, sub, n_sub):
    step = MAX_DILATION // dilation
    if step == 1:
        return ref[(r16, *lead, slice(sub * n_sub, (sub + 1) * n_sub))]
    start = sub * n_sub * step + r16 // dilation
    return ref[(r16 % dilation, *lead, pl.ds(start, n_sub, stride=step), slice(None))]


def _attn_out_chain(sub, o_refs, s_refs, x_ref, w_ref, ex_ref, g_ref, b_ref, y_ref, proj_ref):
    n_sub = OUT_SUB // MAX_DILATION
    n_chunks = D_MODEL // LANES
    classes = range(MAX_DILATION)

    ms = [jnp.concatenate([_class_rows(ref, (), d, r, sub, n_sub) for r in classes], axis=0)
          for ref, d in s_refs]
    ls = [pltpu.roll(v, LANES - ATTN_HEADS, 1) for v in ms]
    mx = jnp.maximum(jnp.maximum(ms[0], ms[1]), ms[2])
    es = [jnp.exp2(v - mx) for v in ms]
    inv = 1.0 / (ls[0] * es[0] + ls[1] * es[1] + ls[2] * es[2])
    head_lane = lax.broadcasted_iota(jnp.int32, (OUT_SUB, LANES), 1) < ATTN_HEADS
    halves = []
    for e in es:
        w = jnp.where(head_lane, e * inv, 0.0)
        hi = w.astype(BF16)
        halves.append(jnp.concatenate([hi, (w - hi.astype(F32)).astype(BF16)], axis=1))
    pieces = []
    for pp in range(ATTN_HEADS // 4):
        cols = slice(2 * pp * LANES, (2 * pp + 2) * LANES)
        mixed = jnp.zeros((OUT_SUB, 2 * LANES), F32)
        for g, (ref, d) in enumerate(o_refs):
            w_wide = jnp.dot(halves[g], ex_ref[:, cols], preferred_element_type=F32)
            o_g = jnp.concatenate(
                [jnp.concatenate([_class_rows(ref, (p,), d, r, sub, n_sub) for r in classes], axis=0)
                 for p in (2 * pp, 2 * pp + 1)], axis=1)
            mixed = mixed + w_wide * o_g
        pieces.append(mixed.astype(BF16))
    mix = jnp.concatenate(pieces, axis=1)
    yield
    proj = jnp.dot(mix, w_ref[...], preferred_element_type=F32)
    pitch = proj_ref.shape[2] // MAX_DILATION
    for c in range(n_chunks):
        for r in classes:
            proj_ref[sub, c, r * pitch:r * pitch + n_sub, :] = proj[r * n_sub:(r + 1) * n_sub,
                                                                    c * LANES:(c + 1) * LANES]
    yield
    g = g_ref[...]
    b = b_ref[...]
    for n in range(n_sub):
        lo = sub * OUT_SUB + n * MAX_DILATION
        tok = slice(lo, lo + MAX_DILATION)
        y = jnp.concatenate([proj_ref[sub, c, pl.ds(n, MAX_DILATION, stride=pitch), :]
                             for c in range(n_chunks)], axis=1)
        y_ref[tok, :] = _layer_norm(DEEPNORM_ALPHA * x_ref[tok, :] + y, g, b)


def _attn_out_kernel(layer, o0_ref, o1_ref, o2_ref, s0_ref, s1_ref, s2_ref, x_ref, w_hbm, ex_ref,
                     g_ref, b_ref, y_ref, proj_ref, o0s_ref, s0s_ref, w_ref, stage_ref, sem_ref):
    tm = x_ref.shape[0]

    @pl.when(_first_step(2))
    def _():
        _load_plain_weights(w_hbm, (layer,), w_ref, stage_ref, sem_ref)

    pairs = ATTN_HEADS // 2
    mid = DILATED_PATTERNS[1][1]
    assert [d for _, d in DILATED_PATTERNS] == [1, mid, MAX_DILATION]

    for r in range(mid):
        s0s_ref[r, :, :] = s0_ref[0, pl.ds(r, tm // mid, stride=mid), :]
        for p in range(pairs):
            o0s_ref[r, p, :, :] = o0_ref[0, p, pl.ds(r, tm // mid, stride=mid), :]
    o_refs = ((o0s_ref, mid), (o1_ref, mid), (o2_ref, MAX_DILATION))
    s_refs = ((s0s_ref, mid), (s1_ref, mid), (s2_ref, MAX_DILATION))
    _run_staggered([_attn_out_chain(sub, o_refs, s_refs, x_ref, w_ref, ex_ref, g_ref, b_ref, y_ref,
                                    proj_ref) for sub in range(tm // OUT_SUB)], 1)


def _head_expansion():
    e = (np.arange(D_MODEL)[None, :] // ATTN_HEAD_DIM == np.arange(LANES)[:, None])
    return jnp.asarray(np.concatenate([e, e], axis=0).astype(np.float32), dtype=BF16)


def _attn_out(os_, stats, x2d, w_out, layer, ln_g, ln_b, batch, seq):
    t = x2d.shape[0]
    tm = OUT_TM
    tiles = seq // tm
    pairs = ATTN_HEADS // 2
    dils = [d for _, d in DILATED_PATTERNS]
    o_spec = lambda d: pl.BlockSpec((None, d, pairs, tm // d, LANES), lambda b, i: (b, 0, 0, i, 0))
    s_spec = lambda d: pl.BlockSpec((None, d, tm // d, LANES), lambda b, i: (b, 0, i, 0))
    row = pl.BlockSpec((tm, D_MODEL), lambda b, i: (b * tiles + i, 0))
    return pl.pallas_call(
        functools.partial(_attn_out_kernel, layer),
        out_shape=jax.ShapeDtypeStruct((t, D_MODEL), F32),
        grid=(batch, tiles),
        in_specs=([o_spec(d) for d in dils] + [s_spec(d) for d in dils]
                  + [row, _HBM, _resident((2 * LANES, D_MODEL)),
                     _resident((1, D_MODEL)), _resident((1, D_MODEL))]),
        out_specs=row,
        scratch_shapes=[pltpu.VMEM((tm // OUT_SUB, D_MODEL // LANES,
                                    OUT_SUB + SUBLANES * MAX_DILATION, LANES), F32),
                        pltpu.VMEM((dils[1], pairs, tm // dils[1], LANES), F32),
                        pltpu.VMEM((dils[1], tm // dils[1], LANES), F32),
                        pltpu.VMEM((D_MODEL, D_MODEL), BF16)] + _weight_scratch(),
        compiler_params=pltpu.CompilerParams(
            dimension_semantics=("arbitrary", "arbitrary"), vmem_limit_bytes=VMEM_LIMIT),
        name="attn_out_ln",
    )(*os_, *stats, x2d, w_out, _head_expansion(), ln_g, ln_b)


def _ffn_kernel(layer, x_ref, wu_hbm, wd_hbm, g_ref, b_ref, y_ref, h_ref, wu_ref, wd_ref,
                stage_ref, sem_ref):
    @pl.when(_first_step(1))
    def _():
        _load_plain_weights(wu_hbm, (layer,), wu_ref, stage_ref, sem_ref)
        _load_plain_weights(wd_hbm, (layer,), wd_ref, stage_ref, sem_ref)

    for t in range(x_ref.shape[0] // FFN_SUB):
        rows = slice(t * FFN_SUB, (t + 1) * FFN_SUB)
        x = x_ref[rows, :]
        xb = x.astype(BF16)
        for c in range(D_FF // D_MODEL):
            sl = slice(c * D_MODEL, (c + 1) * D_MODEL)
            h = jnp.dot(xb, wu_ref[:, sl], preferred_element_type=F32)
            h_ref[rows, sl] = jnp.square(jnp.maximum(h, 0.0)).astype(BF16)
        y = jnp.dot(h_ref[rows, :], wd_ref[...], preferred_element_type=F32)
        y_ref[rows, :] = _layer_norm(DEEPNORM_ALPHA * x + y, g_ref[...], b_ref[...])


def _ffn(x2d, layer, w_up, w_down, ln_g, ln_b):
    t = x2d.shape[0]
    tm = FFN_TM
    return pl.pallas_call(
        functools.partial(_ffn_kernel, layer),
        out_shape=jax.ShapeDtypeStruct((t, D_MODEL), F32),
        grid=(t // tm,),
        in_specs=[pl.BlockSpec((tm, D_MODEL), lambda i: (i, 0)),
                  _HBM, _HBM, _resident((1, D_MODEL)), _resident((1, D_MODEL))],
        out_specs=pl.BlockSpec((tm, D_MODEL), lambda i: (i, 0)),
        scratch_shapes=[pltpu.VMEM((tm, D_FF), BF16), pltpu.VMEM((D_MODEL, D_FF), BF16),
                        pltpu.VMEM((D_FF, D_MODEL), BF16)] + _weight_scratch(),
        compiler_params=pltpu.CompilerParams(
            dimension_semantics=("arbitrary",), vmem_limit_bytes=VMEM_LIMIT),
        name="ffn_ln",
    )(x2d, w_up, w_down, ln_g, ln_b)


def _hgrn_pair_chain(xb, wi_ref, p, lb, tri, block_causal, ng, states, on_ref, out_rows, t, done):
    hk = HGRN_HEADS * HGRN_DK
    dk = HGRN_DK
    c_len = HGRN_CHUNK
    pw = 2 * dk
    rows = xb.shape[0]
    n_chunks = rows // c_len
    cols = slice(p * pw, (p + 1) * pw)
    contract_last = (((1,), (1,)), ((), ()))
    contract_rows = (((0,), (0,)), ((), ()))

    q_raw = jnp.dot(xb, wi_ref[:, p * pw:(p + 1) * pw], preferred_element_type=F32)
    z = jnp.dot(xb, wi_ref[:, hk + p * pw:hk + (p + 1) * pw], preferred_element_type=F32)
    v = jnp.dot(xb, wi_ref[:, 2 * hk + p * pw:2 * hk + (p + 1) * pw], preferred_element_type=F32)
    yield
    lb_p = lb[:, cols]
    key = (1.0 - lb_p) / (1.0 + jnp.exp(z))
    log_f = jnp.log(lb_p + (1.0 - lb_p) / (1.0 + jnp.exp(-z)))
    q = q_raw / (1.0 + jnp.exp(-q_raw))
    v_b = v.astype(BF16)
    hi = log_f.astype(BF16)
    lo = (log_f - hi.astype(F32)).astype(BF16)
    yield
    bcum = (jnp.dot(tri, hi, preferred_element_type=F32)
            + jnp.dot(tri, lo, preferred_element_type=F32))
    yield
    last = [bcum[(c + 1) * c_len - 1:(c + 1) * c_len] for c in range(n_chunks)]
    b_last = jnp.concatenate([jnp.broadcast_to(r, (c_len, pw)) for r in last], axis=0)
    q_dec = (q * jnp.exp(bcum)).astype(BF16)
    k_dec = (key * jnp.exp(-bcum)).astype(BF16)
    k_end = (key * jnp.exp(b_last - bcum)).astype(BF16)
    yield
    scores = [lax.dot_general(q_dec[:, hh * dk:(hh + 1) * dk], k_dec[:, hh * dk:(hh + 1) * dk],
                              contract_last, preferred_element_type=F32) for hh in range(2)]
    yield
    intra = [jnp.dot(jnp.where(block_causal, scores[hh], 0.0).astype(BF16),
                     v_b[:, hh * dk:(hh + 1) * dk], preferred_element_type=F32) for hh in range(2)]
    yield
    zero_st = jnp.zeros((HGRN_DV, dk), BF16)
    zero_k = jnp.zeros((c_len, dk), BF16)
    kvs = []
    for c in range(n_chunks):
        rs = slice(c * c_len, (c + 1) * c_len)
        v_rows = jnp.concatenate([v_b[rs, :dk], v_b[rs, dk:]], axis=0)
        k_rows = jnp.concatenate(
            [jnp.concatenate([k_end[rs, :dk], zero_k], axis=1),
             jnp.concatenate([zero_k, k_end[rs, dk:]], axis=1)], axis=0)
        kvs.append(lax.dot_general(v_rows, k_rows, contract_rows,
                                   preferred_element_type=F32))
        if c % 2 == 1:
            yield
    while t > 0 and (t - 1, p) not in done:
        yield
    st0, st1 = states[2 * p], states[2 * p + 1]
    inter = []
    for c in range(n_chunks):
        rs = slice(c * c_len, (c + 1) * c_len)
        st_pair = jnp.concatenate(
            [jnp.concatenate([st0.astype(BF16), zero_st], axis=1),
             jnp.concatenate([zero_st, st1.astype(BF16)], axis=1)], axis=0)
        inter.append(lax.dot_general(q_dec[rs], st_pair, contract_last,
                                     preferred_element_type=F32))
        decay = jnp.exp(last[c])
        st0 = decay[:, :dk] * st0 + kvs[c][:, :dk]
        st1 = decay[:, dk:] * st1 + kvs[c][:, dk:]
        yield
    states[2 * p], states[2 * p + 1] = st0, st1
    inter = jnp.concatenate(inter, axis=0)
    outs = []
    for hh in range(2):
        ls = slice(hh * dk, (hh + 1) * dk)
        o = intra[hh] + inter[:, ls]
        o = o * lax.rsqrt(jnp.mean(o * o, axis=-1, keepdims=True) + RMS_EPS) * ng[:, cols][:, ls]
        outs.append(o.astype(BF16))
    on_ref[out_rows, cols] = jnp.concatenate(outs, axis=1)
    done.add((t, p))


def _hgrn_out_chain(x_ref, on_ref, wo_ref, g_ref, b_ref, y_ref, rows, t, done):
    while any((t, p) not in done for p in range(HGRN_HEADS // 2)):
        yield
    y = jnp.dot(on_ref[rows, :], wo_ref[...], preferred_element_type=F32)
    yield
    y_ref[rows, :] = _layer_norm(DEEPNORM_ALPHA * x_ref[rows, :] + y, g_ref[...], b_ref[...])


def _hgrn_kernel(layer, w_layer, sub, x_ref, wi_hbm, wo_hbm, lbl_ref, ng_ref, g_ref, b_ref, y_ref,
                 state_ref, on_ref, wi_ref, wo_ref, stage_ref, sem_ref):
    tm = x_ref.shape[0]
    c_len = HGRN_CHUNK

    @pl.when(_first_step(2))
    def _():
        _load_plain_weights(wi_hbm, (w_layer,), wi_ref, stage_ref, sem_ref)
        _load_plain_weights(wo_hbm, (w_layer,), wo_ref, stage_ref, sem_ref)

    @pl.when(pl.program_id(1) == 0)
    def _():
        state_ref[...] = jnp.zeros_like(state_ref)

    logits = lbl_ref[...]
    ex = jnp.exp(logits - jnp.max(logits, axis=0, keepdims=True))
    sm = ex / jnp.sum(ex, axis=0, keepdims=True)
    lb = jnp.sum(sm[1:layer + 1], axis=0, keepdims=True)

    ri = lax.broadcasted_iota(jnp.int32, (sub, sub), 0)
    ci = lax.broadcasted_iota(jnp.int32, (sub, sub), 1)
    block_causal = (ri // c_len == ci // c_len) & (ci <= ri)
    tri = block_causal.astype(F32).astype(BF16)
    ng = ng_ref[...]
    states = [state_ref[h] for h in range(HGRN_HEADS)]

    chains, done = [], set()
    for t in range(tm // sub):
        rows = slice(t * sub, (t + 1) * sub)
        xb = x_ref[rows, :].astype(BF16)
        for p in range(HGRN_HEADS // 2):
            chains.append(_hgrn_pair_chain(xb, wi_ref, p, lb, tri, block_causal, ng, states,
                                           on_ref, rows, t, done))
        chains.append(_hgrn_out_chain(x_ref, on_ref, wo_ref, g_ref, b_ref, y_ref, rows, t, done))
    _run_staggered(chains, HGRN_STAGGER)
    for h in range(HGRN_HEADS):
        state_ref[h] = states[h]


def _hgrn_mixer(layer, w_layer, x2d, w_in, w_out, lb_logits, norm_g, ln_g, ln_b, batch, seq):
    t = x2d.shape[0]
    tm = HGRN_TM
    tiles = seq // tm
    d_in = w_in.shape[-1]
    row = pl.BlockSpec((tm, D_MODEL), lambda b, i: (b * tiles + i, 0))
    return pl.pallas_call(
        functools.partial(_hgrn_kernel, layer, w_layer, HGRN_SUB),
        out_shape=jax.ShapeDtypeStruct((t, D_MODEL), F32),
        grid=(batch, tiles),
        in_specs=[row, _HBM, _HBM, _resident((DEPTH, D_MODEL)), _resident((1, D_MODEL)),
                  _resident((1, D_MODEL)), _resident((1, D_MODEL))],
        out_specs=row,
        scratch_shapes=[pltpu.VMEM((HGRN_HEADS, HGRN_DV, HGRN_DK), F32),
                        pltpu.VMEM((tm, D_MODEL), BF16), pltpu.VMEM((D_MODEL, d_in), BF16),
                        pltpu.VMEM((D_MODEL, D_MODEL), BF16)] + _weight_scratch(),
        compiler_params=pltpu.CompilerParams(
            dimension_semantics=("arbitrary", "arbitrary"), vmem_limit_bytes=VMEM_LIMIT),
        name="hgrn2_mixer_ln",
    )(x2d, w_in, w_out, lb_logits, norm_g, ln_g, ln_b)


def kernel(x, attn_w_in, attn_w_out, hgrn_w_in, hgrn_w_out, hgrn_norm_g, lb_logits,
           ln_mix_g, ln_mix_b, ln_ffn_g, ln_ffn_b, ffn_w_up, ffn_w_down):
    batch, seq, d = x.shape
    assert d == D_MODEL and lb_logits.shape[0] == DEPTH
    for window, dilation in DILATED_PATTERNS:
        assert window // dilation == ATTN_BLK and seq % window == 0
        assert QKV_TM % (dilation * BF16_ROWS) == 0 and OUT_TM % (dilation * SUBLANES) == 0
        assert dilation % max(1, ATTN_TQ // (seq // dilation)) == 0
    assert seq % QKV_TM == 0 and seq % OUT_TM == 0 and OUT_TM % OUT_SUB == 0
    assert OUT_SUB % (MAX_DILATION * SUBLANES) == 0
    assert seq % HGRN_TM == 0 and HGRN_TM % HGRN_SUB == 0 and HGRN_SUB % HGRN_CHUNK == 0
    h = x.reshape(batch * seq, d)
    row = lambda a: a.reshape(1, -1)
    for i in range(DEPTH):
        j = i // 2
        if i % 2 == 0:
            os_, stats = [], []
            for g, (_, dil) in enumerate(DILATED_PATTERNS):
                qkv = _qkv_rope(h, attn_w_in, j, g, dil, batch, seq)
                o, st = _attention_group(qkv, dil, batch, seq)
                os_.append(o)
                stats.append(st)
            h = _attn_out(os_, stats, h, attn_w_out, j, row(ln_mix_g[i]), row(ln_mix_b[i]),
                          batch, seq)
        else:
            h = _hgrn_mixer(i, j, h, hgrn_w_in, hgrn_w_out, lb_logits, row(hgrn_norm_g[j]),
                            row(ln_mix_g[i]), row(ln_mix_b[i]), batch, seq)
        h = _ffn(h, i, ffn_w_up, ffn_w_down, row(ln_ffn_g[i]), row(ln_ffn_b[i]))
    return h.reshape(batch, seq, d)
```

```python
import functools
import math

import jax
import jax.numpy as jnp
from jax import lax
from jax.experimental import pallas as pl
from jax.experimental.pallas import tpu as pltpu
import numpy as np

F32 = jnp.float32
BF16 = jnp.bfloat16

D_MODEL = 1024
DEPTH = 2
ATTN_HEAD_DIM = 64
ATTN_HEADS = D_MODEL // ATTN_HEAD_DIM
DILATED_PATTERNS = ((128, 1), (512, 4), (2048, 16))
MAX_DILATION = max(d for _, d in DILATED_PATTERNS)
ROPE_THETA = 10000.0
HGRN_HEADS = 8
HGRN_DK = 128
HGRN_DV = 128
HGRN_CHUNK = 64
D_FF = 4 * D_MODEL
LN_EPS = 1e-5
RMS_EPS = 1e-6
DEEPNORM_ALPHA = (2 * DEPTH) ** 0.25

LANES = 128
SUBLANES = 8
BF16_ROWS = 2 * SUBLANES
ATTN_BLK = 128
HALF = ATTN_HEAD_DIM // 2
MASK_VALUE = -1e30
LN2 = math.log(2.0)
Q_SCALE = ATTN_HEAD_DIM ** -0.5 / LN2
VMEM_LIMIT = 56 * 1024 * 1024
MAX_ROW_STRIDE = 4
W_CHUNK_ROWS = 1024
W_CHUNK_COLS = 256
W_SLOTS = 4

QKV_TM = 1024
QKV_SUB = 512
ATTN_TQ = 1024
OUT_TM = 512
OUT_SUB = 256
FFN_TM = 1024
FFN_SUB = 256
HGRN_TM = 1024
HGRN_SUB = 256
HGRN_STAGGER = 1


def _layer_norm(y, g, b):
    mu = jnp.mean(y, axis=-1, keepdims=True)
    d = y - mu
    var = jnp.mean(d * d, axis=-1, keepdims=True)
    return d * lax.rsqrt(var + LN_EPS) * g + b


def _resident(shape):
    nd = len(shape)
    return pl.BlockSpec(shape, lambda *_: (0,) * nd, pipeline_mode=pl.Buffered(1))


_HBM = pl.BlockSpec(memory_space=pl.ANY)


def _first_step(grid_rank):
    ids = [pl.program_id(a) == 0 for a in range(grid_rank)]
    return functools.reduce(jnp.logical_and, ids)


def _run_staggered(chains, stagger):
    active, pending, tick = [], list(chains), 0
    while active or pending:
        if pending and tick % stagger == 0:
            active.append(pending.pop(0))
        tick += 1
        for gen in list(active):
            try:
                next(gen)
            except StopIteration:
                active.remove(gen)


def _weight_scratch():
    return [pltpu.VMEM((W_SLOTS, W_CHUNK_ROWS, W_CHUNK_COLS), F32),
            pltpu.SemaphoreType.DMA((W_SLOTS,))]


def _load_weights(chunks, stage_ref, sem_ref, store):
    copies = [pltpu.make_async_copy(src, stage_ref.at[i % W_SLOTS], sem_ref.at[i % W_SLOTS])
              for i, src in enumerate(chunks)]
    ahead = W_SLOTS - 1
    for cp in copies[:ahead]:
        cp.start()
    for i, cp in enumerate(copies):
        if i + ahead < len(copies):
            copies[i + ahead].start()
        cp.wait()
        store(i, stage_ref[i % W_SLOTS])


def _weight_chunks(w_hbm, lead, n_rows, col0, n_cols):
    out = []
    for r in range(0, n_rows, W_CHUNK_ROWS):
        for c in range(0, n_cols, W_CHUNK_COLS):
            view = w_hbm.at[(*lead, pl.ds(r, W_CHUNK_ROWS), pl.ds(col0 + c, W_CHUNK_COLS))]
            out.append((view, r, c))
    return out


def _load_plain_weights(w_hbm, lead, w_ref, stage_ref, sem_ref):
    chunks = _weight_chunks(w_hbm, lead, w_ref.shape[0], 0, w_ref.shape[1])

    def store(i, val):
        _, r, c = chunks[i]
        w_ref[r:r + W_CHUNK_ROWS, c:c + W_CHUNK_COLS] = val.astype(BF16)

    _load_weights([v for v, _, _ in chunks], stage_ref, sem_ref, store)


def _row_passes(dilation):
    passes, left = [], dilation
    while left > 1:
        passes.append(min(left, MAX_ROW_STRIDE))
        left //= passes[-1]
    return passes


def _qkv_rope_kernel(dilation, layer, g, *refs):
    n_chunks = D_MODEL // LANES
    n_x = 1 if dilation == 1 else n_chunks
    x_refs = refs[:n_x]
    w_hbm, tab_ref, o_ref, xb_ref, xs_ref, w_ref, stage_ref, sem_ref = refs[n_x:]
    tm = x_refs[0].shape[0]
    n_per = tm // dilation

    @pl.when(_first_step(2))
    def _():
        chunks = _weight_chunks(w_hbm, (layer,), D_MODEL, 3 * g * D_MODEL, 3 * D_MODEL)
        lane = lax.broadcasted_iota(jnp.int32, (1, LANES), 1)
        from_right = (lane >= HALF) & (lane < 2 * HALF)
        from_left = (lane >= 2 * HALF) & (lane < 3 * HALF)

        def store(i, val):
            _, _, c = chunks[i]
            if c < 2 * D_MODEL:
                parts = []
                for j in range(W_CHUNK_COLS // LANES):
                    a = val[:, j * LANES:(j + 1) * LANES]
                    parts.append(jnp.where(from_right, pltpu.roll(a, LANES - HALF, 1),
                                           jnp.where(from_left, pltpu.roll(a, HALF, 1), a)))
                val = jnp.concatenate(parts, axis=1)
            w_ref[:, c:c + W_CHUNK_COLS] = val.astype(BF16)

        _load_weights([v for v, _, _ in chunks], stage_ref, sem_ref, store)

    if dilation == 1:
        xb_ref[...] = x_refs[0][...].astype(BF16)
    else:
        passes = _row_passes(dilation)
        n_slabs = xs_ref.shape[0]
        blocks = 1
        for i, st in enumerate(passes):
            rows_blk = tm // blocks
            for blk in range(blocks):
                for r in range(st):
                    lo = (blk + r * blocks) * (rows_blk // st)
                    dst = slice(lo, lo + rows_blk // st)
                    rows = pl.ds(blk * rows_blk + r, rows_blk // st, stride=st)
                    parts = [x_refs[c][rows, :] if i == 0 else xs_ref[(i - 1) % n_slabs, c, rows, :]
                             for c in range(n_chunks)]
                    if i == len(passes) - 1:
                        xb_ref[dst, :] = jnp.concatenate([v.astype(BF16) for v in parts], axis=1)
                    else:
                        for c in range(n_chunks):
                            xs_ref[i % n_slabs, c, dst, :] = parts[c]
            blocks *= st

    def store(kind, s, val):
        if n_per >= QKV_SUB:
            start = s * QKV_SUB
            o_ref[kind, start // n_per, start % n_per:start % n_per + QKV_SUB, :] = val
        else:
            per = QKV_SUB // n_per
            for c in range(per):
                o_ref[kind, s * per + c] = val[c * n_per:(c + 1) * n_per]

    for kind in range(3):
        cols = slice(kind * D_MODEL, (kind + 1) * D_MODEL)
        for s in range(tm // QKV_SUB):
            rows = slice(s * QKV_SUB, (s + 1) * QKV_SUB)
            acc = jnp.dot(xb_ref[rows], w_ref[:, cols], preferred_element_type=F32)
            if kind == 2:
                store(kind, s, acc.astype(BF16))
                continue
            cos = tab_ref[0, rows, :]
            sin = tab_ref[1, rows, :]
            if kind == 0:
                cos = cos * Q_SCALE
                sin = sin * Q_SCALE
            pieces = []
            for c in range(n_chunks):
                a = acc[:, c * LANES:(c + 1) * LANES]
                pieces.append((a * cos + pltpu.roll(a, LANES // 2, 1) * sin).astype(BF16))
            store(kind, s, jnp.concatenate(pieces, axis=1))


def _rope_table(seq, dilation, tm):
    inv = ROPE_THETA ** (-np.arange(HALF, dtype=np.float64) * (2.0 / ATTN_HEAD_DIM))
    ang = np.arange(seq, dtype=np.float64)[:, None] * inv[None, :]
    cos = np.tile(np.cos(ang), (1, LANES // HALF))
    sin = np.tile(np.sin(ang), (1, LANES // HALF))
    sign = np.where(np.arange(LANES) < LANES // 2, -1.0, 1.0)
    tab = np.stack([cos, sin * sign])
    tab = tab.reshape(2, seq // tm, tm // dilation, dilation, LANES)
    tab = tab.transpose(0, 1, 3, 2, 4).reshape(2, seq, LANES)
    return jnp.asarray(tab.astype(np.float32))


def _qkv_rope(x2d, w_in, layer, g, dilation, batch, seq):
    tm = QKV_TM
    tiles = seq // tm
    n_per = tm // dilation
    tab = _rope_table(seq, dilation, tm)
    if dilation == 1:
        x_specs = [pl.BlockSpec((tm, D_MODEL), lambda b, i: (b * tiles + i, 0))]
    else:
        x_specs = [pl.BlockSpec((tm, LANES), functools.partial(lambda b, i, c: (b * tiles + i, c), c=c))
                   for c in range(D_MODEL // LANES)]
    n_slabs = max(1, len(_row_passes(dilation)) - 1)
    return pl.pallas_call(
        functools.partial(_qkv_rope_kernel, dilation, layer, g),
        out_shape=jax.ShapeDtypeStruct((3, batch, dilation, seq // dilation, D_MODEL), BF16),
        grid=(batch, tiles),
        in_specs=x_specs + [_HBM, pl.BlockSpec((2, tm, LANES), lambda b, i: (0, i, 0))],
        out_specs=pl.BlockSpec((3, None, dilation, n_per, D_MODEL), lambda b, i: (0, b, 0, i, 0)),
        scratch_shapes=[pltpu.VMEM((tm, D_MODEL), BF16),
                        pltpu.VMEM((n_slabs, D_MODEL // LANES, tm, LANES), F32),
                        pltpu.VMEM((D_MODEL, 3 * D_MODEL), BF16)] + _weight_scratch(),
        compiler_params=pltpu.CompilerParams(
            dimension_semantics=("arbitrary", "arbitrary"), vmem_limit_bytes=VMEM_LIMIT),
        name=f"qkv_rope_d{dilation}",
    )(*([x2d] * len(x_specs)), w_in, tab)


def _attn_kernel(q_ref, kp_ref, kc_ref, vp_ref, vc_ref, o_ref, stat_ref):
    i = pl.program_id(2)
    blk = ATTN_BLK
    n_cls, tq = q_ref.shape[0], q_ref.shape[1]
    row = lax.broadcasted_iota(jnp.int32, (2 * blk, 2 * blk), 0) % blk
    col = lax.broadcasted_iota(jnp.int32, (2 * blk, 2 * blk), 1)
    valid = (col >= row) & (col <= row + blk)
    bias = jnp.where(valid, 0.0, MASK_VALUE).astype(F32)
    bias_first = jnp.where(valid & ((col >= blk) | (i > 0)), 0.0, MASK_VALUE).astype(F32)
    lane = lax.broadcasted_iota(jnp.int32, (blk, LANES), 1)
    qk_head0 = ((lane // HALF) % 2 == 0).astype(F32).astype(BF16)
    qk_head1 = ((lane // HALF) % 2 == 1).astype(F32).astype(BF16)
    v_head0 = lane < ATTN_HEAD_DIM
    ones = jnp.ones((2 * blk, LANES), BF16)
    for cls, qb in [(c, b) for c in range(n_cls) for b in range(tq // blk)]:
        rows = slice(qb * blk, (qb + 1) * blk)
        prev_rows = slice((qb - 1) * blk, qb * blk)
        stat = jnp.zeros((blk, LANES), F32)
        for p in range(ATTN_HEADS // 2):
            sl = slice(p * LANES, (p + 1) * LANES)
            q = q_ref[cls, rows, sl]
            qs = jnp.concatenate([q * qk_head0, q * qk_head1], axis=0)
            k_prev = kp_ref[cls, :, sl] if qb == 0 else kc_ref[cls, prev_rows, sl]
            v_prev = vp_ref[cls, :, sl] if qb == 0 else vc_ref[cls, prev_rows, sl]
            k = jnp.concatenate([k_prev, kc_ref[cls, rows, sl]], axis=0)
            v = jnp.concatenate([v_prev, vc_ref[cls, rows, sl]], axis=0)
            s = lax.dot_general(qs, k, (((1,), (1,)), ((), ())), preferred_element_type=F32)
            s = s + (bias_first if qb == 0 else bias)
            m = jnp.max(s, axis=-1, keepdims=True)
            e = jnp.exp2(s - m).astype(BF16)
            pv = jnp.dot(e, jnp.concatenate([v, ones], axis=1),
                         preferred_element_type=F32)
            l_rep = pv[:, LANES:]
            o_ref[cls, p, rows, :] = jnp.where(v_head0, pv[:blk, :LANES], pv[blk:, :LANES])
            stat = jnp.where(lane == 2 * p, m[:blk], stat)
            stat = jnp.where(lane == 2 * p + 1, m[blk:], stat)
            stat = jnp.where(lane == ATTN_HEADS + 2 * p, l_rep[:blk], stat)
            stat = jnp.where(lane == ATTN_HEADS + 2 * p + 1, l_rep[blk:], stat)
        stat_ref[cls, rows, :] = stat


def _attention_group(qkv, dilation, batch, seq):
    n = seq // dilation
    tq = min(ATTN_TQ, n)
    n_cls = ATTN_TQ // tq
    per = tq // ATTN_BLK
    pairs = ATTN_HEADS // 2

    def cur(which):
        return pl.BlockSpec((None, None, n_cls, tq, D_MODEL), lambda b, r, i: (which, b, r, i, 0))

    def prev(which):
        return pl.BlockSpec((None, None, n_cls, ATTN_BLK, D_MODEL),
                            lambda b, r, i: (which, b, r, jnp.maximum(i * per - 1, 0), 0))

    return pl.pallas_call(
        _attn_kernel,
        out_shape=(jax.ShapeDtypeStruct((batch, dilation, pairs, n, LANES), F32),
                   jax.ShapeDtypeStruct((batch, dilation, n, LANES), F32)),
        grid=(batch, dilation // n_cls, n // tq),
        in_specs=[cur(0), prev(1), cur(1), prev(2), cur(2)],
        out_specs=(pl.BlockSpec((None, n_cls, pairs, tq, LANES), lambda b, r, i: (b, r, 0, i, 0)),
                   pl.BlockSpec((None, n_cls, tq, LANES), lambda b, r, i: (b, r, i, 0))),
        compiler_params=pltpu.CompilerParams(
            dimension_semantics=("arbitrary", "arbitrary", "arbitrary"),
            vmem_limit_bytes=VMEM_LIMIT),
        name=f"dilated_attn_d{dilation}",
    )(qkv, qkv, qkv, qkv, qkv)


def _class_rows(ref, lead, dilation, r16, sub, n_sub):
    step = MAX_DILATION // dilation
    if step == 1:
        return ref[(r16, *lead, slice(sub * n_sub, (sub + 1) * n_sub))]
    start = sub * n_sub * step + r16 // dilation
    return ref[(r16 % dilation, *lead, pl.ds(start, n_sub, stride=step), slice(None))]


def _attn_out_chain(sub, o_refs, s_refs, x_ref, w_ref, ex_ref, g_ref, b_ref, y_ref, proj_ref):
    n_sub = OUT_SUB // MAX_DILATION
    n_chunks = D_MODEL // LANES
    classes = range(MAX_DILATION)

    ms = [jnp.concatenate([_class_rows(ref, (), d, r, sub, n_sub) for r in classes], axis=0)
          for ref, d in s_refs]
    ls = [pltpu.roll(v, LANES - ATTN_HEADS, 1) for v in ms]
    mx = jnp.maximum(jnp.maximum(ms[0], ms[1]), ms[2])
    es = [jnp.exp2(v - mx) for v in ms]
    inv = 1.0 / (ls[0] * es[0] + ls[1] * es[1] + ls[2] * es[2])
    head_lane = lax.broadcasted_iota(jnp.int32, (OUT_SUB, LANES), 1) < ATTN_HEADS
    halves = []
    for e in es:
        w = jnp.where(head_lane, e * inv, 0.0)
        hi = w.astype(BF16)
        halves.append(jnp.concatenate([hi, (w - hi.astype(F32)).astype(BF16)], axis=1))
    pieces = []
    for pp in range(ATTN_HEADS // 4):
        cols = slice(2 * pp * LANES, (2 * pp + 2) * LANES)
        mixed = jnp.zeros((OUT_SUB, 2 * LANES), F32)
        for g, (ref, d) in enumerate(o_refs):
            w_wide = jnp.dot(halves[g], ex_ref[:, cols], preferred_element_type=F32)
            o_g = jnp.concatenate(
                [jnp.concatenate([_class_rows(ref, (p,), d, r, sub, n_sub) for r in classes], axis=0)
                 for p in (2 * pp, 2 * pp + 1)], axis=1)
            mixed = mixed + w_wide * o_g
        pieces.append(mixed.astype(BF16))
    mix = jnp.concatenate(pieces, axis=1)
    yield
    proj = jnp.dot(mix, w_ref[...], preferred_element_type=F32)
    pitch = proj_ref.shape[2] // MAX_DILATION
    for c in range(n_chunks):
        for r in classes:
            proj_ref[sub, c, r * pitch:r * pitch + n_sub, :] = proj[r * n_sub:(r + 1) * n_sub,
                                                                    c * LANES:(c + 1) * LANES]
    yield
    g = g_ref[...]
    b = b_ref[...]
    for n in range(n_sub):
        lo = sub * OUT_SUB + n * MAX_DILATION
        tok = slice(lo, lo + MAX_DILATION)
        y = jnp.concatenate([proj_ref[sub, c, pl.ds(n, MAX_DILATION, stride=pitch), :]
                             for c in range(n_chunks)], axis=1)
        y_ref[tok, :] = _layer_norm(DEEPNORM_ALPHA * x_ref[tok, :] + y, g, b)


def _attn_out_kernel(layer, o0_ref, o1_ref, o2_ref, s0_ref, s1_ref, s2_ref, x_ref, w_hbm, ex_ref,
                     g_ref, b_ref, y_ref, proj_ref, o0s_ref, s0s_ref, w_ref, stage_ref, sem_ref):
    tm = x_ref.shape[0]

    @pl.when(_first_step(2))
    def _():
        _load_plain_weights(w_hbm, (layer,), w_ref, stage_ref, sem_ref)

    pairs = ATTN_HEADS // 2
    mid = DILATED_PATTERNS[1][1]
    assert [d for _, d in DILATED_PATTERNS] == [1, mid, MAX_DILATION]

    for r in range(mid):
        s0s_ref[r, :, :] = s0_ref[0, pl.ds(r, tm // mid, stride=mid), :]
        for p in range(pairs):
            o0s_ref[r, p, :, :] = o0_ref[0, p, pl.ds(r, tm // mid, stride=mid), :]
    o_refs = ((o0s_ref, mid), (o1_ref, mid), (o2_ref, MAX_DILATION))
    s_refs = ((s0s_ref, mid), (s1_ref, mid), (s2_ref, MAX_DILATION))
    _run_staggered([_attn_out_chain(sub, o_refs, s_refs, x_ref, w_ref, ex_ref, g_ref, b_ref, y_ref,
                                    proj_ref) for sub in range(tm // OUT_SUB)], 1)


def _head_expansion():
    e = (np.arange(D_MODEL)[None, :] // ATTN_HEAD_DIM == np.arange(LANES)[:, None])
    return jnp.asarray(np.concatenate([e, e], axis=0).astype(np.float32), dtype=BF16)


def _attn_out(os_, stats, x2d, w_out, layer, ln_g, ln_b, batch, seq):
    t = x2d.shape[0]
    tm = OUT_TM
    tiles = seq // tm
    pairs = ATTN_HEADS // 2
    dils = [d for _, d in DILATED_PATTERNS]
    o_spec = lambda d: pl.BlockSpec((None, d, pairs, tm // d, LANES), lambda b, i: (b, 0, 0, i, 0))
    s_spec = lambda d: pl.BlockSpec((None, d, tm // d, LANES), lambda b, i: (b, 0, i, 0))
    row = pl.BlockSpec((tm, D_MODEL), lambda b, i: (b * tiles + i, 0))
    return pl.pallas_call(
        functools.partial(_attn_out_kernel, layer),
        out_shape=jax.ShapeDtypeStruct((t, D_MODEL), F32),
        grid=(batch, tiles),
        in_specs=([o_spec(d) for d in dils] + [s_spec(d) for d in dils]
                  + [row, _HBM, _resident((2 * LANES, D_MODEL)),
                     _resident((1, D_MODEL)), _resident((1, D_MODEL))]),
        out_specs=row,
        scratch_shapes=[pltpu.VMEM((tm // OUT_SUB, D_MODEL // LANES,
                                    OUT_SUB + SUBLANES * MAX_DILATION, LANES), F32),
                        pltpu.VMEM((dils[1], pairs, tm // dils[1], LANES), F32),
                        pltpu.VMEM((dils[1], tm // dils[1], LANES), F32),
                        pltpu.VMEM((D_MODEL, D_MODEL), BF16)] + _weight_scratch(),
        compiler_params=pltpu.CompilerParams(
            dimension_semantics=("arbitrary", "arbitrary"), vmem_limit_bytes=VMEM_LIMIT),
        name="attn_out_ln",
    )(*os_, *stats, x2d, w_out, _head_expansion(), ln_g, ln_b)


def _ffn_kernel(layer, x_ref, wu_hbm, wd_hbm, g_ref, b_ref, y_ref, h_ref, wu_ref, wd_ref,
                stage_ref, sem_ref):
    @pl.when(_first_step(1))
    def _():
        _load_plain_weights(wu_hbm, (layer,), wu_ref, stage_ref, sem_ref)
        _load_plain_weights(wd_hbm, (layer,), wd_ref, stage_ref, sem_ref)

    for t in range(x_ref.shape[0] // FFN_SUB):
        rows = slice(t * FFN_SUB, (t + 1) * FFN_SUB)
        x = x_ref[rows, :]
        xb = x.astype(BF16)
        for c in range(D_FF // D_MODEL):
            sl = slice(c * D_MODEL, (c + 1) * D_MODEL)
            h = jnp.dot(xb, wu_ref[:, sl], preferred_element_type=F32)
            h_ref[rows, sl] = jnp.square(jnp.maximum(h, 0.0)).astype(BF16)
        y = jnp.dot(h_ref[rows, :], wd_ref[...], preferred_element_type=F32)
        y_ref[rows, :] = _layer_norm(DEEPNORM_ALPHA * x + y, g_ref[...], b_ref[...])


def _ffn(x2d, layer, w_up, w_down, ln_g, ln_b):
    t = x2d.shape[0]
    tm = FFN_TM
    return pl.pallas_call(
        functools.partial(_ffn_kernel, layer),
        out_shape=jax.ShapeDtypeStruct((t, D_MODEL), F32),
        grid=(t // tm,),
        in_specs=[pl.BlockSpec((tm, D_MODEL), lambda i: (i, 0)),
                  _HBM, _HBM, _resident((1, D_MODEL)), _resident((1, D_MODEL))],
        out_specs=pl.BlockSpec((tm, D_MODEL), lambda i: (i, 0)),
        scratch_shapes=[pltpu.VMEM((tm, D_FF), BF16), pltpu.VMEM((D_MODEL, D_FF), BF16),
                        pltpu.VMEM((D_FF, D_MODEL), BF16)] + _weight_scratch(),
        compiler_params=pltpu.CompilerParams(
            dimension_semantics=("arbitrary",), vmem_limit_bytes=VMEM_LIMIT),
        name="ffn_ln",
    )(x2d, w_up, w_down, ln_g, ln_b)


def _hgrn_pair_chain(xb, wi_ref, p, lb, tri, block_causal, ng, states, on_ref, out_rows, t, done):
    hk = HGRN_HEADS * HGRN_DK
    dk = HGRN_DK
    c_len = HGRN_CHUNK
    pw = 2 * dk
    rows = xb.shape[0]
    n_chunks = rows // c_len
    cols = slice(p * pw, (p + 1) * pw)
    contract_last = (((1,), (1,)), ((), ()))
    contract_rows = (((0,), (0,)), ((), ()))

    q_raw = jnp.dot(xb, wi_ref[:, p * pw:(p + 1) * pw], preferred_element_type=F32)
    z = jnp.dot(xb, wi_ref[:, hk + p * pw:hk + (p + 1) * pw], preferred_element_type=F32)
    v = jnp.dot(xb, wi_ref[:, 2 * hk + p * pw:2 * hk + (p + 1) * pw], preferred_element_type=F32)
    yield
    lb_p = lb[:, cols]
    key = (1.0 - lb_p) / (1.0 + jnp.exp(z))
    log_f = jnp.log(lb_p + (1.0 - lb_p) / (1.0 + jnp.exp(-z)))
    q = q_raw / (1.0 + jnp.exp(-q_raw))
    v_b = v.astype(BF16)
    hi = log_f.astype(BF16)
    lo = (log_f - hi.astype(F32)).astype(BF16)
    yield
    bcum = (jnp.dot(tri, hi, preferred_element_type=F32)
            + jnp.dot(tri, lo, preferred_element_type=F32))
    yield
    last = [bcum[(c + 1) * c_len - 1:(c + 1) * c_len] for c in range(n_chunks)]
    b_last = jnp.concatenate([jnp.broadcast_to(r, (c_len, pw)) for r in last], axis=0)
    q_dec = (q * jnp.exp(bcum)).astype(BF16)
    k_dec = (key * jnp.exp(-bcum)).astype(BF16)
    k_end = (key * jnp.exp(b_last - bcum)).astype(BF16)
    yield
    scores = [lax.dot_general(q_dec[:, hh * dk:(hh + 1) * dk], k_dec[:, hh * dk:(hh + 1) * dk],
                              contract_last, preferred_element_type=F32) for hh in range(2)]
    yield
    intra = [jnp.dot(jnp.where(block_causal, scores[hh], 0.0).astype(BF16),
                     v_b[:, hh * dk:(hh + 1) * dk], preferred_element_type=F32) for hh in range(2)]
    yield
    zero_st = jnp.zeros((HGRN_DV, dk), BF16)
    zero_k = jnp.zeros((c_len, dk), BF16)
    kvs = []
    for c in range(n_chunks):
        rs = slice(c * c_len, (c + 1) * c_len)
        v_rows = jnp.concatenate([v_b[rs, :dk], v_b[rs, dk:]], axis=0)
        k_rows = jnp.concatenate(
            [jnp.concatenate([k_end[rs, :dk], zero_k], axis=1),
             jnp.concatenate([zero_k, k_end[rs, dk:]], axis=1)], axis=0)
        kvs.append(lax.dot_general(v_rows, k_rows, contract_rows,
                                   preferred_element_type=F32))
        if c % 2 == 1:
            yield
    while t > 0 and (t - 1, p) not in done:
        yield
    st0, st1 = states[2 * p], states[2 * p + 1]
    inter = []
    for c in range(n_chunks):
        rs = slice(c * c_len, (c + 1) * c_len)
        st_pair = jnp.concatenate(
            [jnp.concatenate([st0.astype(BF16), zero_st], axis=1),
             jnp.concatenate([zero_st, st1.astype(BF16)], axis=1)], axis=0)
        inter.append(lax.dot_general(q_dec[rs], st_pair, contract_last,
                                     preferred_element_type=F32))
        decay = jnp.exp(last[c])
        st0 = decay[:, :dk] * st0 + kvs[c][:, :dk]
        st1 = decay[:, dk:] * st1 + kvs[c][:, dk:]
        yield
    states[2 * p], states[2 * p + 1] = st0, st1
    inter = jnp.concatenate(inter, axis=0)
    outs = []
    for hh in range(2):
        ls = slice(hh * dk, (hh + 1) * dk)
        o = intra[hh] + inter[:, ls]
        o = o * lax.rsqrt(jnp.mean(o * o, axis=-1, keepdims=True) + RMS_EPS) * ng[:, cols][:, ls]
        outs.append(o.astype(BF16))
    on_ref[out_rows, cols] = jnp.concatenate(outs, axis=1)
    done.add((t, p))


def _hgrn_out_chain(x_ref, on_ref, wo_ref, g_ref, b_ref, y_ref, rows, t, done):
    while any((t, p) not in done for p in range(HGRN_HEADS // 2)):
        yield
    y = jnp.dot(on_ref[rows, :], wo_ref[...], preferred_element_type=F32)
    yield
    y_ref[rows, :] = _layer_norm(DEEPNORM_ALPHA * x_ref[rows, :] + y, g_ref[...], b_ref[...])


def _hgrn_kernel(layer, w_layer, sub, x_ref, wi_hbm, wo_hbm, lbl_ref, ng_ref, g_ref, b_ref, y_ref,
                 state_ref, on_ref, wi_ref, wo_ref, stage_ref, sem_ref):
    tm = x_ref.shape[0]
    c_len = HGRN_CHUNK

    @pl.when(_first_step(2))
    def _():
        _load_plain_weights(wi_hbm, (w_layer,), wi_ref, stage_ref, sem_ref)
        _load_plain_weights(wo_hbm, (w_layer,), wo_ref, stage_ref, sem_ref)

    @pl.when(pl.program_id(1) == 0)
    def _():
        state_ref[...] = jnp.zeros_like(state_ref)

    logits = lbl_ref[...]
    ex = jnp.exp(logits - jnp.max(logits, axis=0, keepdims=True))
    sm = ex / jnp.sum(ex, axis=0, keepdims=True)
    lb = jnp.sum(sm[1:layer + 1], axis=0, keepdims=True)

    ri = lax.broadcasted_iota(jnp.int32, (sub, sub), 0)
    ci = lax.broadcasted_iota(jnp.int32, (sub, sub), 1)
    block_causal = (ri // c_len == ci // c_len) & (ci <= ri)
    tri = block_causal.astype(F32).astype(BF16)
    ng = ng_ref[...]
    states = [state_ref[h] for h in range(HGRN_HEADS)]

    chains, done = [], set()
    for t in range(tm // sub):
        rows = slice(t * sub, (t + 1) * sub)
        xb = x_ref[rows, :].astype(BF16)
        for p in range(HGRN_HEADS // 2):
            chains.append(_hgrn_pair_chain(xb, wi_ref, p, lb, tri, block_causal, ng, states,
                                           on_ref, rows, t, done))
        chains.append(_hgrn_out_chain(x_ref, on_ref, wo_ref, g_ref, b_ref, y_ref, rows, t, done))
    _run_staggered(chains, HGRN_STAGGER)
    for h in range(HGRN_HEADS):
        state_ref[h] = states[h]


def _hgrn_mixer(layer, w_layer, x2d, w_in, w_out, lb_logits, norm_g, ln_g, ln_b, batch, seq):
    t = x2d.shape[0]
    tm = HGRN_TM
    tiles = seq // tm
    d_in = w_in.shape[-1]
    row = pl.BlockSpec((tm, D_MODEL), lambda b, i: (b * tiles + i, 0))
    return pl.pallas_call(
        functools.partial(_hgrn_kernel, layer, w_layer, HGRN_SUB),
        out_shape=jax.ShapeDtypeStruct((t, D_MODEL), F32),
        grid=(batch, tiles),
        in_specs=[row, _HBM, _HBM, _resident((DEPTH, D_MODEL)), _resident((1, D_MODEL)),
                  _resident((1, D_MODEL)), _resident((1, D_MODEL))],
        out_specs=row,
        scratch_shapes=[pltpu.VMEM((HGRN_HEADS, HGRN_DV, HGRN_DK), F32),
                        pltpu.VMEM((tm, D_MODEL), BF16), pltpu.VMEM((D_MODEL, d_in), BF16),
                        pltpu.VMEM((D_MODEL, D_MODEL), BF16)] + _weight_scratch(),
        compiler_params=pltpu.CompilerParams(
            dimension_semantics=("arbitrary", "arbitrary"), vmem_limit_bytes=VMEM_LIMIT),
        name="hgrn2_mixer_ln",
    )(x2d, w_in, w_out, lb_logits, norm_g, ln_g, ln_b)


def kernel(x, attn_w_in, attn_w_out, hgrn_w_in, hgrn_w_out, hgrn_norm_g, lb_logits,
           ln_mix_g, ln_mix_b, ln_ffn_g, ln_ffn_b, ffn_w_up, ffn_w_down):
    batch, seq, d = x.shape
    assert d == D_MODEL and lb_logits.shape[0] == DEPTH
    for window, dilation in DILATED_PATTERNS:
        assert window // dilation == ATTN_BLK and seq % window == 0
        assert QKV_TM % (dilation * BF16_ROWS) == 0 and OUT_TM % (dilation * SUBLANES) == 0
        assert dilation % max(1, ATTN_TQ // (seq // dilation)) == 0
    assert seq % QKV_TM == 0 and seq % OUT_TM == 0 and OUT_TM % OUT_SUB == 0
    assert OUT_SUB % (MAX_DILATION * SUBLANES) == 0
    assert seq % HGRN_TM == 0 and HGRN_TM % HGRN_SUB == 0 and HGRN_SUB % HGRN_CHUNK == 0
    h = x.reshape(batch * seq, d)
    row = lambda a: a.reshape(1, -1)
    for i in range(DEPTH):
        j = i // 2
        if i % 2 == 0:
            os_, stats = [], []
            for g, (_, dil) in enumerate(DILATED_PATTERNS):
                qkv = _qkv_rope(h, attn_w_in, j, g, dil, batch, seq)
                o, st = _attention_group(qkv, dil, batch, seq)
                os_.append(o)
                stats.append(st)
            h = _attn_out(os_, stats, h, attn_w_out, j, row(ln_mix_g[i]), row(ln_mix_b[i]),
                          batch, seq)
        else:
            h = _hgrn_mixer(i, j, h, hgrn_w_in, hgrn_w_out, lb_logits, row(hgrn_norm_g[j]),
                            row(ln_mix_g[i]), row(ln_mix_b[i]), batch, seq)
        h = _ffn(h, i, ffn_w_up, ffn_w_down, row(ln_ffn_g[i]), row(ln_ffn_b[i]))
    return h.reshape(batch, seq, d)
```

```python
import functools
import math

import jax
import jax.numpy as jnp
from jax import lax
from jax.experimental import pallas as pl
from jax.experimental.pallas import tpu as pltpu
import numpy as np

F32 = jnp.float32
BF16 = jnp.bfloat16

D_MODEL = 1024
DEPTH = 2
ATTN_HEAD_DIM = 64
ATTN_HEADS = D_MODEL // ATTN_HEAD_DIM
DILATED_PATTERNS = ((128, 1), (512, 4), (2048, 16))
MAX_DILATION = max(d for _, d in DILATED_PATTERNS)
ROPE_THETA = 10000.0
HGRN_HEADS = 8
HGRN_DK = 128
HGRN_DV = 128
HGRN_CHUNK = 64
D_FF = 4 * D_MODEL
LN_EPS = 1e-5
RMS_EPS = 1e-6
DEEPNORM_ALPHA = (2 * DEPTH) ** 0.25

LANES = 128
SUBLANES = 8
BF16_ROWS = 2 * SUBLANES
ATTN_BLK = 128
HALF = ATTN_HEAD_DIM // 2
MASK_VALUE = -1e30
LN2 = math.log(2.0)
Q_SCALE = ATTN_HEAD_DIM ** -0.5 / LN2
VMEM_LIMIT = 56 * 1024 * 1024
MAX_ROW_STRIDE = 4
W_CHUNK_ROWS = 1024
W_CHUNK_COLS = 256
W_SLOTS = 4

QKV_TM = 1024
QKV_SUB = 512
ATTN_TQ = 1024
OUT_TM = 512
OUT_SUB = 256
FFN_TM = 1024
FFN_SUB = 256
HGRN_TM = 1024
HGRN_SUB = 256
HGRN_STAGGER = 1


def _layer_norm(y, g, b):
    mu = jnp.mean(y, axis=-1, keepdims=True)
    d = y - mu
    var = jnp.mean(d * d, axis=-1, keepdims=True)
    return d * lax.rsqrt(var + LN_EPS) * g + b


def _resident(shape):
    nd = len(shape)
    return pl.BlockSpec(shape, lambda *_: (0,) * nd, pipeline_mode=pl.Buffered(1))


_HBM = pl.BlockSpec(memory_space=pl.ANY)


def _first_step(grid_rank):
    ids = [pl.program_id(a) == 0 for a in range(grid_rank)]
    return functools.reduce(jnp.logical_and, ids)


def _run_staggered(chains, stagger):
    active, pending, tick = [], list(chains), 0
    while active or pending:
        if pending and tick % stagger == 0:
            active.append(pending.pop(0))
        tick += 1
        for gen in list(active):
            try:
                next(gen)
            except StopIteration:
                active.remove(gen)


def _weight_scratch():
    return [pltpu.VMEM((W_SLOTS, W_CHUNK_ROWS, W_CHUNK_COLS), F32),
            pltpu.SemaphoreType.DMA((W_SLOTS,))]


def _load_weights(chunks, stage_ref, sem_ref, store):
    copies = [pltpu.make_async_copy(src, stage_ref.at[i % W_SLOTS], sem_ref.at[i % W_SLOTS])
              for i, src in enumerate(chunks)]
    ahead = W_SLOTS - 1
    for cp in copies[:ahead]:
        cp.start()
    for i, cp in enumerate(copies):
        if i + ahead < len(copies):
            copies[i + ahead].start()
        cp.wait()
        store(i, stage_ref[i % W_SLOTS])


def _weight_chunks(w_hbm, lead, n_rows, col0, n_cols):
    out = []
    for r in range(0, n_rows, W_CHUNK_ROWS):
        for c in range(0, n_cols, W_CHUNK_COLS):
            view = w_hbm.at[(*lead, pl.ds(r, W_CHUNK_ROWS), pl.ds(col0 + c, W_CHUNK_COLS))]
            out.append((view, r, c))
    return out


def _load_plain_weights(w_hbm, lead, w_ref, stage_ref, sem_ref):
    chunks = _weight_chunks(w_hbm, lead, w_ref.shape[0], 0, w_ref.shape[1])

    def store(i, val):
        _, r, c = chunks[i]
        w_ref[r:r + W_CHUNK_ROWS, c:c + W_CHUNK_COLS] = val.astype(BF16)

    _load_weights([v for v, _, _ in chunks], stage_ref, sem_ref, store)


def _row_passes(dilation):
    passes, left = [], dilation
    while left > 1:
        passes.append(min(left, MAX_ROW_STRIDE))
        left //= passes[-1]
    return passes


def _qkv_rope_kernel(dilation, layer, g, *refs):
    n_chunks = D_MODEL // LANES
    n_x = 1 if dilation == 1 else n_chunks
    x_refs = refs[:n_x]
    w_hbm, tab_ref, o_ref, xb_ref, xs_ref, w_ref, stage_ref, sem_ref = refs[n_x:]
    tm = x_refs[0].shape[0]
    n_per = tm // dilation

    @pl.when(_first_step(2))
    def _():
        chunks = _weight_chunks(w_hbm, (layer,), D_MODEL, 3 * g * D_MODEL, 3 * D_MODEL)
        lane = lax.broadcasted_iota(jnp.int32, (1, LANES), 1)
        from_right = (lane >= HALF) & (lane < 2 * HALF)
        from_left = (lane >= 2 * HALF) & (lane < 3 * HALF)

        def store(i, val):
            _, _, c = chunks[i]
            if c < 2 * D_MODEL:
                parts = []
                for j in range(W_CHUNK_COLS // LANES):
                    a = val[:, j * LANES:(j + 1) * LANES]
                    parts.append(jnp.where(from_right, pltpu.roll(a, LANES - HALF, 1),
                                           jnp.where(from_left, pltpu.roll(a, HALF, 1), a)))
                val = jnp.concatenate(parts, axis=1)
            w_ref[:, c:c + W_CHUNK_COLS] = val.astype(BF16)

        _load_weights([v for v, _, _ in chunks], stage_ref, sem_ref, store)

    if dilation == 1:
        xb_ref[...] = x_refs[0][...].astype(BF16)
    else:
        passes = _row_passes(dilation)
        n_slabs = xs_ref.shape[0]
        blocks = 1
        for i, st in enumerate(passes):
            rows_blk = tm // blocks
            for blk in range(blocks):
                for r in range(st):
                    lo = (blk + r * blocks) * (rows_blk // st)
                    dst = slice(lo, lo + rows_blk // st)
                    rows = pl.ds(blk * rows_blk + r, rows_blk // st, stride=st)
                    parts = [x_refs[c][rows, :] if i == 0 else xs_ref[(i - 1) % n_slabs, c, rows, :]
                             for c in range(n_chunks)]
                    if i == len(passes) - 1:
                        xb_ref[dst, :] = jnp.concatenate([v.astype(BF16) for v in parts], axis=1)
                    else:
                        for c in range(n_chunks):
                            xs_ref[i % n_slabs, c, dst, :] = parts[c]
            blocks *= st

    def store(kind, s, val):
        if n_per >= QKV_SUB:
            start = s * QKV_SUB
            o_ref[kind, start // n_per, start % n_per:start % n_per + QKV_SUB, :] = val
        else:
            per = QKV_SUB // n_per
            for c in range(per):
                o_ref[kind, s * per + c] = val[c * n_per:(c + 1) * n_per]

    for kind in range(3):
        cols = slice(kind * D_MODEL, (kind + 1) * D_MODEL)
        for s in range(tm // QKV_SUB):
            rows = slice(s * QKV_SUB, (s + 1) * QKV_SUB)
            acc = jnp.dot(xb_ref[rows], w_ref[:, cols], preferred_element_type=F32)
            if kind == 2:
                store(kind, s, acc.astype(BF16))
                continue
            cos = tab_ref[0, rows, :]
            sin = tab_ref[1, rows, :]
            if kind == 0:
                cos = cos * Q_SCALE
                sin = sin * Q_SCALE
            pieces = []
            for c in range(n_chunks):
                a = acc[:, c * LANES:(c + 1) * LANES]
                pieces.append((a * cos + pltpu.roll(a, LANES // 2, 1) * sin).astype(BF16))
            store(kind, s, jnp.concatenate(pieces, axis=1))


def _rope_table(seq, dilation, tm):
    inv = ROPE_THETA ** (-np.arange(HALF, dtype=np.float64) * (2.0 / ATTN_HEAD_DIM))
    ang = np.arange(seq, dtype=np.float64)[:, None] * inv[None, :]
    cos = np.tile(np.cos(ang), (1, LANES // HALF))
    sin = np.tile(np.sin(ang), (1, LANES // HALF))
    sign = np.where(np.arange(LANES) < LANES // 2, -1.0, 1.0)
    tab = np.stack([cos, sin * sign])
    tab = tab.reshape(2, seq // tm, tm // dilation, dilation, LANES)
    tab = tab.transpose(0, 1, 3, 2, 4).reshape(2, seq, LANES)
    return jnp.asarray(tab.astype(np.float32))


def _qkv_rope(x2d, w_in, layer, g, dilation, batch, seq):
    tm = QKV_TM
    tiles = seq // tm
    n_per = tm // dilation
    tab = _rope_table(seq, dilation, tm)
    if dilation == 1:
        x_specs = [pl.BlockSpec((tm, D_MODEL), lambda b, i: (b * tiles + i, 0))]
    else:
        x_specs = [pl.BlockSpec((tm, LANES), functools.partial(lambda b, i, c: (b * tiles + i, c), c=c))
                   for c in range(D_MODEL // LANES)]
    n_slabs = max(1, len(_row_passes(dilation)) - 1)
    return pl.pallas_call(
        functools.partial(_qkv_rope_kernel, dilation, layer, g),
        out_shape=jax.ShapeDtypeStruct((3, batch, dilation, seq // dilation, D_MODEL), BF16),
        grid=(batch, tiles),
        in_specs=x_specs + [_HBM, pl.BlockSpec((2, tm, LANES), lambda b, i: (0, i, 0))],
        out_specs=pl.BlockSpec((3, None, dilation, n_per, D_MODEL), lambda b, i: (0, b, 0, i, 0)),
        scratch_shapes=[pltpu.VMEM((tm, D_MODEL), BF16),
                        pltpu.VMEM((n_slabs, D_MODEL // LANES, tm, LANES), F32),
                        pltpu.VMEM((D_MODEL, 3 * D_MODEL), BF16)] + _weight_scratch(),
        compiler_params=pltpu.CompilerParams(
            dimension_semantics=("arbitrary", "arbitrary"), vmem_limit_bytes=VMEM_LIMIT),
        name=f"qkv_rope_d{dilation}",
    )(*([x2d] * len(x_specs)), w_in, tab)


def _attn_kernel(q_ref, kp_ref, kc_ref, vp_ref, vc_ref, o_ref, stat_ref):
    i = pl.program_id(2)
    blk = ATTN_BLK
    n_cls, tq = q_ref.shape[0], q_ref.shape[1]
    row = lax.broadcasted_iota(jnp.int32, (2 * blk, 2 * blk), 0) % blk
    col = lax.broadcasted_iota(jnp.int32, (2 * blk, 2 * blk), 1)
    valid = (col >= row) & (col <= row + blk)
    bias = jnp.where(valid, 0.0, MASK_VALUE).astype(F32)
    bias_first = jnp.where(valid & ((col >= blk) | (i > 0)), 0.0, MASK_VALUE).astype(F32)
    lane = lax.broadcasted_iota(jnp.int32, (blk, LANES), 1)
    qk_head0 = ((lane // HALF) % 2 == 0).astype(F32).astype(BF16)
    qk_head1 = ((lane // HALF) % 2 == 1).astype(F32).astype(BF16)
    v_head0 = lane < ATTN_HEAD_DIM
    ones = jnp.ones((2 * blk, LANES), BF16)
    for cls, qb in [(c, b) for c in range(n_cls) for b in range(tq // blk)]:
        rows = slice(qb * blk, (qb + 1) * blk)
        prev_rows = slice((qb - 1) * blk, qb * blk)
        stat = jnp.zeros((blk, LANES), F32)
        for p in range(ATTN_HEADS // 2):
            sl = slice(p * LANES, (p + 1) * LANES)
            q = q_ref[cls, rows, sl]
            qs = jnp.concatenate([q * qk_head0, q * qk_head1], axis=0)
            k_prev = kp_ref[cls, :, sl] if qb == 0 else kc_ref[cls, prev_rows, sl]
            v_prev = vp_ref[cls, :, sl] if qb == 0 else vc_ref[cls, prev_rows, sl]
            k = jnp.concatenate([k_prev, kc_ref[cls, rows, sl]], axis=0)
            v = jnp.concatenate([v_prev, vc_ref[cls, rows, sl]], axis=0)
            s = lax.dot_general(qs, k, (((1,), (1,)), ((), ())), preferred_element_type=F32)
            s = s + (bias_first if qb == 0 else bias)
            m = jnp.max(s, axis=-1, keepdims=True)
            e = jnp.exp2(s - m).astype(BF16)
            pv = jnp.dot(e, jnp.concatenate([v, ones], axis=1),
                         preferred_element_type=F32)
            l_rep = pv[:, LANES:]
            o_ref[cls, p, rows, :] = jnp.where(v_head0, pv[:blk, :LANES], pv[blk:, :LANES])
            stat = jnp.where(lane == 2 * p, m[:blk], stat)
            stat = jnp.where(lane == 2 * p + 1, m[blk:], stat)
            stat = jnp.where(lane == ATTN_HEADS + 2 * p, l_rep[:blk], stat)
            stat = jnp.where(lane == ATTN_HEADS + 2 * p + 1, l_rep[blk:], stat)
        stat_ref[cls, rows, :] = stat


def _attention_group(qkv, dilation, batch, seq):
    n = seq // dilation
    tq = min(ATTN_TQ, n)
    n_cls = ATTN_TQ // tq
    per = tq // ATTN_BLK
    pairs = ATTN_HEADS // 2

    def cur(which):
        return pl.BlockSpec((None, None, n_cls, tq, D_MODEL), lambda b, r, i: (which, b, r, i, 0))

    def prev(which):
        return pl.BlockSpec((None, None, n_cls, ATTN_BLK, D_MODEL),
                            lambda b, r, i: (which, b, r, jnp.maximum(i * per - 1, 0), 0))

    return pl.pallas_call(
        _attn_kernel,
        out_shape=(jax.ShapeDtypeStruct((batch, dilation, pairs, n, LANES), F32),
                   jax.ShapeDtypeStruct((batch, dilation, n, LANES), F32)),
        grid=(batch, dilation // n_cls, n // tq),
        in_specs=[cur(0), prev(1), cur(1), prev(2), cur(2)],
        out_specs=(pl.BlockSpec((None, n_cls, pairs, tq, LANES), lambda b, r, i: (b, r, 0, i, 0)),
                   pl.BlockSpec((None, n_cls, tq, LANES), lambda b, r, i: (b, r, i, 0))),
        compiler_params=pltpu.CompilerParams(
            dimension_semantics=("arbitrary", "arbitrary", "arbitrary"),
            vmem_limit_bytes=VMEM_LIMIT),
        name=f"dilated_attn_d{dilation}",
    )(qkv, qkv, qkv, qkv, qkv)


def _class_rows(ref, lead, dilation, r16, sub, n_sub):
    step = MAX_DILATION // dilation
    if step == 1:
        return ref[(r16, *lead, slice(sub * n_sub, (sub + 1) * n_sub))]
    start = sub * n_sub * step + r16 // dilation
    return ref[(r16 % dilation, *lead, pl.ds(start, n_sub, stride=step), slice(None))]


def _attn_out_chain(sub, o_refs, s_refs, x_ref, w_ref, ex_ref, g_ref, b_ref, y_ref, proj_ref):
    n_sub = OUT_SUB // MAX_DILATION
    n_chunks = D_MODEL // LANES
    classes = range(MAX_DILATION)

    ms = [jnp.concatenate([_class_rows(ref, (), d, r, sub, n_sub) for r in classes], axis=0)
          for ref, d in s_refs]
    ls = [pltpu.roll(v, LANES - ATTN_HEADS, 1) for v in ms]
    mx = jnp.maximum(jnp.maximum(ms[0], ms[1]), ms[2])
    es = [jnp.exp2(v - mx) for v in ms]
    inv = 1.0 / (ls[0] * es[0] + ls[1] * es[1] + ls[2] * es[2])
    head_lane = lax.broadcasted_iota(jnp.int32, (OUT_SUB, LANES), 1) < ATTN_HEADS
    halves = []
    for e in es:
        w = jnp.where(head_lane, e * inv, 0.0)
        hi = w.astype(BF16)
        halves.append(jnp.concatenate([hi, (w - hi.astype(F32)).astype(BF16)], axis=1))
    pieces = []
    for pp in range(ATTN_HEADS // 4):
        cols = slice(2 * pp * LANES, (2 * pp + 2) * LANES)
        mixed = jnp.zeros((OUT_SUB, 2 * LANES), F32)
        for g, (ref, d) in enumerate(o_refs):
            w_wide = jnp.dot(halves[g], ex_ref[:, cols], preferred_element_type=F32)
            o_g = jnp.concatenate(
                [jnp.concatenate([_class_rows(ref, (p,), d, r, sub, n_sub) for r in classes], axis=0)
                 for p in (2 * pp, 2 * pp + 1)], axis=1)
            mixed = mixed + w_wide * o_g
        pieces.append(mixed.astype(BF16))
    mix = jnp.concatenate(pieces, axis=1)
    yield
    proj = jnp.dot(mix, w_ref[...], preferred_element_type=F32)
    pitch = proj_ref.shape[2] // MAX_DILATION
    for c in range(n_chunks):
        for r in classes:
            proj_ref[sub, c, r * pitch:r * pitch + n_sub, :] = proj[r * n_sub:(r + 1) * n_sub,
                                                                    c * LANES:(c + 1) * LANES]
    yield
    g = g_ref[...]
    b = b_ref[...]
    for n in range(n_sub):
        lo = sub * OUT_SUB + n * MAX_DILATION
        tok = slice(lo, lo + MAX_DILATION)
        y = jnp.concatenate([proj_ref[sub, c, pl.ds(n, MAX_DILATION, stride=pitch), :]
                             for c in range(n_chunks)], axis=1)
        y_ref[tok, :] = _layer_norm(DEEPNORM_ALPHA * x_ref[tok, :] + y, g, b)


def _attn_out_kernel(layer, o0_ref, o1_ref, o2_ref, s0_ref, s1_ref, s2_ref, x_ref, w_hbm, ex_ref,
                     g_ref, b_ref, y_ref, proj_ref, o0s_ref, s0s_ref, w_ref, stage_ref, sem_ref):
    tm = x_ref.shape[0]

    @pl.when(_first_step(2))
    def _():
        _load_plain_weights(w_hbm, (layer,), w_ref, stage_ref, sem_ref)

    pairs = ATTN_HEADS // 2
    mid = DILATED_PATTERNS[1][1]
    assert [d for _, d in DILATED_PATTERNS] == [1, mid, MAX_DILATION]

    for r in range(mid):
        s0s_ref[r, :, :] = s0_ref[0, pl.ds(r, tm // mid, stride=mid), :]
        for p in range(pairs):
            o0s_ref[r, p, :, :] = o0_ref[0, p, pl.ds(r, tm // mid, stride=mid), :]
    o_refs = ((o0s_ref, mid), (o1_ref, mid), (o2_ref, MAX_DILATION))
    s_refs = ((s0s_ref, mid), (s1_ref, mid), (s2_ref, MAX_DILATION))
    _run_staggered([_attn_out_chain(sub, o_refs, s_refs, x_ref, w_ref, ex_ref, g_ref, b_ref, y_ref,
                                    proj_ref) for sub in range(tm // OUT_SUB)], 1)


def _head_expansion():
    e = (np.arange(D_MODEL)[None, :] // ATTN_HEAD_DIM == np.arange(LANES)[:, None])
    return jnp.asarray(np.concatenate([e, e], axis=0).astype(np.float32), dtype=BF16)


def _attn_out(os_, stats, x2d, w_out, layer, ln_g, ln_b, batch, seq):
    t = x2d.shape[0]
    tm = OUT_TM
    tiles = seq // tm
    pairs = ATTN_HEADS // 2
    dils = [d for _, d in DILATED_PATTERNS]
    o_spec = lambda d: pl.BlockSpec((None, d, pairs, tm // d, LANES), lambda b, i: (b, 0, 0, i, 0))
    s_spec = lambda d: pl.BlockSpec((None, d, tm // d, LANES), lambda b, i: (b, 0, i, 0))
    row = pl.BlockSpec((tm, D_MODEL), lambda b, i: (b * tiles + i, 0))
    return pl.pallas_call(
        functools.partial(_attn_out_kernel, layer),
        out_shape=jax.ShapeDtypeStruct((t, D_MODEL), F32),
        grid=(batch, tiles),
        in_specs=([o_spec(d) for d in dils] + [s_spec(d) for d in dils]
                  + [row, _HBM, _resident((2 * LANES, D_MODEL)),
                     _resident((1, D_MODEL)), _resident((1, D_MODEL))]),
        out_specs=row,
        scratch_shapes=[pltpu.VMEM((tm // OUT_SUB, D_MODEL // LANES,
                                    OUT_SUB + SUBLANES * MAX_DILATION, LANES), F32),
                        pltpu.VMEM((dils[1], pairs, tm // dils[1], LANES), F32),
                        pltpu.VMEM((dils[1], tm // dils[1], LANES), F32),
                        pltpu.VMEM((D_MODEL, D_MODEL), BF16)] + _weight_scratch(),
        compiler_params=pltpu.CompilerParams(
            dimension_semantics=("arbitrary", "arbitrary"), vmem_limit_bytes=VMEM_LIMIT),
        name="attn_out_ln",
    )(*os_, *stats, x2d, w_out, _head_expansion(), ln_g, ln_b)


def _ffn_kernel(layer, x_ref, wu_hbm, wd_hbm, g_ref, b_ref, y_ref, h_ref, wu_ref, wd_ref,
                stage_ref, sem_ref):
    @pl.when(_first_step(1))
    def _():
        _load_plain_weights(wu_hbm, (layer,), wu_ref, stage_ref, sem_ref)
        _load_plain_weights(wd_hbm, (layer,), wd_ref, stage_ref, sem_ref)

    for t in range(x_ref.shape[0] // FFN_SUB):
        rows = slice(t * FFN_SUB, (t + 1) * FFN_SUB)
        x = x_ref[rows, :]
        xb = x.astype(BF16)
        for c in range(D_FF // D_MODEL):
            sl = slice(c * D_MODEL, (c + 1) * D_MODEL)
            h = jnp.dot(xb, wu_ref[:, sl], preferred_element_type=F32)
            h_ref[rows, sl] = jnp.square(jnp.maximum(h, 0.0)).astype(BF16)
        y = jnp.dot(h_ref[rows, :], wd_ref[...], preferred_element_type=F32)
        y_ref[rows, :] = _layer_norm(DEEPNORM_ALPHA * x + y, g_ref[...], b_ref[...])


def _ffn(x2d, layer, w_up, w_down, ln_g, ln_b):
    t = x2d.shape[0]
    tm = FFN_TM
    return pl.pallas_call(
        functools.partial(_ffn_kernel, layer),
        out_shape=jax.ShapeDtypeStruct((t, D_MODEL), F32),
        grid=(t // tm,),
        in_specs=[pl.BlockSpec((tm, D_MODEL), lambda i: (i, 0)),
                  _HBM, _HBM, _resident((1, D_MODEL)), _resident((1, D_MODEL))],
        out_specs=pl.BlockSpec((tm, D_MODEL), lambda i: (i, 0)),
        scratch_shapes=[pltpu.VMEM((tm, D_FF), BF16), pltpu.VMEM((D_MODEL, D_FF), BF16),
                        pltpu.VMEM((D_FF, D_MODEL), BF16)] + _weight_scratch(),
        compiler_params=pltpu.CompilerParams(
            dimension_semantics=("arbitrary",), vmem_limit_bytes=VMEM_LIMIT),
        name="ffn_ln",
    )(x2d, w_up, w_down, ln_g, ln_b)


def _hgrn_pair_chain(xb, wi_ref, p, lb, tri, block_causal, ng, states, on_ref, out_rows, t, done):
    hk = HGRN_HEADS * HGRN_DK
    dk = HGRN_DK
    c_len = HGRN_CHUNK
    pw = 2 * dk
    rows = xb.shape[0]
    n_chunks = rows // c_len
    cols = slice(p * pw, (p + 1) * pw)
    contract_last = (((1,), (1,)), ((), ()))
    contract_rows = (((0,), (0,)), ((), ()))

    q_raw = jnp.dot(xb, wi_ref[:, p * pw:(p + 1) * pw], preferred_element_type=F32)
    z = jnp.dot(xb, wi_ref[:, hk + p * pw:hk + (p + 1) * pw], preferred_element_type=F32)
    v = jnp.dot(xb, wi_ref[:, 2 * hk + p * pw:2 * hk + (p + 1) * pw], preferred_element_type=F32)
    yield
    lb_p = lb[:, cols]
    e_z = jnp.exp(z)
    key = (1.0 - lb_p) / (1.0 + e_z)
    log_f = jnp.log(lb_p + (1.0 - lb_p) / (1.0 + 1.0 / e_z))
    q = q_raw / (1.0 + jnp.exp(-q_raw))
    v_b = v.astype(BF16)
    hi = log_f.astype(BF16)
    lo = (log_f - hi.astype(F32)).astype(BF16)
    yield
    bcum = (jnp.dot(tri, hi, preferred_element_type=F32)
            + jnp.dot(tri, lo, preferred_element_type=F32))
    yield
    last = [bcum[(c + 1) * c_len - 1:(c + 1) * c_len] for c in range(n_chunks)]
    b_last = jnp.concatenate([jnp.broadcast_to(r, (c_len, pw)) for r in last], axis=0)
    q_dec = (q * jnp.exp(bcum)).astype(BF16)
    k_dec = (key * jnp.exp(-bcum)).astype(BF16)
    k_end = (key * jnp.exp(b_last - bcum)).astype(BF16)
    yield
    scores = [lax.dot_general(q_dec[:, hh * dk:(hh + 1) * dk], k_dec[:, hh * dk:(hh + 1) * dk],
                              contract_last, preferred_element_type=F32) for hh in range(2)]
    yield
    intra = [jnp.dot(jnp.where(block_causal, scores[hh], 0.0).astype(BF16),
                     v_b[:, hh * dk:(hh + 1) * dk], preferred_element_type=F32) for hh in range(2)]
    yield
    zero_st = jnp.zeros((HGRN_DV, dk), BF16)
    zero_k = jnp.zeros((c_len, dk), BF16)
    kvs = []
    for c in range(n_chunks):
        rs = slice(c * c_len, (c + 1) * c_len)
        v_rows = jnp.concatenate([v_b[rs, :dk], v_b[rs, dk:]], axis=0)
        k_rows = jnp.concatenate(
            [jnp.concatenate([k_end[rs, :dk], zero_k], axis=1),
             jnp.concatenate([zero_k, k_end[rs, dk:]], axis=1)], axis=0)
        kvs.append(lax.dot_general(v_rows, k_rows, contract_rows,
                                   preferred_element_type=F32))
        if c % 2 == 1:
            yield
    while t > 0 and (t - 1, p) not in done:
        yield
    st0, st1 = states[2 * p], states[2 * p + 1]
    inter = []
    for c in range(n_chunks):
        rs = slice(c * c_len, (c + 1) * c_len)
        st_pair = jnp.concatenate(
            [jnp.concatenate([st0.astype(BF16), zero_st], axis=1),
             jnp.concatenate([zero_st, st1.astype(BF16)], axis=1)], axis=0)
        inter.append(lax.dot_general(q_dec[rs], st_pair, contract_last,
                                     preferred_element_type=F32))
        decay = jnp.exp(last[c])
        st0 = decay[:, :dk] * st0 + kvs[c][:, :dk]
        st1 = decay[:, dk:] * st1 + kvs[c][:, dk:]
        yield
    states[2 * p], states[2 * p + 1] = st0, st1
    inter = jnp.concatenate(inter, axis=0)
    outs = []
    for hh in range(2):
        ls = slice(hh * dk, (hh + 1) * dk)
        o = intra[hh] + inter[:, ls]
        o = o * lax.rsqrt(jnp.mean(o * o, axis=-1, keepdims=True) + RMS_EPS) * ng[:, cols][:, ls]
        outs.append(o.astype(BF16))
    on_ref[out_rows, cols] = jnp.concatenate(outs, axis=1)
    done.add((t, p))


def _hgrn_out_chain(x_ref, on_ref, wo_ref, g_ref, b_ref, y_ref, rows, t, done):
    while any((t, p) not in done for p in range(HGRN_HEADS // 2)):
        yield
    y = jnp.dot(on_ref[rows, :], wo_ref[...], preferred_element_type=F32)
    yield
    y_ref[rows, :] = _layer_norm(DEEPNORM_ALPHA * x_ref[rows, :] + y, g_ref[...], b_ref[...])


def _hgrn_kernel(layer, w_layer, sub, x_ref, wi_hbm, wo_hbm, lbl_ref, ng_ref, g_ref, b_ref, y_ref,
                 state_ref, on_ref, wi_ref, wo_ref, stage_ref, sem_ref):
    tm = x_ref.shape[0]
    c_len = HGRN_CHUNK

    @pl.when(_first_step(2))
    def _():
        _load_plain_weights(wi_hbm, (w_layer,), wi_ref, stage_ref, sem_ref)
        _load_plain_weights(wo_hbm, (w_layer,), wo_ref, stage_ref, sem_ref)

    @pl.when(pl.program_id(1) == 0)
    def _():
        state_ref[...] = jnp.zeros_like(state_ref)

    logits = lbl_ref[...]
    ex = jnp.exp(logits - jnp.max(logits, axis=0, keepdims=True))
    sm = ex / jnp.sum(ex, axis=0, keepdims=True)
    lb = jnp.sum(sm[1:layer + 1], axis=0, keepdims=True)

    ri = lax.broadcasted_iota(jnp.int32, (sub, sub), 0)
    ci = lax.broadcasted_iota(jnp.int32, (sub, sub), 1)
    block_causal = (ri // c_len == ci // c_len) & (ci <= ri)
    tri = block_causal.astype(F32).astype(BF16)
    ng = ng_ref[...]
    states = [state_ref[h] for h in range(HGRN_HEADS)]

    chains, done = [], set()
    for t in range(tm // sub):
        rows = slice(t * sub, (t + 1) * sub)
        xb = x_ref[rows, :].astype(BF16)
        for p in range(HGRN_HEADS // 2):
            chains.append(_hgrn_pair_chain(xb, wi_ref, p, lb, tri, block_causal, ng, states,
                                           on_ref, rows, t, done))
        chains.append(_hgrn_out_chain(x_ref, on_ref, wo_ref, g_ref, b_ref, y_ref, rows, t, done))
    _run_staggered(chains, HGRN_STAGGER)
    for h in range(HGRN_HEADS):
        state_ref[h] = states[h]


def _hgrn_mixer(layer, w_layer, x2d, w_in, w_out, lb_logits, norm_g, ln_g, ln_b, batch, seq):
    t = x2d.shape[0]
    tm = HGRN_TM
    tiles = seq // tm
    d_in = w_in.shape[-1]
    row = pl.BlockSpec((tm, D_MODEL), lambda b, i: (b * tiles + i, 0))
    return pl.pallas_call(
        functools.partial(_hgrn_kernel, layer, w_layer, HGRN_SUB),
        out_shape=jax.ShapeDtypeStruct((t, D_MODEL), F32),
        grid=(batch, tiles),
        in_specs=[row, _HBM, _HBM, _resident((DEPTH, D_MODEL)), _resident((1, D_MODEL)),
                  _resident((1, D_MODEL)), _resident((1, D_MODEL))],
        out_specs=row,
        scratch_shapes=[pltpu.VMEM((HGRN_HEADS, HGRN_DV, HGRN_DK), F32),
                        pltpu.VMEM((tm, D_MODEL), BF16), pltpu.VMEM((D_MODEL, d_in), BF16),
                        pltpu.VMEM((D_MODEL, D_MODEL), BF16)] + _weight_scratch(),
        compiler_params=pltpu.CompilerParams(
            dimension_semantics=("arbitrary", "arbitrary"), vmem_limit_bytes=VMEM_LIMIT),
        name="hgrn2_mixer_ln",
    )(x2d, w_in, w_out, lb_logits, norm_g, ln_g, ln_b)


def kernel(x, attn_w_in, attn_w_out, hgrn_w_in, hgrn_w_out, hgrn_norm_g, lb_logits,
           ln_mix_g, ln_mix_b, ln_ffn_g, ln_ffn_b, ffn_w_up, ffn_w_down):
    batch, seq, d = x.shape
    assert d == D_MODEL and lb_logits.shape[0] == DEPTH
    for window, dilation in DILATED_PATTERNS:
        assert window // dilation == ATTN_BLK and seq % window == 0
        assert QKV_TM % (dilation * BF16_ROWS) == 0 and OUT_TM % (dilation * SUBLANES) == 0
        assert dilation % max(1, ATTN_TQ // (seq // dilation)) == 0
    assert seq % QKV_TM == 0 and seq % OUT_TM == 0 and OUT_TM % OUT_SUB == 0
    assert OUT_SUB % (MAX_DILATION * SUBLANES) == 0
    assert seq % HGRN_TM == 0 and HGRN_TM % HGRN_SUB == 0 and HGRN_SUB % HGRN_CHUNK == 0
    h = x.reshape(batch * seq, d)
    row = lambda a: a.reshape(1, -1)
    for i in range(DEPTH):
        j = i // 2
        if i % 2 == 0:
            os_, stats = [], []
            for g, (_, dil) in enumerate(DILATED_PATTERNS):
                qkv = _qkv_rope(h, attn_w_in, j, g, dil, batch, seq)
                o, st = _attention_group(qkv, dil, batch, seq)
                os_.append(o)
                stats.append(st)
            h = _attn_out(os_, stats, h, attn_w_out, j, row(ln_mix_g[i]), row(ln_mix_b[i]),
                          batch, seq)
        else:
            h = _hgrn_mixer(i, j, h, hgrn_w_in, hgrn_w_out, lb_logits, row(hgrn_norm_g[j]),
                            row(ln_mix_g[i]), row(ln_mix_b[i]), batch, seq)
        h = _ffn(h, i, ffn_w_up, ffn_w_down, row(ln_ffn_g[i]), row(ln_ffn_b[i]))
    return h.reshape(batch, seq, d)
```
